```python
import jax, jax.numpy as jnp
from jax import lax
import numpy as np

D_MODEL = 1024
BATCH = 4
SEQ = 4096
DEPTH = 2
DEC_BATCH = 128
DEC_SEQ = 1
PAST_LEN = 16384
PAGE_SIZE = 128

HEAD_DIM = 64
MIX_WIDTH = 384
N_GROUPS = MIX_WIDTH // HEAD_DIM
N_BRANCH = 4
CONV_WIDTH = 3
SWA_WINDOW = 128
SWA_Q_HEADS = 6
SWA_KV_HEADS = 2
SWA_REP = SWA_Q_HEADS // SWA_KV_HEADS
CHUNK = 128
DIL_GROUPS = ((128, 1), (512, 4), (2048, 16))
DIL_HEADS = N_GROUPS // len(DIL_GROUPS)
BLOCK = 128
ROPE_THETA = 10000.0
EPS = 1e-6
NEG_INF = -1e30

SPLIT_SIZES = (MIX_WIDTH, MIX_WIDTH, MIX_WIDTH, MIX_WIDTH,
               SWA_Q_HEADS * HEAD_DIM, SWA_KV_HEADS * HEAD_DIM, SWA_KV_HEADS * HEAD_DIM, MIX_WIDTH,
               MIX_WIDTH, MIX_WIDTH, MIX_WIDTH,
               MIX_WIDTH, MIX_WIDTH, MIX_WIDTH, MIX_WIDTH)
IN_COLS = sum(SPLIT_SIZES)
SPLIT_POINTS = tuple(int(s) for s in np.cumsum(SPLIT_SIZES)[:-1])

kernel_name = "hybrid_parallel_gated_mixers_step"


def rms_norm(x, g):
    x32 = x.astype(jnp.float32)
    y = x32 * lax.rsqrt(jnp.mean(x32 * x32, axis=-1, keepdims=True) + EPS)
    return (y * g.astype(jnp.float32)).astype(x.dtype)


def layer_norm(x, g, b):
    x32 = x.astype(jnp.float32)
    mu = jnp.mean(x32, axis=-1, keepdims=True)
    var = jnp.mean(jnp.square(x32 - mu), axis=-1, keepdims=True)
    y = (x32 - mu) * lax.rsqrt(var + EPS)
    return (y * g.astype(jnp.float32) + b.astype(jnp.float32)).astype(x.dtype)


def silu(x):
    return x * jax.nn.sigmoid(x)


def rope(x, pos):
    half = x.shape[-1] // 2
    inv_freq = ROPE_THETA ** (-jnp.arange(half, dtype=jnp.float32) / half)
    ang = pos.astype(jnp.float32)[:, None] * inv_freq[None, :]
    cos = jnp.cos(ang)[:, None, :]
    sin = jnp.sin(ang)[:, None, :]
    x32 = x.astype(jnp.float32)
    x1, x2 = x32[..., :half], x32[..., half:]
    return jnp.concatenate([x1 * cos - x2 * sin, x2 * cos + x1 * sin], axis=-1).astype(x.dtype)


def masked_softmax(scores, mask, sink):
    s = jnp.where(mask, scores, jnp.float32(NEG_INF))
    m = jnp.max(s, axis=-1, keepdims=True)
    if sink is not None:
        m = jnp.maximum(m, sink)
    p = jnp.exp(s - m)
    den = jnp.sum(p, axis=-1, keepdims=True)
    if sink is not None:
        den = den + jnp.exp(sink - m)
    return p / den, (m + jnp.log(den))[..., 0]


def banded_window_attention(q, k, v, window, sink):
    Bn, L, Hkv, R, Dh = q.shape
    pad = (-L) % BLOCK
    Lp = L + pad
    nb = Lp // BLOCK

    def padseq(a):
        return jnp.pad(a, [(0, 0), (0, pad)] + [(0, 0)] * (a.ndim - 2))

    qb = padseq(q).reshape(Bn, nb, BLOCK, Hkv, R, Dh)
    kb = padseq(k).reshape(Bn, nb, BLOCK, Hkv, Dh)
    vb = padseq(v).reshape(Bn, nb, BLOCK, Hkv, Dh)
    kk = jnp.concatenate([jnp.concatenate([jnp.zeros_like(kb[:, :1]), kb[:, :-1]], axis=1), kb], axis=2)
    vv = jnp.concatenate([jnp.concatenate([jnp.zeros_like(vb[:, :1]), vb[:, :-1]], axis=1), vb], axis=2)
    scores = jnp.einsum('bnihrd,bnjhd->bnhrij', qb, kk, preferred_element_type=jnp.float32) * (Dh ** -0.5)
    i = jnp.arange(BLOCK)[:, None]
    j = jnp.arange(2 * BLOCK)[None, :]
    dist = i - j + BLOCK
    key_idx = jnp.arange(nb)[:, None, None] * BLOCK + j[None] - BLOCK
    mask = (dist >= 0)[None] & (dist <= window)[None] & (key_idx >= 0)
    mask = mask[None, :, None, None]
    sink_b = None if sink is None else sink.astype(jnp.float32)[None, None, :, :, None, None]
    p, lse = masked_softmax(scores, mask, sink_b)
    out = jnp.einsum('bnhrij,bnjhd->bnihrd', p.astype(v.dtype), vv).reshape(Bn, Lp, Hkv, R, Dh)[:, :L]
    lse = jnp.moveaxis(lse, -1, 2).reshape(Bn, Lp, Hkv, R)[:, :L]
    return out, lse


def gathered_window_attention(q, k_all, v_all, n_past, window, dilation, sink):
    S, Dh = q.shape[1], q.shape[-1]
    n_keys = window // dilation + 1
    idx = n_past + jnp.arange(S)[:, None] - dilation * jnp.arange(n_keys)[None, :]
    valid = idx >= 0
    idx = jnp.maximum(idx, 0)
    kg = k_all[:, idx]
    vg = v_all[:, idx]
    scores = jnp.einsum('bihrd,bikhd->bhrik', q, kg, preferred_element_type=jnp.float32) * (Dh ** -0.5)
    sink_b = None if sink is None else sink.astype(jnp.float32)[None, :, :, None, None]
    p, lse = masked_softmax(scores, valid[None, None, None], sink_b)
    out = jnp.einsum('bhrik,bikhd->bihrd', p.astype(v_all.dtype), vg)
    return out, jnp.moveaxis(lse, -1, 1)


def dilated_prompt_attention(q, k, v, window, dilation):
    Bn, S, H, Dh = q.shape
    L = S // dilation

    def split(a):
        return a.reshape(Bn, L, dilation, H, Dh).transpose(0, 2, 1, 3, 4).reshape(Bn * dilation, L, H, Dh)

    out, lse = banded_window_attention(split(q)[:, :, :, None], split(k), split(v), window // dilation, None)
    out = out[:, :, :, 0].reshape(Bn, dilation, L, H, Dh).transpose(0, 2, 1, 3, 4).reshape(Bn, S, H, Dh)
    lse = lse[..., 0].reshape(Bn, dilation, L, H).transpose(0, 2, 1, 3).reshape(Bn, S, H)
    return out, lse


def chunk_spatial_mix(v, w_s, b_s):
    Bn, S, _ = v.shape
    pad = (-S) % CHUNK
    vp = jnp.pad(v, ((0, 0), (0, pad), (0, 0))).reshape(Bn, (S + pad) // CHUNK, CHUNK, N_GROUPS, HEAD_DIM)
    w_causal = jnp.where(jnp.tril(jnp.ones((CHUNK, CHUNK), dtype=bool))[None], w_s, 0).astype(v.dtype)
    mixed = jnp.einsum('gts,bcsgd->bctgd', w_causal, vp) + b_s.T.astype(v.dtype)[None, None, :, :, None]
    return mixed.reshape(Bn, S + pad, MIX_WIDTH)[:, :S]


def mixer_layer(x, pos, past, norm_g, w_in, conv_w, sinks, v_ln_g, v_ln_b, w_spatial, b_spatial,
                w_branch, w_merge, w_out):
    Bn, S, _ = x.shape
    h = rms_norm(x, norm_g)
    (a_b, a_c, a_h, a_gate, s_q, s_k, s_v, s_gate,
     c_u, c_v, c_gate, d_q, d_k, d_v, d_gate) = jnp.split(h @ w_in, SPLIT_POINTS, axis=-1)

    z = a_c * a_h
    prev = jnp.zeros((Bn, CONV_WIDTH - 1, MIX_WIDTH), z.dtype) if past is None else past[0]
    zp = jnp.concatenate([prev, z], axis=1)
    conv = sum(conv_w[t] * zp[:, t:t + S] for t in range(CONV_WIDTH))
    y_a = a_b * conv
    new_conv = zp[:, -(CONV_WIDTH - 1):]

    q = rope(s_q.reshape(Bn, S, SWA_Q_HEADS, HEAD_DIM), pos).reshape(Bn, S, SWA_KV_HEADS, SWA_REP, HEAD_DIM)
    k = rope(s_k.reshape(Bn, S, SWA_KV_HEADS, HEAD_DIM), pos)
    v = s_v.reshape(Bn, S, SWA_KV_HEADS, HEAD_DIM)
    sink = sinks.reshape(SWA_KV_HEADS, SWA_REP)
    if past is None:
        o_b, _ = banded_window_attention(q, k, v, SWA_WINDOW, sink)
        k_all, v_all, keep = k, v, min(SWA_WINDOW, S)
    else:
        n_past = past[1].shape[1]
        k_all = jnp.concatenate([past[1][:, :, 0], k], axis=1)
        v_all = jnp.concatenate([past[1][:, :, 1], v], axis=1)
        o_b, _ = gathered_window_attention(q, k_all, v_all, n_past, SWA_WINDOW, 1, sink)
        keep = n_past
    y_b = o_b.reshape(Bn, S, MIX_WIDTH)
    new_swa = jnp.stack([k_all[:, -keep:], v_all[:, -keep:]], axis=2)

    vn = layer_norm(c_v, v_ln_g, v_ln_b)
    y_c = c_u * chunk_spatial_mix(vn, w_spatial, b_spatial)

    dq = rope(d_q.reshape(Bn, S, N_GROUPS, HEAD_DIM), pos)
    dk = rope(d_k.reshape(Bn, S, N_GROUPS, HEAD_DIM), pos)
    dv = d_v.reshape(Bn, S, N_GROUPS, HEAD_DIM)
    outs, lses, new_dil = [], [], []
    for g, (win, dil) in enumerate(DIL_GROUPS):
        qg = dq[:, :, g * DIL_HEADS:(g + 1) * DIL_HEADS]
        kg = dk[:, :, g * DIL_HEADS:(g + 1) * DIL_HEADS]
        vg = dv[:, :, g * DIL_HEADS:(g + 1) * DIL_HEADS]
        if past is None:
            o, lse = dilated_prompt_attention(qg, kg, vg, win, dil)
            k_all, v_all, keep = kg, vg, min(win, S)
        else:
            buf = past[2][g]
            n_past = buf.shape[1]
            k_all = jnp.concatenate([buf[:, :, 0], kg], axis=1)
            v_all = jnp.concatenate([buf[:, :, 1], vg], axis=1)
            o, lse = gathered_window_attention(qg[:, :, :, None], k_all, v_all, n_past, win, dil, None)
            o, lse = o[:, :, :, 0], lse[..., 0]
            keep = n_past
        outs.append(o)
        lses.append(lse)
        new_dil.append(jnp.stack([k_all[:, -keep:], v_all[:, -keep:]], axis=2))
    alpha = jax.nn.softmax(jnp.stack(lses, axis=0), axis=0)
    y_d = jnp.concatenate([o * alpha[g][..., None].astype(o.dtype) for g, o in enumerate(outs)],
                          axis=2).reshape(Bn, S, MIX_WIDTH)

    branches = jnp.stack([y_a * silu(a_gate), y_b * silu(s_gate), y_c * silu(c_gate), y_d * silu(d_gate)], axis=2)
    proj_b = jnp.einsum('bsne,ned->bsnd', branches, w_branch)
    gates = jax.nn.sigmoid(h @ w_merge).reshape(Bn, S, N_BRANCH, D_MODEL)
    merged = jnp.sum(gates * proj_b, axis=2)
    return x + merged @ w_out, new_conv, new_swa, new_dil, vn


def setup_inputs(seed: int = 0) -> dict:
    key = jax.random.key(seed)
    ks = jax.random.split(key, 19)

    def nrm(k, shape, scale):
        return scale * jax.random.normal(k, shape, jnp.float32)

    n_swa = min(SWA_WINDOW, PAST_LEN)
    dil_rows = [min(w, PAST_LEN) for w, _ in DIL_GROUPS]
    return {
        "x_prompt": nrm(ks[0], (BATCH, SEQ, D_MODEL), 1.0),
        "x_sample": nrm(ks[1], (DEC_BATCH, DEC_SEQ, D_MODEL), 1.0),
        "state_conv": nrm(ks[2], (DEPTH, DEC_BATCH, CONV_WIDTH - 1, MIX_WIDTH), 0.5),
        "cache_swa_kv": nrm(ks[3], (DEPTH, DEC_BATCH, n_swa, 2, SWA_KV_HEADS, HEAD_DIM), 1.0),
        "cache_dil1_kv": nrm(ks[4], (DEPTH, DEC_BATCH, dil_rows[0], 2, DIL_HEADS, HEAD_DIM), 1.0),
        "cache_dil4_kv": nrm(ks[5], (DEPTH, DEC_BATCH, dil_rows[1], 2, DIL_HEADS, HEAD_DIM), 1.0),
        "cache_dil16_kv": nrm(ks[6], (DEPTH, DEC_BATCH, dil_rows[2], 2, DIL_HEADS, HEAD_DIM), 1.0),
        "norm_g": 1.0 + nrm(ks[7], (DEPTH, D_MODEL), 0.05),
        "w_in": nrm(ks[8], (DEPTH, D_MODEL, IN_COLS), D_MODEL ** -0.5),
        "conv_w": nrm(ks[9], (DEPTH, CONV_WIDTH, MIX_WIDTH), CONV_WIDTH ** -0.5),
        "attn_sinks": nrm(ks[10], (DEPTH, SWA_Q_HEADS), 1.0),
        "v_ln_g": 1.0 + nrm(ks[11], (DEPTH, MIX_WIDTH), 0.05),
        "v_ln_b": nrm(ks[12], (DEPTH, MIX_WIDTH), 0.02),
        "w_spatial": nrm(ks[13], (DEPTH, N_GROUPS, CHUNK, CHUNK), 0.5 * CHUNK ** -0.5),
        "b_spatial": 1.0 + nrm(ks[14], (DEPTH, N_GROUPS, CHUNK), 0.1),
        "w_branch": nrm(ks[15], (DEPTH, N_BRANCH, MIX_WIDTH, D_MODEL), MIX_WIDTH ** -0.5),
        "w_merge": nrm(ks[16], (DEPTH, D_MODEL, N_BRANCH * D_MODEL), D_MODEL ** -0.5),
        "w_out": nrm(ks[17], (DEPTH, D_MODEL, D_MODEL), D_MODEL ** -0.5),
        "final_norm_g": 1.0 + nrm(ks[18], (D_MODEL,), 0.05),
    }


def reference(x_prompt, x_sample, state_conv, cache_swa_kv, cache_dil1_kv, cache_dil4_kv, cache_dil16_kv,
              norm_g, w_in, conv_w, attn_sinks, v_ln_g, v_ln_b, w_spatial, b_spatial,
              w_branch, w_merge, w_out, final_norm_g):
    pos_prompt = jnp.arange(SEQ, dtype=jnp.int32)
    pos_sample = PAST_LEN + jnp.arange(DEC_SEQ, dtype=jnp.int32)
    xp, xs = x_prompt, x_sample
    conv_p, conv_s, swa_p, swa_s, chunk_v_s = [], [], [], [], []
    dil_p = [[] for _ in DIL_GROUPS]
    dil_s = [[] for _ in DIL_GROUPS]
    for l in range(DEPTH):
        lw = (norm_g[l], w_in[l], conv_w[l], attn_sinks[l], v_ln_g[l], v_ln_b[l],
              w_spatial[l], b_spatial[l], w_branch[l], w_merge[l], w_out[l])
        xp, c_p, s_p, d_p, _ = mixer_layer(xp, pos_prompt, None, *lw)
        past = (state_conv[l], cache_swa_kv[l], (cache_dil1_kv[l], cache_dil4_kv[l], cache_dil16_kv[l]))
        xs, c_s, s_s, d_s, v_s = mixer_layer(xs, pos_sample, past, *lw)
        conv_p.append(c_p)
        conv_s.append(c_s)
        swa_p.append(s_p)
        swa_s.append(s_s)
        chunk_v_s.append(v_s)
        for g in range(len(DIL_GROUPS)):
            dil_p[g].append(d_p[g])
            dil_s[g].append(d_s[g])
    y_prompt = rms_norm(xp, final_norm_g)
    y_sample = rms_norm(xs, final_norm_g)
    return (y_prompt, y_sample,
            jnp.stack(conv_p), jnp.stack(conv_s),
            jnp.stack(swa_p), jnp.stack(swa_s),
            jnp.stack(dil_p[0]), jnp.stack(dil_s[0]),
            jnp.stack(dil_p[1]), jnp.stack(dil_s[1]),
            jnp.stack(dil_p[2]), jnp.stack(dil_s[2]),
            jnp.stack(chunk_v_s))
```

```python
import functools

import jax
import jax.numpy as jnp
from jax import lax
from jax.experimental import pallas as pl
from jax.experimental.pallas import tpu as pltpu

F32 = jnp.float32
BF16 = jnp.bfloat16

D_MODEL = 1024
HEAD_DIM = 64
HALF_HEAD = HEAD_DIM // 2
MIX = 384
BLOCK = 128
SWA_Q_HEADS = 6
SWA_KV_HEADS = 2
SWA_REP = SWA_Q_HEADS // SWA_KV_HEADS
DILATIONS = (1, 4, 16)
SUPER = BLOCK * DILATIONS[-1]
ROPE_THETA = 10000.0
EPS = 1e-6
NEG_INF = -1e30
SCALE = HEAD_DIM ** -0.5

COLS_A = 4 * MIX
COLS_B = MIX + 2 * BLOCK + MIX
COLS_C = 3 * MIX
COLS_D = 4 * MIX
OFF_A, OFF_B, OFF_C, OFF_D = 0, COLS_A, COLS_A + COLS_B, COLS_A + COLS_B + COLS_C
IN_COLS = OFF_D + COLS_D
KV_CH = 4 * HEAD_DIM

V7X_VMEM_BYTES = 64 * 1024 * 1024
MATMUL_COL_CHUNK = 512


def _vmem_limit(estimate_bytes):
    return int(min(V7X_VMEM_BYTES - 8 * 1024 * 1024, estimate_bytes + 16 * 1024 * 1024))


def _const_spec(shape):
    nd = len(shape)
    return pl.BlockSpec(shape, lambda *_: (0,) * nd)


def _rms(x, g):
    ms = jnp.mean(x * x, axis=-1, keepdims=True)
    return x * lax.rsqrt(ms + EPS) * g


def _sigmoid(x):
    return 1.0 / (1.0 + jnp.exp(-x))


def _silu(x):
    return x * _sigmoid(x)


def _rope(x, cos, sin):
    w = x.shape[-1]
    lane = lax.broadcasted_iota(jnp.int32, x.shape, 1)
    fwd = pltpu.roll(x, w - HALF_HEAD, 1)
    bwd = pltpu.roll(x, HALF_HEAD, 1)
    partner = jnp.where((lane & HALF_HEAD) == 0, fwd, bwd)
    return x * cos + partner * sin


def _tile_lanes(t, reps):
    return t if reps == 1 else jnp.concatenate([t] * reps, axis=1)


def _dot(a, b):
    return jnp.dot(a, b, preferred_element_type=F32)


def _dot_nt(a, b):
    return lax.dot_general(a, b, (((1,), (1,)), ((), ())), preferred_element_type=F32)


def _inproj_body(x_ref, g_ref, w_ref, oa_ref, ob_ref, oc_ref, od_ref):
    h = _rms(x_ref[...], g_ref[...]).astype(BF16)
    for o_ref, off in ((oa_ref, OFF_A), (ob_ref, OFF_B), (oc_ref, OFF_C), (od_ref, OFF_D)):
        width = o_ref.shape[-1]
        for j in range(0, width, MATMUL_COL_CHUNK):
            cw = min(MATMUL_COL_CHUNK, width - j)
            o_ref[:, j:j + cw] = _dot(h, w_ref[:, off + j:off + j + cw]).astype(o_ref.dtype)


def _inproj(x, g, w_bf16, tm, out_dtype):
    t = x.shape[0]
    widths = (COLS_A, COLS_B, COLS_C, COLS_D)
    osize = jnp.dtype(out_dtype).itemsize
    est = 2 * tm * D_MODEL * 4 + 2 * D_MODEL * IN_COLS * 2 + 2 * tm * IN_COLS * osize
    return pl.pallas_call(
        _inproj_body,
        grid=(t // tm,),
        in_specs=[pl.BlockSpec((tm, D_MODEL), lambda i: (i, 0)),
                  _const_spec((1, D_MODEL)),
                  _const_spec((D_MODEL, IN_COLS))],
        out_specs=[pl.BlockSpec((tm, wd), lambda i: (i, 0)) for wd in widths],
        out_shape=[jax.ShapeDtypeStruct((t, wd), out_dtype) for wd in widths],
        compiler_params=pltpu.CompilerParams(dimension_semantics=("parallel",),
                                             vmem_limit_bytes=_vmem_limit(est)),
        name="inproj",
    )(x, g, w_bf16)


def _outproj_body(x_ref, ya_ref, yb_ref, yc_ref, yd_ref, g_ref, wm_ref, wb_ref, wo_ref, fg_ref,
                  o_ref, *, final):
    x = x_ref[...]
    h = _rms(x, g_ref[...]).astype(BF16)
    merged = None
    for n, y_ref in enumerate((ya_ref, yb_ref, yc_ref, yd_ref)):
        gate = _sigmoid(_dot(h, wm_ref[:, n * D_MODEL:(n + 1) * D_MODEL]))
        term = gate * _dot(y_ref[...].astype(BF16), wb_ref[n])
        merged = term if merged is None else merged + term
    out = x + _dot(merged.astype(BF16), wo_ref[...])
    if final:
        out = _rms(out, fg_ref[...])
    o_ref[...] = out


def _outproj(x, ys, g, wm, wb, wo, fg, tm, final):
    t = x.shape[0]
    ysize = jnp.dtype(ys[0].dtype).itemsize
    est = (4 * tm * D_MODEL * 4 + 2 * 4 * tm * MIX * ysize
           + 2 * (4 * D_MODEL * D_MODEL + 4 * MIX * D_MODEL + D_MODEL * D_MODEL) * 2
           + 4 * tm * D_MODEL * 4)
    row = lambda i: (i, 0)
    return pl.pallas_call(
        functools.partial(_outproj_body, final=final),
        grid=(t // tm,),
        in_specs=[pl.BlockSpec((tm, D_MODEL), row)]
                 + [pl.BlockSpec((tm, MIX), row)] * 4
                 + [_const_spec((1, D_MODEL)),
                    _const_spec((D_MODEL, 4 * D_MODEL)),
                    _const_spec((4, MIX, D_MODEL)),
                    _const_spec((D_MODEL, D_MODEL)),
                    _const_spec((1, D_MODEL))],
        out_specs=pl.BlockSpec((tm, D_MODEL), row),
        out_shape=jax.ShapeDtypeStruct((t, D_MODEL), F32),
        compiler_params=pltpu.CompilerParams(dimension_semantics=("parallel",),
                                             vmem_limit_bytes=_vmem_limit(est)),
        name="outproj",
    )(x, *ys, g, wm, wb, wo, fg)


def _mix_ac_body(pa_ref, pc_ref, cw_ref, lg_ref, lb_ref, ws_ref, bs_ref,
                 ya_ref, yc_ref, nc_ref, tail_ref):
    ts = pa_ref.shape[0]

    @pl.when(pl.program_id(1) == 0)
    def _():
        tail_ref[...] = jnp.zeros_like(tail_ref)

    a_b = pa_ref[:, 0:MIX].astype(F32)
    z = pa_ref[:, MIX:2 * MIX].astype(F32) * pa_ref[:, 2 * MIX:3 * MIX].astype(F32)
    a_gate = pa_ref[:, 3 * MIX:4 * MIX].astype(F32)
    row = lax.broadcasted_iota(jnp.int32, z.shape, 0)
    prev1 = tail_ref[1:2, :]
    prev2 = tail_ref[0:1, :]
    z1 = jnp.where(row == 0, prev1, pltpu.roll(z, 1, 0))
    z2 = jnp.where(row == 0, prev2, jnp.where(row == 1, prev1, pltpu.roll(z, 2, 0)))
    conv = cw_ref[0:1, :] * z2 + cw_ref[1:2, :] * z1 + cw_ref[2:3, :] * z
    ya_ref[...] = (a_b * conv * _silu(a_gate)).astype(ya_ref.dtype)
    last2 = z[ts - 2:ts, :]
    tail_ref[...] = last2
    nc_ref[...] = last2

    c_v = pc_ref[:, MIX:2 * MIX].astype(F32)
    mu = jnp.mean(c_v, axis=-1, keepdims=True)
    cen = c_v - mu
    var = jnp.mean(cen * cen, axis=-1, keepdims=True)
    vn = (cen * lax.rsqrt(var + EPS) * lg_ref[...] + lb_ref[...]).astype(BF16)
    r = lax.broadcasted_iota(jnp.int32, (BLOCK, BLOCK), 0)
    c = lax.broadcasted_iota(jnp.int32, (BLOCK, BLOCK), 1)
    n_groups = MIX // HEAD_DIM
    w_causal = [jnp.where(r >= c, ws_ref[g], 0.0).astype(BF16) for g in range(n_groups)]
    for ck in range(ts // BLOCK):
        rows = slice(ck * BLOCK, (ck + 1) * BLOCK)
        vn_c = vn[rows, :]
        mixed = jnp.concatenate(
            [_dot(w_causal[g], vn_c[:, g * HEAD_DIM:(g + 1) * HEAD_DIM]) for g in range(n_groups)],
            axis=1) + bs_ref[...]
        c_u = pc_ref[rows, 0:MIX].astype(F32)
        c_gate = pc_ref[rows, 2 * MIX:3 * MIX].astype(F32)
        yc_ref[rows, :] = (c_u * mixed * _silu(c_gate)).astype(yc_ref.dtype)


def _mix_ac(pa, pc, conv_w, ln_g, ln_b, w_s, b_tile, ts):
    b, s, _ = pa.shape
    est = 2 * ts * (COLS_A + COLS_C) * 2 + 4 * ts * MIX * 2 + 2 * 6 * BLOCK * BLOCK * 4 + 6 * ts * MIX * 4
    tile = lambda bi, si: (bi, si, 0)
    return pl.pallas_call(
        _mix_ac_body,
        grid=(b, s // ts),
        in_specs=[pl.BlockSpec((None, ts, COLS_A), tile),
                  pl.BlockSpec((None, ts, COLS_C), tile),
                  _const_spec((3, MIX)), _const_spec((1, MIX)), _const_spec((1, MIX)),
                  _const_spec((MIX // HEAD_DIM, BLOCK, BLOCK)), _const_spec((BLOCK, MIX))],
        out_specs=[pl.BlockSpec((None, ts, MIX), tile),
                   pl.BlockSpec((None, ts, MIX), tile),
                   pl.BlockSpec((None, 2, MIX), lambda bi, si: (bi, 0, 0))],
        out_shape=[jax.ShapeDtypeStruct((b, s, MIX), BF16),
                   jax.ShapeDtypeStruct((b, s, MIX), BF16),
                   jax.ShapeDtypeStruct((b, 2, MIX), F32)],
        scratch_shapes=[pltpu.VMEM((2, MIX), F32)],
        compiler_params=pltpu.CompilerParams(dimension_semantics=("arbitrary", "arbitrary"),
                                             vmem_limit_bytes=_vmem_limit(est)),
        name="mix_ac",
    )(pa, pc, conv_w, ln_g, ln_b, w_s, b_tile)


def _band_mask(first_block):
    i = lax.broadcasted_iota(jnp.int32, (BLOCK, 2 * BLOCK), 0)
    j = lax.broadcasted_iota(jnp.int32, (BLOCK, 2 * BLOCK), 1)
    band = (j >= i) & (j <= i + BLOCK)
    return band & jnp.logical_or(jnp.logical_not(first_block), j >= BLOCK)


def _mix_b_body(sink_ref, pb_ref, cos_ref, sin_ref, yb_ref, kv_ref, kprev_ref, vprev_ref):
    blk = pl.program_id(1)

    @pl.when(blk == 0)
    def _():
        kprev_ref[...] = jnp.zeros_like(kprev_ref)
        vprev_ref[...] = jnp.zeros_like(vprev_ref)

    cos = cos_ref[...]
    sin = sin_ref[...]
    q = _rope(pb_ref[:, 0:MIX].astype(F32), _tile_lanes(cos, 3), _tile_lanes(sin, 3)) * SCALE
    k = _rope(pb_ref[:, MIX:MIX + BLOCK].astype(F32), cos, sin)
    v = pb_ref[:, MIX + BLOCK:MIX + 2 * BLOCK]
    gate = pb_ref[:, MIX + 2 * BLOCK:COLS_B].astype(F32)
    kv_ref[:, 0:BLOCK] = k
    kv_ref[:, BLOCK:2 * BLOCK] = v.astype(F32)

    k_bf = k.astype(BF16)
    v_bf = v.astype(BF16)
    kk = jnp.concatenate([kprev_ref[...], k_bf], axis=0)
    vv = jnp.concatenate([vprev_ref[...], v_bf], axis=0)
    mask = _band_mask(blk == 0)
    mask3 = jnp.concatenate([mask] * SWA_REP, axis=0)
    outs = [None] * SWA_Q_HEADS
    for g in range(SWA_KV_HEADS):
        heads = range(g * SWA_REP, (g + 1) * SWA_REP)
        qg = jnp.concatenate([q[:, h * HEAD_DIM:(h + 1) * HEAD_DIM] for h in heads], axis=0).astype(BF16)
        sink = jnp.concatenate([jnp.full((BLOCK, 1), sink_ref[h], F32) for h in heads], axis=0)
        s = jnp.where(mask3, _dot_nt(qg, kk[:, g * HEAD_DIM:(g + 1) * HEAD_DIM]), NEG_INF)
        m = jnp.maximum(jnp.max(s, axis=-1, keepdims=True), sink)
        p = jnp.exp(s - m)
        den = jnp.sum(p, axis=-1, keepdims=True) + jnp.exp(sink - m)
        o = _dot(p.astype(BF16), vv[:, g * HEAD_DIM:(g + 1) * HEAD_DIM]) / den
        for r, h in enumerate(heads):
            outs[h] = o[r * BLOCK:(r + 1) * BLOCK, :]
    y = jnp.concatenate(outs, axis=1)
    yb_ref[...] = (y * _silu(gate)).astype(yb_ref.dtype)
    kprev_ref[...] = k_bf
    vprev_ref[...] = v_bf


def _mix_b(pb, sinks, cos, sin):
    b, s, _ = pb.shape
    est = 2 * BLOCK * COLS_B * 2 + 16 * BLOCK * BLOCK * 4 + 8 * MIX * 2 * BLOCK * 4
    tile = lambda bi, si: (bi, si, 0)
    return pl.pallas_call(
        _mix_b_body,
        grid=(b, s // BLOCK),
        in_specs=[pl.BlockSpec(memory_space=pltpu.SMEM),
                  pl.BlockSpec((None, BLOCK, COLS_B), tile),
                  pl.BlockSpec((BLOCK, BLOCK), lambda bi, si: (si, 0)),
                  pl.BlockSpec((BLOCK, BLOCK), lambda bi, si: (si, 0))],
        out_specs=[pl.BlockSpec((None, BLOCK, MIX), tile),
                   pl.BlockSpec((None, BLOCK, 2 * BLOCK), lambda bi, si: (bi, 0, 0))],
        out_shape=[jax.ShapeDtypeStruct((b, s, MIX), BF16),
                   jax.ShapeDtypeStruct((b, BLOCK, 2 * BLOCK), F32)],
        scratch_shapes=[pltpu.VMEM((BLOCK, BLOCK), BF16), pltpu.VMEM((BLOCK, BLOCK), BF16)],
        compiler_params=pltpu.CompilerParams(dimension_semantics=("arbitrary", "arbitrary"),
                                             vmem_limit_bytes=_vmem_limit(est)),
        name="mix_b",
    )(sinks, pb, cos, sin)


def _mix_d_body(pd_ref, cos_ref, sin_ref, yd_ref, c1_ref, c4_ref, c16_ref,
                q_ref, k_ref, v_ref, o_ref, l_ref):
    sb = pl.program_id(1)

    lw = 2 * HEAD_DIM
    n_groups = len(DILATIONS)

    @pl.when(sb == 0)
    def _():
        k_ref[:, 0:SUPER, :] = jnp.zeros((n_groups, SUPER, lw), F32)
        v_ref[:, 0:SUPER, :] = jnp.zeros((n_groups, SUPER, lw), F32)

    cos = cos_ref[...]
    sin = sin_ref[...]
    for g in range(n_groups):
        gc = slice(g * lw, (g + 1) * lw)
        q_ref[g] = _rope(pd_ref[:, gc].astype(F32), cos, sin) * SCALE
        k_ref[g, SUPER:2 * SUPER, :] = _rope(pd_ref[:, MIX + g * lw:MIX + (g + 1) * lw].astype(F32), cos, sin)
        v_ref[g, SUPER:2 * SUPER, :] = pd_ref[:, 2 * MIX + g * lw:2 * MIX + (g + 1) * lw].astype(F32)

    n_blocks = SUPER // BLOCK
    for g, dil in enumerate(DILATIONS):

        def block(t, carry, dil=dil, g=g):
            r = t % dil
            j = t // dil
            base = j * (BLOCK * dil) + r
            qs = q_ref[g, pl.ds(base, BLOCK, stride=dil), :]
            kstart = SUPER + base - BLOCK * dil
            ks = k_ref[g, pl.ds(kstart, 2 * BLOCK, stride=dil), :]
            vs = v_ref[g, pl.ds(kstart, 2 * BLOCK, stride=dil), :]
            mask = _band_mask(jnp.logical_and(sb == 0, j == 0))
            o_parts, l_parts = [], []
            for hh in range(2):
                hc = slice(hh * HEAD_DIM, (hh + 1) * HEAD_DIM)
                s = jnp.where(mask, _dot_nt(qs[:, hc].astype(BF16), ks[:, hc].astype(BF16)), NEG_INF)
                m = jnp.max(s, axis=-1, keepdims=True)
                p = jnp.exp(s - m)
                den = jnp.sum(p, axis=-1, keepdims=True)
                o_parts.append(_dot(p.astype(BF16), vs[:, hc].astype(BF16)) / den)
                l_parts.append(jnp.broadcast_to(m + jnp.log(den), (BLOCK, HEAD_DIM)))
            o_ref[g, pl.ds(base, BLOCK, stride=dil), :] = jnp.concatenate(o_parts, axis=1)
            l_ref[g, pl.ds(base, BLOCK, stride=dil), :] = jnp.concatenate(l_parts, axis=1)
            return carry

        lax.fori_loop(0, n_blocks, block, 0)

    ls = [l_ref[g] for g in range(n_groups)]
    lmax = jnp.maximum(jnp.maximum(ls[0], ls[1]), ls[2])
    es = [jnp.exp(l - lmax) for l in ls]
    esum = es[0] + es[1] + es[2]
    for g in range(n_groups):
        gate = pd_ref[:, 3 * MIX + g * lw:3 * MIX + (g + 1) * lw].astype(F32)
        y = o_ref[g] * (es[g] / esum)
        yd_ref[:, g * lw:(g + 1) * lw] = (y * _silu(gate)).astype(yd_ref.dtype)

    for g, (c_ref, dil) in enumerate(zip((c1_ref, c4_ref, c16_ref), DILATIONS)):
        n = BLOCK * dil
        c_ref[0:lw, :] = k_ref[g, 2 * SUPER - n:2 * SUPER, :].T
        c_ref[lw:2 * lw, :] = v_ref[g, 2 * SUPER - n:2 * SUPER, :].T

    k_ref[:, 0:SUPER, :] = k_ref[:, SUPER:2 * SUPER, :]
    v_ref[:, 0:SUPER, :] = v_ref[:, SUPER:2 * SUPER, :]


def _mix_d(pd, cos, sin):
    b, s, _ = pd.shape
    est = (2 * SUPER * COLS_D * 2 + 4 * SUPER * BLOCK * 4 + 2 * SUPER * MIX * 2
           + 2 * KV_CH * (BLOCK + 4 * BLOCK + SUPER) * 4 + (3 + 2 * 2) * SUPER * MIX * 4)
    tile = lambda bi, si: (bi, si, 0)
    cache = lambda bi, si: (bi, 0, 0)
    return pl.pallas_call(
        _mix_d_body,
        grid=(b, s // SUPER),
        in_specs=[pl.BlockSpec((None, SUPER, COLS_D), tile),
                  pl.BlockSpec((SUPER, BLOCK), lambda bi, si: (si, 0)),
                  pl.BlockSpec((SUPER, BLOCK), lambda bi, si: (si, 0))],
        out_specs=[pl.BlockSpec((None, SUPER, MIX), tile)]
                  + [pl.BlockSpec((None, KV_CH, BLOCK * d), cache) for d in DILATIONS],
        out_shape=[jax.ShapeDtypeStruct((b, s, MIX), BF16)]
                  + [jax.ShapeDtypeStruct((b, KV_CH, BLOCK * d), F32) for d in DILATIONS],
        scratch_shapes=[pltpu.VMEM((3, SUPER, 2 * HEAD_DIM), F32),
                        pltpu.VMEM((3, 2 * SUPER, 2 * HEAD_DIM), F32),
                        pltpu.VMEM((3, 2 * SUPER, 2 * HEAD_DIM), F32),
                        pltpu.VMEM((3, SUPER, 2 * HEAD_DIM), F32),
                        pltpu.VMEM((3, SUPER, 2 * HEAD_DIM), F32)],
        compiler_params=pltpu.CompilerParams(dimension_semantics=("arbitrary", "arbitrary"),
                                             vmem_limit_bytes=_vmem_limit(est)),
        name="mix_d",
    )(pd, cos, sin)


def _decode_head(q_col, k_new, v_new, k_t, v_t, dil, sink):
    n = k_t.shape[1]
    s = jnp.sum(k_t * q_col, axis=0, keepdims=True)
    if dil > 1:
        lane = lax.broadcasted_iota(jnp.int32, (1, n), 1)
        s = jnp.where((lane & (dil - 1)) == 0, s, NEG_INF)
    s_new = jnp.sum(k_new * q_col, axis=0, keepdims=True)
    m = jnp.maximum(jnp.max(s, axis=1, keepdims=True), s_new)
    if sink is not None:
        m = jnp.maximum(m, sink)
    p = jnp.exp(s - m)
    p_new = jnp.exp(s_new - m)
    den = jnp.sum(p, axis=1, keepdims=True) + p_new
    if sink is not None:
        den = den + jnp.exp(sink - m)
    o = (jnp.sum(v_t * p, axis=1, keepdims=True) + v_new * p_new) / den
    return o, m + jnp.log(den)


def _shifted(cache, new_col):
    n = cache.shape[1]
    lane = lax.broadcasted_iota(jnp.int32, cache.shape, 1)
    return jnp.where(lane == n - 1, new_col, pltpu.roll(cache, n - 1, 1))


XT_QB, XT_KB, XT_VB = 0, MIX, MIX + BLOCK
XT_QD, XT_KD, XT_VD = MIX + 2 * BLOCK, 2 * MIX + 2 * BLOCK, 3 * MIX + 2 * BLOCK
XT_ROWS = 4 * MIX + 2 * BLOCK


def _decode_body(*refs, seq_per_step, n_alias):
    (sink_ref, pa_ref, pb_ref, pc_ref, pd_ref, st_ref, cw_ref, lg_ref, lb_ref,
     wd_ref, bd_ref, cos_ref, sin_ref, cs_ref, c1_ref, c4_ref, c16_ref) = refs[:17]
    (ya_ref, yb_ref, yc_ref, yd_ref, nst_ref, vn_ref, ns_ref, n1_ref, n4_ref, n16_ref,
     xt_ref, yt_ref) = refs[17 + n_alias:]
    step = pl.program_id(0)
    n_seq = pa_ref.shape[0]

    @pl.when(step == 0)
    def _():
        a_b = pa_ref[:, 0:MIX]
        z = pa_ref[:, MIX:2 * MIX] * pa_ref[:, 2 * MIX:3 * MIX]
        prev0 = st_ref[:, 0:MIX]
        prev1 = st_ref[:, MIX:2 * MIX]
        conv = cw_ref[0:1, :] * prev0 + cw_ref[1:2, :] * prev1 + cw_ref[2:3, :] * z
        ya_ref[...] = a_b * conv * _silu(pa_ref[:, 3 * MIX:4 * MIX])
        nst_ref[:, 0:MIX] = prev1
        nst_ref[:, MIX:2 * MIX] = z

        c_v = pc_ref[:, MIX:2 * MIX]
        mu = jnp.mean(c_v, axis=-1, keepdims=True)
        cen = c_v - mu
        var = jnp.mean(cen * cen, axis=-1, keepdims=True)
        vn = cen * lax.rsqrt(var + EPS) * lg_ref[...] + lb_ref[...]
        vn_ref[...] = vn
        mixed = wd_ref[...] * vn + bd_ref[...]
        yc_ref[...] = pc_ref[:, 0:MIX] * mixed * _silu(pc_ref[:, 2 * MIX:3 * MIX])

        cos = cos_ref[...]
        sin = sin_ref[...]
        cos3, sin3 = _tile_lanes(cos, 3), _tile_lanes(sin, 3)
        xt_ref[XT_QB:XT_KB, :] = (_rope(pb_ref[:, 0:MIX], cos3, sin3) * SCALE).T
        xt_ref[XT_KB:XT_VB, :] = _rope(pb_ref[:, MIX:MIX + BLOCK], cos, sin).T
        xt_ref[XT_VB:XT_QD, :] = pb_ref[:, MIX + BLOCK:MIX + 2 * BLOCK].T
        xt_ref[XT_QD:XT_KD, :] = (_rope(pd_ref[:, 0:MIX], cos3, sin3) * SCALE).T
        xt_ref[XT_KD:XT_VD, :] = _rope(pd_ref[:, MIX:2 * MIX], cos3, sin3).T
        xt_ref[XT_VD:XT_ROWS, :] = pd_ref[:, 2 * MIX:3 * MIX].T
        yt_ref[...] = jnp.zeros_like(yt_ref)

    hd = HEAD_DIM

    def one_sequence(i, carry):
        b = step * seq_per_step + i
        lane = lax.broadcasted_iota(jnp.int32, (XT_ROWS, n_seq), 1)
        x_col = jnp.sum(jnp.where(lane == b, xt_ref[...], 0.0), axis=1, keepdims=True)
        q_col = x_col[XT_QB:XT_KB]
        k_new = x_col[XT_KB:XT_VB]
        v_new = x_col[XT_VB:XT_QD]
        cache = cs_ref[i]
        outs = []
        for h in range(SWA_Q_HEADS):
            g = h // SWA_REP
            o, _ = _decode_head(q_col[h * hd:(h + 1) * hd], k_new[g * hd:(g + 1) * hd],
                                v_new[g * hd:(g + 1) * hd], cache[g * hd:(g + 1) * hd],
                                cache[2 * hd + g * hd:2 * hd + (g + 1) * hd], 1,
                                jnp.full((1, 1), sink_ref[h], F32))
            outs.append(o)
        yb_col = jnp.concatenate(outs, axis=0)
        ns_ref[i] = _shifted(cache, jnp.concatenate([k_new, v_new], axis=0))

        q_col = x_col[XT_QD:XT_KD]
        k_new = x_col[XT_KD:XT_VD]
        v_new = x_col[XT_VD:XT_ROWS]
        o_heads, l_heads = [], []
        for g, (c_ref, n_ref, dil) in enumerate(zip((c1_ref, c4_ref, c16_ref),
                                                    (n1_ref, n4_ref, n16_ref), DILATIONS)):
            cache = c_ref[i]
            for hh in range(2):
                ch = slice((2 * g + hh) * hd, (2 * g + hh + 1) * hd)
                o, lse = _decode_head(q_col[ch], k_new[ch], v_new[ch],
                                      cache[hh * hd:(hh + 1) * hd],
                                      cache[2 * hd + hh * hd:2 * hd + (hh + 1) * hd], dil, None)
                o_heads.append(o)
                l_heads.append(lse)
            gc = slice(2 * g * hd, 2 * (g + 1) * hd)
            n_ref[i] = _shifted(cache, jnp.concatenate([k_new[gc], v_new[gc]], axis=0))
        cols = []
        for hh in range(2):
            ls = [l_heads[2 * g + hh] for g in range(3)]
            lmax = jnp.maximum(jnp.maximum(ls[0], ls[1]), ls[2])
            es = [jnp.exp(l - lmax) for l in ls]
            esum = es[0] + es[1] + es[2]
            cols.append([o_heads[2 * g + hh] * (es[g] / esum) for g in range(3)])
        yd_col = jnp.concatenate([cols[hh][g] for g in range(3) for hh in range(2)], axis=0)
        y_col = jnp.concatenate([yb_col, yd_col], axis=0)
        lane_y = lax.broadcasted_iota(jnp.int32, (2 * MIX, n_seq), 1)
        yt_ref[...] = jnp.where(lane_y == b, y_col, yt_ref[...])
        return carry

    lax.fori_loop(0, seq_per_step, one_sequence, 0)

    @pl.when(step == pl.num_programs(0) - 1)
    def _():
        yb_ref[...] = yt_ref[0:MIX, :].T * _silu(pb_ref[:, MIX + 2 * BLOCK:COLS_B])
        yd_ref[...] = yt_ref[MIX:2 * MIX, :].T * _silu(pd_ref[:, 3 * MIX:4 * MIX])


def _decode(layer, sinks, pa, pb, pc, pd, state, conv_w, ln_g, ln_b, w_diag, b_diag, cos, sin,
            caches, prev_outs, seq_per_step):
    nb = pa.shape[0]
    rows = sum(c.shape[3] for c in caches)
    est = 4 * seq_per_step * KV_CH * rows * 4 + 4 * nb * (IN_COLS + 8 * MIX) * 4
    cache_spec = lambda c: pl.BlockSpec((None, seq_per_step) + c.shape[2:], lambda i: (layer, i, 0, 0))
    small = (pa, pb, pc, pd, state, conv_w, ln_g, ln_b, w_diag, b_diag, cos, sin)
    n_small_out = 6
    first_alias_in = 1 + len(small) + len(caches)
    return pl.pallas_call(
        functools.partial(_decode_body, seq_per_step=seq_per_step, n_alias=len(prev_outs)),
        grid=(nb // seq_per_step,),
        in_specs=[pl.BlockSpec(memory_space=pltpu.SMEM)]
                 + [_const_spec(a.shape) for a in small]
                 + [cache_spec(c) for c in caches]
                 + [pl.BlockSpec(memory_space=pl.ANY) for _ in prev_outs],
        out_specs=[_const_spec((nb, MIX))] * 4
                  + [_const_spec((nb, 2 * MIX)), _const_spec((nb, MIX))]
                  + [cache_spec(c) for c in caches],
        out_shape=[jax.ShapeDtypeStruct((nb, MIX), F32)] * 4
                  + [jax.ShapeDtypeStruct((nb, 2 * MIX), F32), jax.ShapeDtypeStruct((nb, MIX), F32)]
                  + [jax.ShapeDtypeStruct(c.shape, F32) for c in caches],
        scratch_shapes=[pltpu.VMEM((XT_ROWS, nb), F32), pltpu.VMEM((2 * MIX, nb), F32)],
        input_output_aliases={first_alias_in + i: n_small_out + i for i in range(len(prev_outs))},
        compiler_params=pltpu.CompilerParams(dimension_semantics=("arbitrary",),
                                             vmem_limit_bytes=_vmem_limit(est)),
        name="decode",
    )(sinks, *small, *caches, *prev_outs)


def _rope_tables(pos):
    inv_freq = ROPE_THETA ** (-jnp.arange(HALF_HEAD, dtype=F32) / HALF_HEAD)
    ang = pos.astype(F32)[:, None] * inv_freq[None, :]
    cos, sin = jnp.cos(ang), jnp.sin(ang)
    return (jnp.concatenate([cos, cos, cos, cos], axis=1),
            jnp.concatenate([-sin, sin, -sin, sin], axis=1))


def _cache_view(c):
    d, b, n = c.shape[:3]
    return jnp.transpose(c, (0, 1, 3, 4, 5, 2)).reshape(d, b, KV_CH, n)


def _cache_unview(c):
    lead, n = c.shape[:-2], c.shape[-1]
    nl = len(lead)
    perm = tuple(range(nl)) + (nl + 3, nl, nl + 1, nl + 2)
    return jnp.transpose(c.reshape(lead + (2, 2, HEAD_DIM, n)), perm)


def kernel(x_prompt, x_sample, state_conv, cache_swa_kv, cache_dil1_kv, cache_dil4_kv, cache_dil16_kv,
           norm_g, w_in, conv_w, attn_sinks, v_ln_g, v_ln_b, w_spatial, b_spatial,
           w_branch, w_merge, w_out, final_norm_g):
    depth = w_in.shape[0]
    nb, seq, _ = x_prompt.shape
    ns, dec_seq, _ = x_sample.shape
    assert dec_seq == 1 and seq % SUPER == 0
    past_len = 16384
    assert cache_swa_kv.shape[2] == BLOCK and cache_dil16_kv.shape[2] == SUPER

    cos_p, sin_p = _rope_tables(jnp.arange(seq, dtype=jnp.int32))
    cos_s, sin_s = _rope_tables(past_len + jnp.arange(1, dtype=jnp.int32))
    w_in_bf = w_in.astype(BF16)
    w_merge_bf = w_merge.astype(BF16)
    w_branch_bf = w_branch.astype(BF16)
    w_out_bf = w_out.astype(BF16)
    caches = [_cache_view(c) for c in (cache_swa_kv, cache_dil1_kv, cache_dil4_kv, cache_dil16_kv)]
    fg = final_norm_g.reshape(1, D_MODEL)

    xp = x_prompt.reshape(nb * seq, D_MODEL)
    xs = x_sample.reshape(ns, D_MODEL)
    conv_p, conv_s, chunk_v = [], [], []
    new_p = [[] for _ in range(4)]
    new_s = ()
    for l in range(depth):
        g = norm_g[l].reshape(1, D_MODEL)
        final = l == depth - 1
        ln_g = v_ln_g[l].reshape(1, MIX)
        ln_b = v_ln_b[l].reshape(1, MIX)

        pa, pb, pc, pd = _inproj(xp, g, w_in_bf[l], 512, BF16)
        shp = lambda a: a.reshape(nb, seq, a.shape[-1])
        b_tile = jnp.repeat(b_spatial[l].T, HEAD_DIM, axis=1)
        ya, yc, nconv = _mix_ac(shp(pa), shp(pc), conv_w[l], ln_g, ln_b, w_spatial[l], b_tile, 512)
        yb, kv_swa = _mix_b(shp(pb), attn_sinks[l], cos_p, sin_p)
        yd, kv1, kv4, kv16 = _mix_d(shp(pd), cos_p, sin_p)
        flat = lambda a: a.reshape(nb * seq, MIX)
        xp = _outproj(xp, (flat(ya), flat(yb), flat(yc), flat(yd)), g, w_merge_bf[l], w_branch_bf[l],
                      w_out_bf[l], fg, 512, final)
        conv_p.append(nconv)
        new_p[0].append(_cache_unview(jnp.transpose(kv_swa, (0, 2, 1))))
        for i, kv in enumerate((kv1, kv4, kv16)):
            new_p[i + 1].append(_cache_unview(kv))

        sa, sb, sc, sd = _inproj(xs, g, w_in_bf[l], ns, F32)
        w_diag = jnp.repeat(w_spatial[l][:, 0, 0], HEAD_DIM).reshape(1, MIX)
        b_diag = jnp.repeat(b_spatial[l][:, 0], HEAD_DIM).reshape(1, MIX)
        outs = _decode(l, attn_sinks[l], sa, sb, sc, sd, state_conv[l].reshape(ns, 2 * MIX), conv_w[l],
                       ln_g, ln_b, w_diag, b_diag, cos_s, sin_s, caches, new_s, 2)
        za, zb, zc, zd, nstate, vn = outs[:6]
        new_s = tuple(outs[6:])
        xs = _outproj(xs, (za, zb, zc, zd), g, w_merge_bf[l], w_branch_bf[l], w_out_bf[l], fg, ns, final)
        conv_s.append(nstate.reshape(ns, 2, MIX))
        chunk_v.append(vn.reshape(ns, 1, MIX))

    st = jnp.stack
    new_s = [_cache_unview(c) for c in new_s]
    return (xp.reshape(nb, seq, D_MODEL), xs.reshape(ns, 1, D_MODEL),
            st(conv_p), st(conv_s),
            st(new_p[0]), new_s[0], st(new_p[1]), new_s[1],
            st(new_p[2]), new_s[2], st(new_p[3]), new_s[3],
            st(chunk_v))
```

```python
import functools

import jax
import jax.numpy as jnp
from jax import lax
from jax.experimental import pallas as pl
from jax.experimental.pallas import tpu as pltpu

F32 = jnp.float32
BF16 = jnp.bfloat16

D_MODEL = 1024
HEAD_DIM = 64
HALF_HEAD = HEAD_DIM // 2
MIX = 384
BLOCK = 128
SWA_Q_HEADS = 6
SWA_KV_HEADS = 2
SWA_REP = SWA_Q_HEADS // SWA_KV_HEADS
DILATIONS = (1, 4, 16)
SUPER = BLOCK * DILATIONS[-1]
BLOCKS_PER_TRIP = 4
ROPE_THETA = 10000.0
EPS = 1e-6
NEG_INF = -1e30
SCALE = HEAD_DIM ** -0.5

COLS_A = 4 * MIX
COLS_B = MIX + 2 * BLOCK + MIX
COLS_C = 3 * MIX
COLS_D = 4 * MIX
OFF_A, OFF_B, OFF_C, OFF_D = 0, COLS_A, COLS_A + COLS_B, COLS_A + COLS_B + COLS_C
IN_COLS = OFF_D + COLS_D
KV_CH = 4 * HEAD_DIM

V7X_VMEM_BYTES = 64 * 1024 * 1024
MATMUL_COL_CHUNK = 512


def _vmem_limit(estimate_bytes):
    return int(min(V7X_VMEM_BYTES - 8 * 1024 * 1024, estimate_bytes + 16 * 1024 * 1024))


def _const_spec(shape):
    nd = len(shape)
    return pl.BlockSpec(shape, lambda *_: (0,) * nd)


def _rms(x, g):
    ms = jnp.mean(x * x, axis=-1, keepdims=True)
    return x * lax.rsqrt(ms + EPS) * g


def _sigmoid(x):
    return 1.0 / (1.0 + jnp.exp(-x))


def _silu(x):
    return x * _sigmoid(x)


def _rope(x, cos, sin):
    w = x.shape[-1]
    lane = lax.broadcasted_iota(jnp.int32, x.shape, 1)
    fwd = pltpu.roll(x, w - HALF_HEAD, 1)
    bwd = pltpu.roll(x, HALF_HEAD, 1)
    partner = jnp.where((lane & HALF_HEAD) == 0, fwd, bwd)
    return x * cos + partner * sin


def _tile_lanes(t, reps):
    return t if reps == 1 else jnp.concatenate([t] * reps, axis=1)


def _dot(a, b):
    return jnp.dot(a, b, preferred_element_type=F32)


def _dot_nt(a, b):
    return lax.dot_general(a, b, (((1,), (1,)), ((), ())), preferred_element_type=F32)


def _inproj_body(x_ref, g_ref, w_ref, oa_ref, ob_ref, oc_ref, od_ref):
    h = _rms(x_ref[...], g_ref[...]).astype(BF16)
    for o_ref, off in ((oa_ref, OFF_A), (ob_ref, OFF_B), (oc_ref, OFF_C), (od_ref, OFF_D)):
        width = o_ref.shape[-1]
        for j in range(0, width, MATMUL_COL_CHUNK):
            cw = min(MATMUL_COL_CHUNK, width - j)
            o_ref[:, j:j + cw] = _dot(h, w_ref[:, off + j:off + j + cw]).astype(o_ref.dtype)


def _inproj(x, g, w_bf16, tm, out_dtype):
    t = x.shape[0]
    widths = (COLS_A, COLS_B, COLS_C, COLS_D)
    osize = jnp.dtype(out_dtype).itemsize
    est = 2 * tm * D_MODEL * 4 + 2 * D_MODEL * IN_COLS * 2 + 2 * tm * IN_COLS * osize
    return pl.pallas_call(
        _inproj_body,
        grid=(t // tm,),
        in_specs=[pl.BlockSpec((tm, D_MODEL), lambda i: (i, 0)),
                  _const_spec((1, D_MODEL)),
                  _const_spec((D_MODEL, IN_COLS))],
        out_specs=[pl.BlockSpec((tm, wd), lambda i: (i, 0)) for wd in widths],
        out_shape=[jax.ShapeDtypeStruct((t, wd), out_dtype) for wd in widths],
        compiler_params=pltpu.CompilerParams(dimension_semantics=("parallel",),
                                             vmem_limit_bytes=_vmem_limit(est)),
        name="inproj",
    )(x, g, w_bf16)


def _outproj_body(x_ref, ya_ref, yb_ref, yc_ref, yd_ref, g_ref, wm_ref, wb_ref, wo_ref, fg_ref,
                  o_ref, *, final):
    x = x_ref[...]
    h = _rms(x, g_ref[...]).astype(BF16)
    merged = None
    for n, y_ref in enumerate((ya_ref, yb_ref, yc_ref, yd_ref)):
        gate = _sigmoid(_dot(h, wm_ref[:, n * D_MODEL:(n + 1) * D_MODEL]))
        term = gate * _dot(y_ref[...].astype(BF16), wb_ref[n])
        merged = term if merged is None else merged + term
    out = x + _dot(merged.astype(BF16), wo_ref[...])
    if final:
        out = _rms(out, fg_ref[...])
    o_ref[...] = out


def _outproj(x, ys, g, wm, wb, wo, fg, tm, final):
    t = x.shape[0]
    ysize = jnp.dtype(ys[0].dtype).itemsize
    est = (4 * tm * D_MODEL * 4 + 2 * 4 * tm * MIX * ysize
           + 2 * (4 * D_MODEL * D_MODEL + 4 * MIX * D_MODEL + D_MODEL * D_MODEL) * 2
           + 4 * tm * D_MODEL * 4)
    row = lambda i: (i, 0)
    return pl.pallas_call(
        functools.partial(_outproj_body, final=final),
        grid=(t // tm,),
        in_specs=[pl.BlockSpec((tm, D_MODEL), row)]
                 + [pl.BlockSpec((tm, MIX), row)] * 4
                 + [_const_spec((1, D_MODEL)),
                    _const_spec((D_MODEL, 4 * D_MODEL)),
                    _const_spec((4, MIX, D_MODEL)),
                    _const_spec((D_MODEL, D_MODEL)),
                    _const_spec((1, D_MODEL))],
        out_specs=pl.BlockSpec((tm, D_MODEL), row),
        out_shape=jax.ShapeDtypeStruct((t, D_MODEL), F32),
        compiler_params=pltpu.CompilerParams(dimension_semantics=("parallel",),
                                             vmem_limit_bytes=_vmem_limit(est)),
        name="outproj",
    )(x, *ys, g, wm, wb, wo, fg)


def _mix_ac_body(pa_ref, pc_ref, cw_ref, lg_ref, lb_ref, ws_ref, bs_ref,
                 ya_ref, yc_ref, nc_ref, tail_ref):
    ts = pa_ref.shape[0]

    @pl.when(pl.program_id(1) == 0)
    def _():
        tail_ref[...] = jnp.zeros_like(tail_ref)

    a_b = pa_ref[:, 0:MIX].astype(F32)
    z = pa_ref[:, MIX:2 * MIX].astype(F32) * pa_ref[:, 2 * MIX:3 * MIX].astype(F32)
    a_gate = pa_ref[:, 3 * MIX:4 * MIX].astype(F32)
    row = lax.broadcasted_iota(jnp.int32, z.shape, 0)
    prev1 = tail_ref[1:2, :]
    prev2 = tail_ref[0:1, :]
    z1 = jnp.where(row == 0, prev1, pltpu.roll(z, 1, 0))
    z2 = jnp.where(row == 0, prev2, jnp.where(row == 1, prev1, pltpu.roll(z, 2, 0)))
    conv = cw_ref[0:1, :] * z2 + cw_ref[1:2, :] * z1 + cw_ref[2:3, :] * z
    ya_ref[...] = (a_b * conv * _silu(a_gate)).astype(ya_ref.dtype)
    last2 = z[ts - 2:ts, :]
    tail_ref[...] = last2
    nc_ref[...] = last2

    c_v = pc_ref[:, MIX:2 * MIX].astype(F32)
    mu = jnp.mean(c_v, axis=-1, keepdims=True)
    cen = c_v - mu
    var = jnp.mean(cen * cen, axis=-1, keepdims=True)
    vn = (cen * lax.rsqrt(var + EPS) * lg_ref[...] + lb_ref[...]).astype(BF16)
    r = lax.broadcasted_iota(jnp.int32, (BLOCK, BLOCK), 0)
    c = lax.broadcasted_iota(jnp.int32, (BLOCK, BLOCK), 1)
    n_groups = MIX // HEAD_DIM
    w_causal = [jnp.where(r >= c, ws_ref[g], 0.0).astype(BF16) for g in range(n_groups)]
    for ck in range(ts // BLOCK):
        rows = slice(ck * BLOCK, (ck + 1) * BLOCK)
        vn_c = vn[rows, :]
        mixed = jnp.concatenate(
            [_dot(w_causal[g], vn_c[:, g * HEAD_DIM:(g + 1) * HEAD_DIM]) for g in range(n_groups)],
            axis=1) + bs_ref[...]
        c_u = pc_ref[rows, 0:MIX].astype(F32)
        c_gate = pc_ref[rows, 2 * MIX:3 * MIX].astype(F32)
        yc_ref[rows, :] = (c_u * mixed * _silu(c_gate)).astype(yc_ref.dtype)


def _mix_ac(pa, pc, conv_w, ln_g, ln_b, w_s, b_tile, ts):
    b, s, _ = pa.shape
    est = 2 * ts * (COLS_A + COLS_C) * 2 + 4 * ts * MIX * 2 + 2 * 6 * BLOCK * BLOCK * 4 + 6 * ts * MIX * 4
    tile = lambda bi, si: (bi, si, 0)
    return pl.pallas_call(
        _mix_ac_body,
        grid=(b, s // ts),
        in_specs=[pl.BlockSpec((None, ts, COLS_A), tile),
                  pl.BlockSpec((None, ts, COLS_C), tile),
                  _const_spec((3, MIX)), _const_spec((1, MIX)), _const_spec((1, MIX)),
                  _const_spec((MIX // HEAD_DIM, BLOCK, BLOCK)), _const_spec((BLOCK, MIX))],
        out_specs=[pl.BlockSpec((None, ts, MIX), tile),
                   pl.BlockSpec((None, ts, MIX), tile),
                   pl.BlockSpec((None, 2, MIX), lambda bi, si: (bi, 0, 0))],
        out_shape=[jax.ShapeDtypeStruct((b, s, MIX), BF16),
                   jax.ShapeDtypeStruct((b, s, MIX), BF16),
                   jax.ShapeDtypeStruct((b, 2, MIX), F32)],
        scratch_shapes=[pltpu.VMEM((2, MIX), F32)],
        compiler_params=pltpu.CompilerParams(dimension_semantics=("arbitrary", "arbitrary"),
                                             vmem_limit_bytes=_vmem_limit(est)),
        name="mix_ac",
    )(pa, pc, conv_w, ln_g, ln_b, w_s, b_tile)


def _band_mask(first_block):
    i = lax.broadcasted_iota(jnp.int32, (BLOCK, 2 * BLOCK), 0)
    j = lax.broadcasted_iota(jnp.int32, (BLOCK, 2 * BLOCK), 1)
    band = (j >= i) & (j <= i + BLOCK)
    return band & jnp.logical_or(jnp.logical_not(first_block), j >= BLOCK)


def _mix_b_body(sink_ref, pb_ref, cos_ref, sin_ref, yb_ref, kv_ref, q_s, k_s, v_s, ksw_s, vsw_s):
    blk = pl.program_id(1)
    ts = pb_ref.shape[0]
    hist = (k_s, v_s, ksw_s, vsw_s)

    @pl.when(blk == 0)
    def _():
        for r in hist:
            r[0:BLOCK, :] = jnp.zeros((BLOCK, BLOCK), BF16)

    cos = cos_ref[...]
    sin = sin_ref[...]
    q_s[...] = (_rope(pb_ref[:, 0:MIX].astype(F32), _tile_lanes(cos, 3), _tile_lanes(sin, 3)) * SCALE
                ).astype(BF16)
    k = _rope(pb_ref[:, MIX:MIX + BLOCK].astype(F32), cos, sin)
    v = pb_ref[:, MIX + BLOCK:MIX + 2 * BLOCK].astype(F32)
    k_s[BLOCK:BLOCK + ts, :] = k.astype(BF16)
    v_s[BLOCK:BLOCK + ts, :] = v.astype(BF16)
    ksw_s[BLOCK:BLOCK + ts, :] = pltpu.roll(k, HEAD_DIM, 1).astype(BF16)
    vsw_s[BLOCK:BLOCK + ts, :] = pltpu.roll(v, HEAD_DIM, 1).astype(BF16)

    @pl.when(blk == pl.num_programs(1) - 1)
    def _():
        kv_ref[0:BLOCK, :] = k[ts - BLOCK:ts, :].T
        kv_ref[BLOCK:2 * BLOCK, :] = v[ts - BLOCK:ts, :].T

    half0 = lax.broadcasted_iota(jnp.int32, (BLOCK, BLOCK), 1) < HEAD_DIM

    def block(c, carry):
        r0 = pl.multiple_of(c * BLOCK, BLOCK)
        rows = pl.ds(r0, BLOCK)
        win = pl.ds(r0, 2 * BLOCK)
        kk, vv, kk_sw, vv_sw = k_s[win, :], v_s[win, :], ksw_s[win, :], vsw_s[win, :]
        mask = _band_mask(jnp.logical_and(blk == 0, c == 0))
        outs = []
        for h in range(SWA_Q_HEADS):
            g, chunk, half = h // SWA_REP, h // 2, h % 2
            qc = q_s[rows, chunk * BLOCK:(chunk + 1) * BLOCK]
            q_h = jnp.where(half0 if half == 0 else jnp.logical_not(half0), qc, jnp.zeros_like(qc))
            s = jnp.where(mask, _dot_nt(q_h, kk if half == g else kk_sw), NEG_INF)
            sink = sink_ref[h]
            m = jnp.maximum(jnp.max(s, axis=-1, keepdims=True), sink)
            p = jnp.exp(s - m)
            den = jnp.sum(p, axis=-1, keepdims=True) + jnp.exp(sink - m)
            outs.append(_dot(p.astype(BF16), vv if half == g else vv_sw) / den)
        y = jnp.concatenate([jnp.where(half0, outs[2 * j], outs[2 * j + 1]) for j in range(3)], axis=1)
        gate = pb_ref[rows, MIX + 2 * BLOCK:COLS_B].astype(F32)
        yb_ref[rows, :] = (y * _silu(gate)).astype(yb_ref.dtype)
        return carry

    lax.fori_loop(0, ts // BLOCK, block, 0, unroll=2)
    for r in hist:
        r[0:BLOCK, :] = r[ts:ts + BLOCK, :]


def _mix_b(pb, sinks, cos, sin, ts):
    b, s, _ = pb.shape
    est = 2 * ts * COLS_B * 2 + 4 * ts * BLOCK * 4 + 2 * ts * MIX * 2 + 12 * ts * MIX * 4
    tile = lambda bi, si: (bi, si, 0)
    return pl.pallas_call(
        _mix_b_body,
        grid=(b, s // ts),
        in_specs=[pl.BlockSpec(memory_space=pltpu.SMEM),
                  pl.BlockSpec((None, ts, COLS_B), tile),
                  pl.BlockSpec((ts, BLOCK), lambda bi, si: (si, 0)),
                  pl.BlockSpec((ts, BLOCK), lambda bi, si: (si, 0))],
        out_specs=[pl.BlockSpec((None, ts, MIX), tile),
                   pl.BlockSpec((None, 2 * BLOCK, BLOCK), lambda bi, si: (bi, 0, 0))],
        out_shape=[jax.ShapeDtypeStruct((b, s, MIX), BF16),
                   jax.ShapeDtypeStruct((b, 2 * BLOCK, BLOCK), F32)],
        scratch_shapes=[pltpu.VMEM((ts, MIX), BF16)] + [pltpu.VMEM((BLOCK + ts, BLOCK), BF16)] * 4,
        compiler_params=pltpu.CompilerParams(dimension_semantics=("arbitrary", "arbitrary"),
                                             vmem_limit_bytes=_vmem_limit(est)),
        name="mix_b",
    )(sinks, pb, cos, sin)


def _mix_d_body(pd_ref, cos_ref, sin_ref, yd_ref, c1_ref, c4_ref, c16_ref,
                q_ref, k_ref, v_ref, o_ref, l_ref):
    sb = pl.program_id(1)

    lw = 2 * HEAD_DIM
    n_groups = len(DILATIONS)

    @pl.when(sb == 0)
    def _():
        k_ref[:, 0:SUPER, :] = jnp.zeros((n_groups, SUPER, lw), F32)
        v_ref[:, 0:SUPER, :] = jnp.zeros((n_groups, SUPER, lw), F32)

    cos = cos_ref[...]
    sin = sin_ref[...]
    for g in range(n_groups):
        gc = slice(g * lw, (g + 1) * lw)
        q_ref[g] = _rope(pd_ref[:, gc].astype(F32), cos, sin) * SCALE
        k_ref[g, SUPER:2 * SUPER, :] = _rope(pd_ref[:, MIX + g * lw:MIX + (g + 1) * lw].astype(F32), cos, sin)
        v_ref[g, SUPER:2 * SUPER, :] = pd_ref[:, 2 * MIX + g * lw:2 * MIX + (g + 1) * lw].astype(F32)

    n_blocks = SUPER // BLOCK
    for g, dil in enumerate(DILATIONS):

        def block(t, carry, dil=dil, g=g):
            r = t % dil
            j = t // dil
            base = j * (BLOCK * dil) + r
            qs = q_ref[g, pl.ds(base, BLOCK, stride=dil), :]
            kstart = SUPER + base - BLOCK * dil
            ks = k_ref[g, pl.ds(kstart, 2 * BLOCK, stride=dil), :]
            vs = v_ref[g, pl.ds(kstart, 2 * BLOCK, stride=dil), :]
            mask = _band_mask(jnp.logical_and(sb == 0, j == 0))
            ks_bf = ks.astype(BF16)
            vs_bf = vs.astype(BF16)
            head0 = lax.broadcasted_iota(jnp.int32, (BLOCK, lw), 1) < HEAD_DIM
            o_h, l_h = [], []
            for hh in range(2):
                q_h = jnp.where(head0 if hh == 0 else jnp.logical_not(head0), qs, 0.0).astype(BF16)
                s = jnp.where(mask, _dot_nt(q_h, ks_bf), NEG_INF)
                m = jnp.max(s, axis=-1, keepdims=True)
                p = jnp.exp(s - m)
                den = jnp.sum(p, axis=-1, keepdims=True)
                o_h.append(_dot(p.astype(BF16), vs_bf) / den)
                l_h.append(m + jnp.log(den))
            o_ref[g, pl.ds(base, BLOCK, stride=dil), :] = jnp.where(head0, o_h[0], o_h[1])
            l_ref[g, pl.ds(base, BLOCK, stride=dil), :] = jnp.where(head0, l_h[0], l_h[1])
            return carry

        lax.fori_loop(0, n_blocks, block, 0, unroll=BLOCKS_PER_TRIP)

    ls = [l_ref[g] for g in range(n_groups)]
    lmax = jnp.maximum(jnp.maximum(ls[0], ls[1]), ls[2])
    es = [jnp.exp(l - lmax) for l in ls]
    esum = es[0] + es[1] + es[2]
    for g in range(n_groups):
        gate = pd_ref[:, 3 * MIX + g * lw:3 * MIX + (g + 1) * lw].astype(F32)
        y = o_ref[g] * (es[g] / esum)
        yd_ref[:, g * lw:(g + 1) * lw] = (y * _silu(gate)).astype(yd_ref.dtype)

    for g, (c_ref, dil) in enumerate(zip((c1_ref, c4_ref, c16_ref), DILATIONS)):
        n = BLOCK * dil
        c_ref[0:lw, :] = k_ref[g, 2 * SUPER - n:2 * SUPER, :].T
        c_ref[lw:2 * lw, :] = v_ref[g, 2 * SUPER - n:2 * SUPER, :].T

    k_ref[:, 0:SUPER, :] = k_ref[:, SUPER:2 * SUPER, :]
    v_ref[:, 0:SUPER, :] = v_ref[:, SUPER:2 * SUPER, :]


def _mix_d(pd, cos, sin):
    b, s, _ = pd.shape
    est = (2 * SUPER * COLS_D * 2 + 4 * SUPER * BLOCK * 4 + 2 * SUPER * MIX * 2
           + 2 * KV_CH * (BLOCK + 4 * BLOCK + SUPER) * 4 + (3 + 2 * 2) * SUPER * MIX * 4)
    tile = lambda bi, si: (bi, si, 0)
    cache = lambda bi, si: (bi, 0, 0)
    return pl.pallas_call(
        _mix_d_body,
        grid=(b, s // SUPER),
        in_specs=[pl.BlockSpec((None, SUPER, COLS_D), tile),
                  pl.BlockSpec((SUPER, BLOCK), lambda bi, si: (si, 0)),
                  pl.BlockSpec((SUPER, BLOCK), lambda bi, si: (si, 0))],
        out_specs=[pl.BlockSpec((None, SUPER, MIX), tile)]
                  + [pl.BlockSpec((None, KV_CH, BLOCK * d), cache) for d in DILATIONS],
        out_shape=[jax.ShapeDtypeStruct((b, s, MIX), BF16)]
                  + [jax.ShapeDtypeStruct((b, KV_CH, BLOCK * d), F32) for d in DILATIONS],
        scratch_shapes=[pltpu.VMEM((3, SUPER, 2 * HEAD_DIM), F32),
                        pltpu.VMEM((3, 2 * SUPER, 2 * HEAD_DIM), F32),
                        pltpu.VMEM((3, 2 * SUPER, 2 * HEAD_DIM), F32),
                        pltpu.VMEM((3, SUPER, 2 * HEAD_DIM), F32),
                        pltpu.VMEM((3, SUPER, 2 * HEAD_DIM), F32)],
        compiler_params=pltpu.CompilerParams(dimension_semantics=("arbitrary", "arbitrary"),
                                             vmem_limit_bytes=_vmem_limit(est)),
        name="mix_d",
    )(pd, cos, sin)


def _shift_pass(c_ref, o_ref, i, ch0, new_b, per_tile):
    n_tiles = c_ref.shape[2] // BLOCK
    chans = pl.ds(ch0, 2 * HEAD_DIM)
    lane = lax.broadcasted_iota(jnp.int32, (2 * HEAD_DIM, BLOCK), 1)
    prev = None
    for t in range(n_tiles):
        x = c_ref[i, chans, t * BLOCK:(t + 1) * BLOCK]
        per_tile(t, x)
        r = pltpu.roll(x, BLOCK - 1, 1)
        if t > 0:
            o_ref[i, chans, (t - 1) * BLOCK:t * BLOCK] = jnp.where(lane < BLOCK - 1, prev, r)
        prev = r
    o_ref[i, chans, (n_tiles - 1) * BLOCK:n_tiles * BLOCK] = jnp.where(lane < BLOCK - 1, prev, new_b)


def _decode_cache(c_ref, o_ref, i, heads, k_new, v_new, dil):
    hd = HEAD_DIM
    n = c_ref.shape[2]
    n_tiles = n // BLOCK
    s_tiles = [[None] * (n_tiles + 1) for _ in heads]

    def scores(t, x):
        for qi, (g, q_b, _) in enumerate(heads):
            s_tiles[qi][t] = jnp.sum(x[g * hd:(g + 1) * hd, :] * q_b, axis=0, keepdims=True)

    _shift_pass(c_ref, o_ref, i, 0, k_new, scores)
    scores(n_tiles, k_new)

    lane = lax.broadcasted_iota(jnp.int32, (1, n + BLOCK), 1)
    valid = jnp.logical_or(jnp.logical_and(lane < n, (lane & (dil - 1)) == 0), lane == n + BLOCK - 1)
    probs, dens, lses = [], [], []
    for qi, (g, q_b, sink) in enumerate(heads):
        s = jnp.where(valid, jnp.concatenate(s_tiles[qi], axis=1), NEG_INF)
        m = jnp.max(s, axis=1, keepdims=True)
        if sink is not None:
            m = jnp.maximum(m, sink)
        p = jnp.exp(s - m)
        den = jnp.sum(p, axis=1, keepdims=True)
        if sink is not None:
            den = den + jnp.exp(sink - m)
        probs.append(p)
        dens.append(den)
        lses.append(m + jnp.log(den))

    accs = [jnp.zeros((hd, BLOCK), F32) for _ in heads]

    def weighted(t, x):
        for qi, (g, _, _) in enumerate(heads):
            accs[qi] = accs[qi] + x[g * hd:(g + 1) * hd, :] * probs[qi][:, t * BLOCK:(t + 1) * BLOCK]

    _shift_pass(c_ref, o_ref, i, 2 * hd, v_new, weighted)
    weighted(n_tiles, v_new)

    outs = [jnp.sum(accs[qi], axis=1, keepdims=True) / dens[qi] for qi in range(len(heads))]
    return outs, lses


XT_KB, XT_VB, XT_KD, XT_VD = 0, BLOCK, 2 * BLOCK, 2 * BLOCK + MIX
XT_ROWS = 2 * BLOCK + 2 * MIX


def _decode_body(*refs, seq_per_step, n_alias):
    (sink_ref, pa_ref, pb_ref, pc_ref, pd_ref, st_ref, cw_ref, lg_ref, lb_ref,
     wd_ref, bd_ref, cos_ref, sin_ref, cs_ref, c1_ref, c4_ref, c16_ref) = refs[:17]
    (ya_ref, yb_ref, yc_ref, yd_ref, nst_ref, vn_ref, ns_ref, n1_ref, n4_ref, n16_ref,
     xq_ref, xt_ref, yt_ref) = refs[17 + n_alias:]
    step = pl.program_id(0)
    n_seq = pa_ref.shape[0]
    assert n_seq == BLOCK

    @pl.when(step == 0)
    def _():
        a_b = pa_ref[:, 0:MIX]
        z = pa_ref[:, MIX:2 * MIX] * pa_ref[:, 2 * MIX:3 * MIX]
        prev0 = st_ref[:, 0:MIX]
        prev1 = st_ref[:, MIX:2 * MIX]
        conv = cw_ref[0:1, :] * prev0 + cw_ref[1:2, :] * prev1 + cw_ref[2:3, :] * z
        ya_ref[...] = a_b * conv * _silu(pa_ref[:, 3 * MIX:4 * MIX])
        nst_ref[:, 0:MIX] = prev1
        nst_ref[:, MIX:2 * MIX] = z

        c_v = pc_ref[:, MIX:2 * MIX]
        mu = jnp.mean(c_v, axis=-1, keepdims=True)
        cen = c_v - mu
        var = jnp.mean(cen * cen, axis=-1, keepdims=True)
        vn = cen * lax.rsqrt(var + EPS) * lg_ref[...] + lb_ref[...]
        vn_ref[...] = vn
        mixed = wd_ref[...] * vn + bd_ref[...]
        yc_ref[...] = pc_ref[:, 0:MIX] * mixed * _silu(pc_ref[:, 2 * MIX:3 * MIX])

        cos = cos_ref[...]
        sin = sin_ref[...]
        cos3, sin3 = _tile_lanes(cos, 3), _tile_lanes(sin, 3)
        xq_ref[0:MIX, :] = (_rope(pb_ref[:, 0:MIX], cos3, sin3) * SCALE).T.astype(BF16)
        xq_ref[MIX:2 * MIX, :] = (_rope(pd_ref[:, 0:MIX], cos3, sin3) * SCALE).T.astype(BF16)
        xt_ref[XT_KB:XT_VB, :] = _rope(pb_ref[:, MIX:MIX + BLOCK], cos, sin).T
        xt_ref[XT_VB:XT_KD, :] = pb_ref[:, MIX + BLOCK:MIX + 2 * BLOCK].T
        xt_ref[XT_KD:XT_VD, :] = _rope(pd_ref[:, MIX:2 * MIX], cos3, sin3).T
        xt_ref[XT_VD:XT_ROWS, :] = pd_ref[:, 2 * MIX:3 * MIX].T
        yt_ref[...] = jnp.zeros_like(yt_ref)

    hd = HEAD_DIM

    def one_sequence(i, carry):
        b = step * seq_per_step + i
        onehot = (lax.broadcasted_iota(jnp.int32, (n_seq, BLOCK), 0) == b).astype(BF16)
        q_b = _dot(xq_ref[...], onehot)
        new_rot = pltpu.roll(xt_ref[...], BLOCK - 1 - b, 1)
        heads = [(h // SWA_REP, q_b[h * hd:(h + 1) * hd], jnp.full((1, 1), sink_ref[h], F32))
                 for h in range(SWA_Q_HEADS)]
        outs, _ = _decode_cache(cs_ref, ns_ref, i, heads, new_rot[XT_KB:XT_VB], new_rot[XT_VB:XT_KD], 1)
        yb_col = jnp.concatenate(outs, axis=0)

        q_col = q_b[MIX:2 * MIX]
        k_new = new_rot[XT_KD:XT_VD]
        v_new = new_rot[XT_VD:XT_ROWS]
        o_heads, l_heads = [], []
        for g, (c_ref, n_ref, dil) in enumerate(zip((c1_ref, c4_ref, c16_ref),
                                                    (n1_ref, n4_ref, n16_ref), DILATIONS)):
            gc = slice(2 * g * hd, 2 * (g + 1) * hd)
            heads = [(hh, q_col[(2 * g + hh) * hd:(2 * g + hh + 1) * hd], None) for hh in range(2)]
            o, lse = _decode_cache(c_ref, n_ref, i, heads, k_new[gc], v_new[gc], dil)
            o_heads += o
            l_heads += lse
        cols = []
        for hh in range(2):
            ls = [l_heads[2 * g + hh] for g in range(3)]
            lmax = jnp.maximum(jnp.maximum(ls[0], ls[1]), ls[2])
            es = [jnp.exp(l - lmax) for l in ls]
            esum = es[0] + es[1] + es[2]
            cols.append([o_heads[2 * g + hh] * (es[g] / esum) for g in range(3)])
        yd_col = jnp.concatenate([cols[hh][g] for g in range(3) for hh in range(2)], axis=0)
        y_col = jnp.concatenate([yb_col, yd_col], axis=0)
        lane_y = lax.broadcasted_iota(jnp.int32, (2 * MIX, n_seq), 1)
        yt_ref[...] = jnp.where(lane_y == b, y_col, yt_ref[...])
        return carry

    lax.fori_loop(0, seq_per_step, one_sequence, 0, unroll=True)

    @pl.when(step == pl.num_programs(0) - 1)
    def _():
        yb_ref[...] = yt_ref[0:MIX, :].T * _silu(pb_ref[:, MIX + 2 * BLOCK:COLS_B])
        yd_ref[...] = yt_ref[MIX:2 * MIX, :].T * _silu(pd_ref[:, 3 * MIX:4 * MIX])


def _decode(layer, sinks, pa, pb, pc, pd, state, conv_w, ln_g, ln_b, w_diag, b_diag, cos, sin,
            caches, prev_outs, seq_per_step):
    nb = pa.shape[0]
    rows = sum(c.shape[3] for c in caches)
    est = 4 * seq_per_step * KV_CH * rows * 4 + 4 * nb * (IN_COLS + 8 * MIX) * 4
    cache_spec = lambda c: pl.BlockSpec((None, seq_per_step) + c.shape[2:], lambda i: (layer, i, 0, 0))
    small = (pa, pb, pc, pd, state, conv_w, ln_g, ln_b, w_diag, b_diag, cos, sin)
    n_small_out = 6
    first_alias_in = 1 + len(small) + len(caches)
    return pl.pallas_call(
        functools.partial(_decode_body, seq_per_step=seq_per_step, n_alias=len(prev_outs)),
        grid=(nb // seq_per_step,),
        in_specs=[pl.BlockSpec(memory_space=pltpu.SMEM)]
                 + [_const_spec(a.shape) for a in small]
                 + [cache_spec(c) for c in caches]
                 + [pl.BlockSpec(memory_space=pl.ANY) for _ in prev_outs],
        out_specs=[_const_spec((nb, MIX))] * 4
                  + [_const_spec((nb, 2 * MIX)), _const_spec((nb, MIX))]
                  + [cache_spec(c) for c in caches],
        out_shape=[jax.ShapeDtypeStruct((nb, MIX), F32)] * 4
                  + [jax.ShapeDtypeStruct((nb, 2 * MIX), F32), jax.ShapeDtypeStruct((nb, MIX), F32)]
                  + [jax.ShapeDtypeStruct(c.shape, F32) for c in caches],
        scratch_shapes=[pltpu.VMEM((2 * MIX, nb), BF16), pltpu.VMEM((XT_ROWS, nb), F32),
                        pltpu.VMEM((2 * MIX, nb), F32)],
        input_output_aliases={first_alias_in + i: n_small_out + i for i in range(len(prev_outs))},
        compiler_params=pltpu.CompilerParams(dimension_semantics=("arbitrary",),
                                             vmem_limit_bytes=_vmem_limit(est)),
        name="decode",
    )(sinks, *small, *caches, *prev_outs)


def _rope_tables(pos):
    inv_freq = ROPE_THETA ** (-jnp.arange(HALF_HEAD, dtype=F32) / HALF_HEAD)
    ang = pos.astype(F32)[:, None] * inv_freq[None, :]
    cos, sin = jnp.cos(ang), jnp.sin(ang)
    return (jnp.concatenate([cos, cos, cos, cos], axis=1),
            jnp.concatenate([-sin, sin, -sin, sin], axis=1))


def _cache_view(c):
    d, b, n = c.shape[:3]
    return jnp.transpose(c, (0, 1, 3, 4, 5, 2)).reshape(d, b, KV_CH, n)


def _cache_unview(c):
    lead, n = c.shape[:-2], c.shape[-1]
    nl = len(lead)
    perm = tuple(range(nl)) + (nl + 3, nl, nl + 1, nl + 2)
    return jnp.transpose(c.reshape(lead + (2, 2, HEAD_DIM, n)), perm)


def kernel(x_prompt, x_sample, state_conv, cache_swa_kv, cache_dil1_kv, cache_dil4_kv, cache_dil16_kv,
           norm_g, w_in, conv_w, attn_sinks, v_ln_g, v_ln_b, w_spatial, b_spatial,
           w_branch, w_merge, w_out, final_norm_g):
    depth = w_in.shape[0]
    nb, seq, _ = x_prompt.shape
    ns, dec_seq, _ = x_sample.shape
    assert dec_seq == 1 and seq % SUPER == 0
    past_len = 16384
    assert cache_swa_kv.shape[2] == BLOCK and cache_dil16_kv.shape[2] == SUPER

    cos_p, sin_p = _rope_tables(jnp.arange(seq, dtype=jnp.int32))
    cos_s, sin_s = _rope_tables(past_len + jnp.arange(1, dtype=jnp.int32))
    w_in_bf = w_in.astype(BF16)
    w_merge_bf = w_merge.astype(BF16)
    w_branch_bf = w_branch.astype(BF16)
    w_out_bf = w_out.astype(BF16)
    caches = [_cache_view(c) for c in (cache_swa_kv, cache_dil1_kv, cache_dil4_kv, cache_dil16_kv)]
    fg = final_norm_g.reshape(1, D_MODEL)

    xp = x_prompt.reshape(nb * seq, D_MODEL)
    xs = x_sample.reshape(ns, D_MODEL)
    conv_p, conv_s, chunk_v = [], [], []
    new_p = [[] for _ in range(4)]
    new_s = ()
    for l in range(depth):
        g = norm_g[l].reshape(1, D_MODEL)
        final = l == depth - 1
        ln_g = v_ln_g[l].reshape(1, MIX)
        ln_b = v_ln_b[l].reshape(1, MIX)

        pa, pb, pc, pd = _inproj(xp, g, w_in_bf[l], 512, BF16)
        shp = lambda a: a.reshape(nb, seq, a.shape[-1])
        b_tile = jnp.repeat(b_spatial[l].T, HEAD_DIM, axis=1)
        ya, yc, nconv = _mix_ac(shp(pa), shp(pc), conv_w[l], ln_g, ln_b, w_spatial[l], b_tile, 512)
        yb, kv_swa = _mix_b(shp(pb), attn_sinks[l], cos_p, sin_p, 1024)
        yd, kv1, kv4, kv16 = _mix_d(shp(pd), cos_p, sin_p)
        flat = lambda a: a.reshape(nb * seq, MIX)
        xp = _outproj(xp, (flat(ya), flat(yb), flat(yc), flat(yd)), g, w_merge_bf[l], w_branch_bf[l],
                      w_out_bf[l], fg, 512, final)
        conv_p.append(nconv)
        new_p[0].append(_cache_unview(kv_swa))
        for i, kv in enumerate((kv1, kv4, kv16)):
            new_p[i + 1].append(_cache_unview(kv))

        sa, sb, sc, sd = _inproj(xs, g, w_in_bf[l], ns, F32)
        w_diag = jnp.repeat(w_spatial[l][:, 0, 0], HEAD_DIM).reshape(1, MIX)
        b_diag = jnp.repeat(b_spatial[l][:, 0], HEAD_DIM).reshape(1, MIX)
        outs = _decode(l, attn_sinks[l], sa, sb, sc, sd, state_conv[l].reshape(ns, 2 * MIX), conv_w[l],
                       ln_g, ln_b, w_diag, b_diag, cos_s, sin_s, caches, new_s, 2)
        za, zb, zc, zd, nstate, vn = outs[:6]
        new_s = tuple(outs[6:])
        xs = _outproj(xs, (za, zb, zc, zd), g, w_merge_bf[l], w_branch_bf[l], w_out_bf[l], fg, ns, final)
        conv_s.append(nstate.reshape(ns, 2, MIX))
        chunk_v.append(vn.reshape(ns, 1, MIX))

    st = jnp.stack
    new_s = [_cache_unview(c) for c in new_s]
    return (xp.reshape(nb, seq, D_MODEL), xs.reshape(ns, 1, D_MODEL),
            st(conv_p), st(conv_s),
            st(new_p[0]), new_s[0], st(new_p[1]), new_s[1],
            st(new_p[2]), new_s[2], st(new_p[3]), new_s[3],
            st(chunk_v))
```

```python
import functools
from typing import NamedTuple

import jax
import jax.numpy as jnp
from jax import lax
from jax.experimental import pallas as pl
from jax.experimental.pallas import tpu as pltpu

F32 = jnp.float32
BF16 = jnp.bfloat16

D_MODEL = 1024
HEAD_DIM = 64
HALF_HEAD = HEAD_DIM // 2
MIX = 384
BLOCK = 128
SWA_Q_HEADS = 6
SWA_KV_HEADS = 2
SWA_REP = SWA_Q_HEADS // SWA_KV_HEADS
DILATIONS = (1, 4, 16)
SUPER = BLOCK * DILATIONS[-1]
BLOCKS_PER_TRIP = 4
ROPE_THETA = 10000.0
EPS = 1e-6
NEG_INF = -1e30
SCALE = HEAD_DIM ** -0.5

COLS_A = 4 * MIX
COLS_B = MIX + 2 * BLOCK + MIX
COLS_C = 3 * MIX
COLS_D = 4 * MIX
OFF_A, OFF_B, OFF_C, OFF_D = 0, COLS_A, COLS_A + COLS_B, COLS_A + COLS_B + COLS_C
IN_COLS = OFF_D + COLS_D
KV_CH = 4 * HEAD_DIM

V7X_VMEM_BYTES = 64 * 1024 * 1024
MATMUL_COL_CHUNK = 512


def _vmem_limit(estimate_bytes):
    return int(min(V7X_VMEM_BYTES - 8 * 1024 * 1024, estimate_bytes + 16 * 1024 * 1024))


def _const_spec(shape, single=False):
    nd = len(shape)
    if single:
        return pl.BlockSpec(shape, lambda *_: (0,) * nd, pipeline_mode=pl.Buffered(1))
    return pl.BlockSpec(shape, lambda *_: (0,) * nd)


def _rms(x, g):
    ms = jnp.mean(x * x, axis=-1, keepdims=True)
    return x * lax.rsqrt(ms + EPS) * g


def _sigmoid(x):
    return 1.0 / (1.0 + jnp.exp(-x))


def _silu(x):
    return x * _sigmoid(x)


def _rope(x, cos, sin):
    w = x.shape[-1]
    lane = lax.broadcasted_iota(jnp.int32, x.shape, 1)
    fwd = pltpu.roll(x, w - HALF_HEAD, 1)
    bwd = pltpu.roll(x, HALF_HEAD, 1)
    partner = jnp.where((lane & HALF_HEAD) == 0, fwd, bwd)
    return x * cos + partner * sin


def _tile_lanes(t, reps):
    return t if reps == 1 else jnp.concatenate([t] * reps, axis=1)


def _dot(a, b):
    return jnp.dot(a, b, preferred_element_type=F32)


def _dot_nt(a, b):
    return lax.dot_general(a, b, (((1,), (1,)), ((), ())), preferred_element_type=F32)


def _inproj_body(*refs, dec):
    if dec is None:
        x_ref, g_ref, w_ref = refs[:3]
        outs = refs[3:]
    else:
        sink_ref, x_ref, g_ref, w_ref, xq_ref, xt_ref = refs[:6]
        outs = refs[10 + dec.n_alias:10 + dec.n_alias + 4]
        _decode_sequences(dec, sink_ref, xq_ref, xt_ref, refs[6:10], refs[14 + dec.n_alias:])
    h = _rms(x_ref[...], g_ref[...]).astype(BF16)
    for o_ref, off in zip(outs, (OFF_A, OFF_B, OFF_C, OFF_D)):
        width = o_ref.shape[-1]
        for j in range(0, width, MATMUL_COL_CHUNK):
            cw = min(MATMUL_COL_CHUNK, width - j)
            o_ref[:, j:j + cw] = _dot(h, w_ref[:, off + j:off + j + cw]).astype(o_ref.dtype)


def _inproj(x, g, w_bf16, tm, out_dtype, dec=None):
    t = x.shape[0]
    widths = (COLS_A, COLS_B, COLS_C, COLS_D)
    osize = jnp.dtype(out_dtype).itemsize
    est = 2 * tm * D_MODEL * 4 + D_MODEL * IN_COLS * 2 + 2 * tm * IN_COLS * osize
    row = lambda i: (i, 0)
    in_specs = [pl.BlockSpec((tm, D_MODEL), row), _const_spec((1, D_MODEL)),
                _const_spec((D_MODEL, IN_COLS), single=True)]
    out_specs = [pl.BlockSpec((tm, wd), row) for wd in widths]
    out_shape = [jax.ShapeDtypeStruct((t, wd), out_dtype) for wd in widths]
    args = [x, g, w_bf16]
    aliases = {}
    if dec is not None:
        assert t // tm == dec.steps
        in_specs, out_specs, out_shape, args, aliases, extra = dec.extend(in_specs, out_specs, out_shape, args)
        est += extra
    return pl.pallas_call(
        functools.partial(_inproj_body, dec=None if dec is None else dec.static),
        grid=(t // tm,),
        in_specs=in_specs, out_specs=out_specs, out_shape=out_shape,
        input_output_aliases=aliases,
        compiler_params=pltpu.CompilerParams(
            dimension_semantics=("parallel",) if dec is None else ("arbitrary",),
            vmem_limit_bytes=_vmem_limit(est)),
        name="inproj" if dec is None else "inproj_dec",
    )(*args)


def _outproj_body(*refs, final, dec):
    if dec is None:
        (x_ref, ya_ref, yb_ref, yc_ref, yd_ref, g_ref, wm_ref, wb_ref, wo_ref, fg_ref, o_ref) = refs
    else:
        sink_ref = refs[0]
        (x_ref, ya_ref, yb_ref, yc_ref, yd_ref, g_ref, wm_ref, wb_ref, wo_ref, fg_ref) = refs[1:11]
        xq_ref, xt_ref = refs[11:13]
        o_ref = refs[17 + dec.n_alias]
        _decode_sequences(dec, sink_ref, xq_ref, xt_ref, refs[13:17], refs[18 + dec.n_alias:])
    x = x_ref[...]
    h = _rms(x, g_ref[...]).astype(BF16)
    merged = None
    for n, y_ref in enumerate((ya_ref, yb_ref, yc_ref, yd_ref)):
        gate = _sigmoid(_dot(h, wm_ref[:, n * D_MODEL:(n + 1) * D_MODEL]))
        term = gate * _dot(y_ref[...].astype(BF16), wb_ref[n])
        merged = term if merged is None else merged + term
    out = x + _dot(merged.astype(BF16), wo_ref[...])
    if final:
        out = _rms(out, fg_ref[...])
    o_ref[...] = out


def _outproj(x, ys, g, wm, wb, wo, fg, tm, final, dec=None):
    t = x.shape[0]
    ysize = jnp.dtype(ys[0].dtype).itemsize
    est = (4 * tm * D_MODEL * 4 + 2 * 4 * tm * MIX * ysize
           + (4 * D_MODEL * D_MODEL + 4 * MIX * D_MODEL + D_MODEL * D_MODEL) * 2
           + 4 * tm * D_MODEL * 4)
    row = lambda i: (i, 0)
    in_specs = ([pl.BlockSpec((tm, D_MODEL), row)]
                + [pl.BlockSpec((tm, MIX), row)] * 4
                + [_const_spec((1, D_MODEL)),
                   _const_spec((D_MODEL, 4 * D_MODEL), single=True),
                   _const_spec((4, MIX, D_MODEL), single=True),
                   _const_spec((D_MODEL, D_MODEL), single=True),
                   _const_spec((1, D_MODEL))])
    out_specs = [pl.BlockSpec((tm, D_MODEL), row)]
    out_shape = [jax.ShapeDtypeStruct((t, D_MODEL), F32)]
    args = [x, *ys, g, wm, wb, wo, fg]
    aliases = {}
    if dec is not None:
        assert t // tm == dec.steps
        in_specs, out_specs, out_shape, args, aliases, extra = dec.extend(in_specs, out_specs, out_shape, args)
        est += extra
    return pl.pallas_call(
        functools.partial(_outproj_body, final=final, dec=None if dec is None else dec.static),
        grid=(t // tm,),
        in_specs=in_specs, out_specs=out_specs, out_shape=out_shape,
        input_output_aliases=aliases,
        compiler_params=pltpu.CompilerParams(
            dimension_semantics=("parallel",) if dec is None else ("arbitrary",),
            vmem_limit_bytes=_vmem_limit(est)),
        name="outproj" if dec is None else "outproj_dec",
    )(*args)


def _mix_ac_body(pa_ref, pc_ref, cw_ref, lg_ref, lb_ref, ws_ref, bs_ref,
                 ya_ref, yc_ref, nc_ref, tail_ref):
    ts = pa_ref.shape[0]

    @pl.when(pl.program_id(1) == 0)
    def _():
        tail_ref[...] = jnp.zeros_like(tail_ref)

    a_b = pa_ref[:, 0:MIX].astype(F32)
    z = pa_ref[:, MIX:2 * MIX].astype(F32) * pa_ref[:, 2 * MIX:3 * MIX].astype(F32)
    a_gate = pa_ref[:, 3 * MIX:4 * MIX].astype(F32)
    row = lax.broadcasted_iota(jnp.int32, z.shape, 0)
    prev1 = tail_ref[1:2, :]
    prev2 = tail_ref[0:1, :]
    z1 = jnp.where(row == 0, prev1, pltpu.roll(z, 1, 0))
    z2 = jnp.where(row == 0, prev2, jnp.where(row == 1, prev1, pltpu.roll(z, 2, 0)))
    conv = cw_ref[0:1, :] * z2 + cw_ref[1:2, :] * z1 + cw_ref[2:3, :] * z
    ya_ref[...] = (a_b * conv * _silu(a_gate)).astype(ya_ref.dtype)
    last2 = z[ts - 2:ts, :]
    tail_ref[...] = last2
    nc_ref[...] = last2

    c_v = pc_ref[:, MIX:2 * MIX].astype(F32)
    mu = jnp.mean(c_v, axis=-1, keepdims=True)
    cen = c_v - mu
    var = jnp.mean(cen * cen, axis=-1, keepdims=True)
    vn = (cen * lax.rsqrt(var + EPS) * lg_ref[...] + lb_ref[...]).astype(BF16)
    r = lax.broadcasted_iota(jnp.int32, (BLOCK, BLOCK), 0)
    c = lax.broadcasted_iota(jnp.int32, (BLOCK, BLOCK), 1)
    n_groups = MIX // HEAD_DIM
    w_causal = [jnp.where(r >= c, ws_ref[g], 0.0).astype(BF16) for g in range(n_groups)]
    for ck in range(ts // BLOCK):
        rows = slice(ck * BLOCK, (ck + 1) * BLOCK)
        vn_c = vn[rows, :]
        mixed = jnp.concatenate(
            [_dot(w_causal[g], vn_c[:, g * HEAD_DIM:(g + 1) * HEAD_DIM]) for g in range(n_groups)],
            axis=1) + bs_ref[...]
        c_u = pc_ref[rows, 0:MIX].astype(F32)
        c_gate = pc_ref[rows, 2 * MIX:3 * MIX].astype(F32)
        yc_ref[rows, :] = (c_u * mixed * _silu(c_gate)).astype(yc_ref.dtype)


def _mix_ac(pa, pc, conv_w, ln_g, ln_b, w_s, b_tile, ts):
    b, s, _ = pa.shape
    est = 2 * ts * (COLS_A + COLS_C) * 2 + 4 * ts * MIX * 2 + 2 * 6 * BLOCK * BLOCK * 4 + 6 * ts * MIX * 4
    tile = lambda bi, si: (bi, si, 0)
    return pl.pallas_call(
        _mix_ac_body,
        grid=(b, s // ts),
        in_specs=[pl.BlockSpec((None, ts, COLS_A), tile),
                  pl.BlockSpec((None, ts, COLS_C), tile),
                  _const_spec((3, MIX)), _const_spec((1, MIX)), _const_spec((1, MIX)),
                  _const_spec((MIX // HEAD_DIM, BLOCK, BLOCK)), _const_spec((BLOCK, MIX))],
        out_specs=[pl.BlockSpec((None, ts, MIX), tile),
                   pl.BlockSpec((None, ts, MIX), tile),
                   pl.BlockSpec((None, 2, MIX), lambda bi, si: (bi, 0, 0))],
        out_shape=[jax.ShapeDtypeStruct((b, s, MIX), BF16),
                   jax.ShapeDtypeStruct((b, s, MIX), BF16),
                   jax.ShapeDtypeStruct((b, 2, MIX), F32)],
        scratch_shapes=[pltpu.VMEM((2, MIX), F32)],
        compiler_params=pltpu.CompilerParams(dimension_semantics=("arbitrary", "arbitrary"),
                                             vmem_limit_bytes=_vmem_limit(est)),
        name="mix_ac",
    )(pa, pc, conv_w, ln_g, ln_b, w_s, b_tile)


def _band_mask(first_block):
    i = lax.broadcasted_iota(jnp.int32, (BLOCK, 2 * BLOCK), 0)
    j = lax.broadcasted_iota(jnp.int32, (BLOCK, 2 * BLOCK), 1)
    band = (j >= i) & (j <= i + BLOCK)
    return band & jnp.logical_or(jnp.logical_not(first_block), j >= BLOCK)


def _mix_b_body(sink_ref, pb_ref, cos_ref, sin_ref, yb_ref, kv_ref, q_s, k_s, v_s, ksw_s, vsw_s):
    blk = pl.program_id(1)
    ts = pb_ref.shape[0]
    hist = (k_s, v_s, ksw_s, vsw_s)

    @pl.when(blk == 0)
    def _():
        for r in hist:
            r[0:BLOCK, :] = jnp.zeros((BLOCK, BLOCK), BF16)

    cos = cos_ref[...]
    sin = sin_ref[...]
    q_s[...] = (_rope(pb_ref[:, 0:MIX].astype(F32), _tile_lanes(cos, 3), _tile_lanes(sin, 3)) * SCALE
                ).astype(BF16)
    k = _rope(pb_ref[:, MIX:MIX + BLOCK].astype(F32), cos, sin)
    v = pb_ref[:, MIX + BLOCK:MIX + 2 * BLOCK].astype(F32)
    k_s[BLOCK:BLOCK + ts, :] = k.astype(BF16)
    v_s[BLOCK:BLOCK + ts, :] = v.astype(BF16)
    ksw_s[BLOCK:BLOCK + ts, :] = pltpu.roll(k, HEAD_DIM, 1).astype(BF16)
    vsw_s[BLOCK:BLOCK + ts, :] = pltpu.roll(v, HEAD_DIM, 1).astype(BF16)

    @pl.when(blk == pl.num_programs(1) - 1)
    def _():
        kv_ref[0:BLOCK, :] = k[ts - BLOCK:ts, :].T
        kv_ref[BLOCK:2 * BLOCK, :] = v[ts - BLOCK:ts, :].T

    half0 = lax.broadcasted_iota(jnp.int32, (BLOCK, BLOCK), 1) < HEAD_DIM

    def block(c, carry):
        r0 = pl.multiple_of(c * BLOCK, BLOCK)
        rows = pl.ds(r0, BLOCK)
        win = pl.ds(r0, 2 * BLOCK)
        kk, vv, kk_sw, vv_sw = k_s[win, :], v_s[win, :], ksw_s[win, :], vsw_s[win, :]
        mask = _band_mask(jnp.logical_and(blk == 0, c == 0))
        outs = []
        for h in range(SWA_Q_HEADS):
            g, chunk, half = h // SWA_REP, h // 2, h % 2
            qc = q_s[rows, chunk * BLOCK:(chunk + 1) * BLOCK]
            q_h = jnp.where(half0 if half == 0 else jnp.logical_not(half0), qc, jnp.zeros_like(qc))
            s = jnp.where(mask, _dot_nt(q_h, kk if half == g else kk_sw), NEG_INF)
            sink = sink_ref[h]
            m = jnp.maximum(jnp.max(s, axis=-1, keepdims=True), sink)
            p = jnp.exp(s - m)
            den = jnp.sum(p, axis=-1, keepdims=True) + jnp.exp(sink - m)
            outs.append(_dot(p.astype(BF16), vv if half == g else vv_sw) / den)
        y = jnp.concatenate([jnp.where(half0, outs[2 * j], outs[2 * j + 1]) for j in range(3)], axis=1)
        gate = pb_ref[rows, MIX + 2 * BLOCK:COLS_B].astype(F32)
        yb_ref[rows, :] = (y * _silu(gate)).astype(yb_ref.dtype)
        return carry

    lax.fori_loop(0, ts // BLOCK, block, 0, unroll=2)
    for r in hist:
        r[0:BLOCK, :] = r[ts:ts + BLOCK, :]


def _mix_b(pb, sinks, cos, sin, ts):
    b, s, _ = pb.shape
    est = 2 * ts * COLS_B * 2 + 4 * ts * BLOCK * 4 + 2 * ts * MIX * 2 + 12 * ts * MIX * 4
    tile = lambda bi, si: (bi, si, 0)
    return pl.pallas_call(
        _mix_b_body,
        grid=(b, s // ts),
        in_specs=[pl.BlockSpec(memory_space=pltpu.SMEM),
                  pl.BlockSpec((None, ts, COLS_B), tile),
                  pl.BlockSpec((ts, BLOCK), lambda bi, si: (si, 0)),
                  pl.BlockSpec((ts, BLOCK), lambda bi, si: (si, 0))],
        out_specs=[pl.BlockSpec((None, ts, MIX), tile),
                   pl.BlockSpec((None, 2 * BLOCK, BLOCK), lambda bi, si: (bi, 0, 0))],
        out_shape=[jax.ShapeDtypeStruct((b, s, MIX), BF16),
                   jax.ShapeDtypeStruct((b, 2 * BLOCK, BLOCK), F32)],
        scratch_shapes=[pltpu.VMEM((ts, MIX), BF16)] + [pltpu.VMEM((BLOCK + ts, BLOCK), BF16)] * 4,
        compiler_params=pltpu.CompilerParams(dimension_semantics=("arbitrary", "arbitrary"),
                                             vmem_limit_bytes=_vmem_limit(est)),
        name="mix_b",
    )(sinks, pb, cos, sin)


def _mix_d_body(pd_ref, cos_ref, sin_ref, yd_ref, c1_ref, c4_ref, c16_ref,
                q_ref, k_ref, v_ref, o_ref, l_ref):
    sb = pl.program_id(1)

    lw = 2 * HEAD_DIM
    n_groups = len(DILATIONS)

    @pl.when(sb == 0)
    def _():
        k_ref[:, 0:SUPER, :] = jnp.zeros((n_groups, SUPER, lw), F32)
        v_ref[:, 0:SUPER, :] = jnp.zeros((n_groups, SUPER, lw), F32)

    cos = cos_ref[...]
    sin = sin_ref[...]
    for g in range(n_groups):
        gc = slice(g * lw, (g + 1) * lw)
        q_ref[g] = _rope(pd_ref[:, gc].astype(F32), cos, sin) * SCALE
        k_ref[g, SUPER:2 * SUPER, :] = _rope(pd_ref[:, MIX + g * lw:MIX + (g + 1) * lw].astype(F32), cos, sin)
        v_ref[g, SUPER:2 * SUPER, :] = pd_ref[:, 2 * MIX + g * lw:2 * MIX + (g + 1) * lw].astype(F32)

    n_blocks = SUPER // BLOCK
    for g, dil in enumerate(DILATIONS):

        def block(t, carry, dil=dil, g=g):
            r = t % dil
            j = t // dil
            base = j * (BLOCK * dil) + r
            qs = q_ref[g, pl.ds(base, BLOCK, stride=dil), :]
            kstart = SUPER + base - BLOCK * dil
            ks = k_ref[g, pl.ds(kstart, 2 * BLOCK, stride=dil), :]
            vs = v_ref[g, pl.ds(kstart, 2 * BLOCK, stride=dil), :]
            mask = _band_mask(jnp.logical_and(sb == 0, j == 0))
            ks_bf = ks.astype(BF16)
            vs_bf = vs.astype(BF16)
            head0 = lax.broadcasted_iota(jnp.int32, (BLOCK, lw), 1) < HEAD_DIM
            o_h, l_h = [], []
            for hh in range(2):
                q_h = jnp.where(head0 if hh == 0 else jnp.logical_not(head0), qs, 0.0).astype(BF16)
                s = jnp.where(mask, _dot_nt(q_h, ks_bf), NEG_INF)
                m = jnp.max(s, axis=-1, keepdims=True)
                p = jnp.exp(s - m)
                den = jnp.sum(p, axis=-1, keepdims=True)
                o_h.append(_dot(p.astype(BF16), vs_bf) / den)
                l_h.append(m + jnp.log(den))
            o_ref[g, pl.ds(base, BLOCK, stride=dil), :] = jnp.where(head0, o_h[0], o_h[1])
            l_ref[g, pl.ds(base, BLOCK, stride=dil), :] = jnp.where(head0, l_h[0], l_h[1])
            return carry

        lax.fori_loop(0, n_blocks, block, 0, unroll=BLOCKS_PER_TRIP)

    ls = [l_ref[g] for g in range(n_groups)]
    lmax = jnp.maximum(jnp.maximum(ls[0], ls[1]), ls[2])
    es = [jnp.exp(l - lmax) for l in ls]
    esum = es[0] + es[1] + es[2]
    for g in range(n_groups):
        gate = pd_ref[:, 3 * MIX + g * lw:3 * MIX + (g + 1) * lw].astype(F32)
        y = o_ref[g] * (es[g] / esum)
        yd_ref[:, g * lw:(g + 1) * lw] = (y * _silu(gate)).astype(yd_ref.dtype)

    for g, (c_ref, dil) in enumerate(zip((c1_ref, c4_ref, c16_ref), DILATIONS)):
        n = BLOCK * dil
        c_ref[0:lw, :] = k_ref[g, 2 * SUPER - n:2 * SUPER, :].T
        c_ref[lw:2 * lw, :] = v_ref[g, 2 * SUPER - n:2 * SUPER, :].T

    k_ref[:, 0:SUPER, :] = k_ref[:, SUPER:2 * SUPER, :]
    v_ref[:, 0:SUPER, :] = v_ref[:, SUPER:2 * SUPER, :]


def _mix_d(pd, cos, sin):
    b, s, _ = pd.shape
    est = (2 * SUPER * COLS_D * 2 + 4 * SUPER * BLOCK * 4 + 2 * SUPER * MIX * 2
           + 2 * KV_CH * (BLOCK + 4 * BLOCK + SUPER) * 4 + (3 + 2 * 2) * SUPER * MIX * 4)
    tile = lambda bi, si: (bi, si, 0)
    cache = lambda bi, si: (bi, 0, 0)
    return pl.pallas_call(
        _mix_d_body,
        grid=(b, s // SUPER),
        in_specs=[pl.BlockSpec((None, SUPER, COLS_D), tile),
                  pl.BlockSpec((SUPER, BLOCK), lambda bi, si: (si, 0)),
                  pl.BlockSpec((SUPER, BLOCK), lambda bi, si: (si, 0))],
        out_specs=[pl.BlockSpec((None, SUPER, MIX), tile)]
                  + [pl.BlockSpec((None, KV_CH, BLOCK * d), cache) for d in DILATIONS],
        out_shape=[jax.ShapeDtypeStruct((b, s, MIX), BF16)]
                  + [jax.ShapeDtypeStruct((b, KV_CH, BLOCK * d), F32) for d in DILATIONS],
        scratch_shapes=[pltpu.VMEM((3, SUPER, 2 * HEAD_DIM), F32),
                        pltpu.VMEM((3, 2 * SUPER, 2 * HEAD_DIM), F32),
                        pltpu.VMEM((3, 2 * SUPER, 2 * HEAD_DIM), F32),
                        pltpu.VMEM((3, SUPER, 2 * HEAD_DIM), F32),
                        pltpu.VMEM((3, SUPER, 2 * HEAD_DIM), F32)],
        compiler_params=pltpu.CompilerParams(dimension_semantics=("arbitrary", "arbitrary"),
                                             vmem_limit_bytes=_vmem_limit(est)),
        name="mix_d",
    )(pd, cos, sin)


def _shift_pass(c_ref, o_ref, i, ch0, new_b, per_tile):
    n_tiles = c_ref.shape[2] // BLOCK
    chans = pl.ds(ch0, 2 * HEAD_DIM)
    lane = lax.broadcasted_iota(jnp.int32, (2 * HEAD_DIM, BLOCK), 1)
    prev = None
    for t in range(n_tiles):
        x = c_ref[i, chans, t * BLOCK:(t + 1) * BLOCK]
        per_tile(t, x)
        r = pltpu.roll(x, BLOCK - 1, 1)
        if t > 0:
            o_ref[i, chans, (t - 1) * BLOCK:t * BLOCK] = jnp.where(lane < BLOCK - 1, prev, r)
        prev = r
    o_ref[i, chans, (n_tiles - 1) * BLOCK:n_tiles * BLOCK] = jnp.where(lane < BLOCK - 1, prev, new_b)


def _decode_cache(c_ref, o_ref, i, heads, k_new, v_new, dil):
    hd = HEAD_DIM
    n = c_ref.shape[2]
    n_tiles = n // BLOCK
    s_tiles = [[None] * (n_tiles + 1) for _ in heads]

    def scores(t, x):
        for qi, (g, q_b, _) in enumerate(heads):
            s_tiles[qi][t] = jnp.sum(x[g * hd:(g + 1) * hd, :] * q_b, axis=0, keepdims=True)

    _shift_pass(c_ref, o_ref, i, 0, k_new, scores)
    scores(n_tiles, k_new)

    lane = lax.broadcasted_iota(jnp.int32, (1, n + BLOCK), 1)
    valid = jnp.logical_or(jnp.logical_and(lane < n, (lane & (dil - 1)) == 0), lane == n + BLOCK - 1)
    probs, dens, lses = [], [], []
    for qi, (g, q_b, sink) in enumerate(heads):
        s = jnp.where(valid, jnp.concatenate(s_tiles[qi], axis=1), NEG_INF)
        m = jnp.max(s, axis=1, keepdims=True)
        if sink is not None:
            m = jnp.maximum(m, sink)
        p = jnp.exp(s - m)
        den = jnp.sum(p, axis=1, keepdims=True)
        if sink is not None:
            den = den + jnp.exp(sink - m)
        probs.append(p)
        dens.append(den)
        lses.append(m + jnp.log(den))

    accs = [jnp.zeros((hd, BLOCK), F32) for _ in heads]

    def weighted(t, x):
        for qi, (g, _, _) in enumerate(heads):
            accs[qi] = accs[qi] + x[g * hd:(g + 1) * hd, :] * probs[qi][:, t * BLOCK:(t + 1) * BLOCK]

    _shift_pass(c_ref, o_ref, i, 2 * hd, v_new, weighted)
    weighted(n_tiles, v_new)

    outs = [jnp.sum(accs[qi], axis=1, keepdims=True) / dens[qi] for qi in range(len(heads))]
    return outs, lses


XT_KB, XT_VB, XT_KD, XT_VD = 0, BLOCK, 2 * BLOCK, 2 * BLOCK + MIX
XT_ROWS = 2 * BLOCK + 2 * MIX


SEQ_PER_STEP = 2


def _decode_prep_body(pa_ref, pb_ref, pc_ref, pd_ref, st_ref, cw_ref, lg_ref, lb_ref,
                      wd_ref, bd_ref, cos_ref, sin_ref,
                      ya_ref, yc_ref, nst_ref, vn_ref, xq_ref, xt_ref):
    a_b = pa_ref[:, 0:MIX]
    z = pa_ref[:, MIX:2 * MIX] * pa_ref[:, 2 * MIX:3 * MIX]
    prev0 = st_ref[:, 0:MIX]
    prev1 = st_ref[:, MIX:2 * MIX]
    conv = cw_ref[0:1, :] * prev0 + cw_ref[1:2, :] * prev1 + cw_ref[2:3, :] * z
    ya_ref[...] = a_b * conv * _silu(pa_ref[:, 3 * MIX:4 * MIX])
    nst_ref[:, 0:MIX] = prev1
    nst_ref[:, MIX:2 * MIX] = z

    c_v = pc_ref[:, MIX:2 * MIX]
    mu = jnp.mean(c_v, axis=-1, keepdims=True)
    cen = c_v - mu
    var = jnp.mean(cen * cen, axis=-1, keepdims=True)
    vn = cen * lax.rsqrt(var + EPS) * lg_ref[...] + lb_ref[...]
    vn_ref[...] = vn
    mixed = wd_ref[...] * vn + bd_ref[...]
    yc_ref[...] = pc_ref[:, 0:MIX] * mixed * _silu(pc_ref[:, 2 * MIX:3 * MIX])

    cos = cos_ref[...]
    sin = sin_ref[...]
    cos3, sin3 = _tile_lanes(cos, 3), _tile_lanes(sin, 3)
    xq_ref[0:MIX, :] = (_rope(pb_ref[:, 0:MIX], cos3, sin3) * SCALE).T.astype(BF16)
    xq_ref[MIX:2 * MIX, :] = (_rope(pd_ref[:, 0:MIX], cos3, sin3) * SCALE).T.astype(BF16)
    xt_ref[XT_KB:XT_VB, :] = _rope(pb_ref[:, MIX:MIX + BLOCK], cos, sin).T
    xt_ref[XT_VB:XT_KD, :] = pb_ref[:, MIX + BLOCK:MIX + 2 * BLOCK].T
    xt_ref[XT_KD:XT_VD, :] = _rope(pd_ref[:, MIX:2 * MIX], cos3, sin3).T
    xt_ref[XT_VD:XT_ROWS, :] = pd_ref[:, 2 * MIX:3 * MIX].T


def _decode_prep(pa, pb, pc, pd, state, conv_w, ln_g, ln_b, w_diag, b_diag, cos, sin):
    nb = pa.shape[0]
    assert nb == BLOCK
    shapes = [((nb, MIX), F32), ((nb, MIX), F32), ((nb, 2 * MIX), F32), ((nb, MIX), F32),
              ((2 * MIX, nb), BF16), ((XT_ROWS, nb), F32)]
    return pl.pallas_call(
        _decode_prep_body,
        out_shape=[jax.ShapeDtypeStruct(s, d) for s, d in shapes],
        name="decode_prep",
    )(pa, pb, pc, pd, state, conv_w, ln_g, ln_b, w_diag, b_diag, cos, sin)


def _decode_sequences(dec, sink_ref, xq_ref, xt_ref, cache_refs, out_refs):
    cs_ref, c1_ref, c4_ref, c16_ref = cache_refs
    ns_ref, n1_ref, n4_ref, n16_ref, yt_ref = out_refs
    step = pl.program_id(0)
    n_seq = xq_ref.shape[1]
    hd = HEAD_DIM

    @pl.when(step == 0)
    def _():
        yt_ref[...] = jnp.zeros_like(yt_ref)

    for i in range(SEQ_PER_STEP):
        b = dec.seq_base + step * SEQ_PER_STEP + i
        onehot = (lax.broadcasted_iota(jnp.int32, (n_seq, BLOCK), 0) == b).astype(BF16)
        q_b = _dot(xq_ref[...], onehot)
        new_rot = pltpu.roll(xt_ref[...], BLOCK - 1 - b, 1)
        heads = [(h // SWA_REP, q_b[h * hd:(h + 1) * hd], jnp.full((1, 1), sink_ref[h], F32))
                 for h in range(SWA_Q_HEADS)]
        outs, _ = _decode_cache(cs_ref, ns_ref, i, heads, new_rot[XT_KB:XT_VB], new_rot[XT_VB:XT_KD], 1)
        yb_col = jnp.concatenate(outs, axis=0)

        q_col = q_b[MIX:2 * MIX]
        k_new = new_rot[XT_KD:XT_VD]
        v_new = new_rot[XT_VD:XT_ROWS]
        o_heads, l_heads = [], []
        for g, (c_ref, n_ref, dil) in enumerate(zip((c1_ref, c4_ref, c16_ref),
                                                    (n1_ref, n4_ref, n16_ref), DILATIONS)):
            gc = slice(2 * g * hd, 2 * (g + 1) * hd)
            heads = [(hh, q_col[(2 * g + hh) * hd:(2 * g + hh + 1) * hd], None) for hh in range(2)]
            o, lse = _decode_cache(c_ref, n_ref, i, heads, k_new[gc], v_new[gc], dil)
            o_heads += o
            l_heads += lse
        cols = []
        for hh in range(2):
            ls = [l_heads[2 * g + hh] for g in range(3)]
            lmax = jnp.maximum(jnp.maximum(ls[0], ls[1]), ls[2])
            es = [jnp.exp(l - lmax) for l in ls]
            esum = es[0] + es[1] + es[2]
            cols.append([o_heads[2 * g + hh] * (es[g] / esum) for g in range(3)])
        yd_col = jnp.concatenate([cols[hh][g] for g in range(3) for hh in range(2)], axis=0)
        y_col = jnp.concatenate([yb_col, yd_col], axis=0)
        lane_y = lax.broadcasted_iota(jnp.int32, (2 * MIX, n_seq), 1)
        yt_ref[...] = jnp.where(lane_y == b, y_col, yt_ref[...])


def _decode_fin_body(yt0_ref, yt1_ref, pb_ref, pd_ref, yb_ref, yd_ref, *, split):
    lane = lax.broadcasted_iota(jnp.int32, yt0_ref.shape, 1)
    yt = jnp.where(lane < split, yt0_ref[...], yt1_ref[...])
    yb_ref[...] = yt[0:MIX, :].T * _silu(pb_ref[:, MIX + 2 * BLOCK:COLS_B])
    yd_ref[...] = yt[MIX:2 * MIX, :].T * _silu(pd_ref[:, 3 * MIX:4 * MIX])


def _decode_fin(yt0, yt1, pb, pd, split):
    nb = pb.shape[0]
    return pl.pallas_call(
        functools.partial(_decode_fin_body, split=split),
        out_shape=[jax.ShapeDtypeStruct((nb, MIX), F32)] * 2,
        name="decode_fin",
    )(yt0, yt1, pb, pd)


class _DecodeStatic(NamedTuple):
    seq_base: int
    n_alias: int


class _Decode:
    def __init__(self, layer, seq_base, steps, sinks, xq, xt, caches, prev_outs):
        self.layer, self.seq_base, self.steps = layer, seq_base, steps
        self.sinks, self.xq, self.xt = sinks, xq, xt
        self.caches, self.prev_outs = tuple(caches), tuple(prev_outs)
        self.static = _DecodeStatic(seq_base, len(self.prev_outs))

    def extend(self, in_specs, out_specs, out_shape, args):
        layer, block0 = self.layer, self.seq_base // SEQ_PER_STEP
        spec = lambda c: pl.BlockSpec((None, SEQ_PER_STEP) + c.shape[2:],
                                      lambda i: (layer, block0 + i, 0, 0))
        first_alias_in = 1 + len(args) + 2 + len(self.caches)
        aliases = {first_alias_in + i: len(out_specs) + i for i in range(len(self.prev_outs))}
        in_specs = ([pl.BlockSpec(memory_space=pltpu.SMEM)] + in_specs
                    + [_const_spec(self.xq.shape), _const_spec(self.xt.shape)]
                    + [spec(c) for c in self.caches]
                    + [pl.BlockSpec(memory_space=pl.ANY) for _ in self.prev_outs])
        args = [self.sinks] + args + [self.xq, self.xt, *self.caches, *self.prev_outs]
        yt_shape = (2 * MIX, self.xq.shape[1])
        out_specs = out_specs + [spec(c) for c in self.caches] + [_const_spec(yt_shape)]
        out_shape = (out_shape + [jax.ShapeDtypeStruct(c.shape, F32) for c in self.caches]
                     + [jax.ShapeDtypeStruct(yt_shape, F32)])
        rows = sum(c.shape[3] for c in self.caches)
        extra = 4 * SEQ_PER_STEP * KV_CH * rows * 4 + 4 * (2 * MIX + XT_ROWS) * BLOCK * 4
        return in_specs, out_specs, out_shape, args, aliases, extra


def _rope_tables(pos):
    inv_freq = ROPE_THETA ** (-jnp.arange(HALF_HEAD, dtype=F32) / HALF_HEAD)
    ang = pos.astype(F32)[:, None] * inv_freq[None, :]
    cos, sin = jnp.cos(ang), jnp.sin(ang)
    return (jnp.concatenate([cos, cos, cos, cos], axis=1),
            jnp.concatenate([-sin, sin, -sin, sin], axis=1))


def _cache_view(c):
    d, b, n = c.shape[:3]
    return jnp.transpose(c, (0, 1, 3, 4, 5, 2)).reshape(d, b, KV_CH, n)


def _cache_unview(c):
    lead, n = c.shape[:-2], c.shape[-1]
    nl = len(lead)
    perm = tuple(range(nl)) + (nl + 3, nl, nl + 1, nl + 2)
    return jnp.transpose(c.reshape(lead + (2, 2, HEAD_DIM, n)), perm)


def kernel(x_prompt, x_sample, state_conv, cache_swa_kv, cache_dil1_kv, cache_dil4_kv, cache_dil16_kv,
           norm_g, w_in, conv_w, attn_sinks, v_ln_g, v_ln_b, w_spatial, b_spatial,
           w_branch, w_merge, w_out, final_norm_g):
    depth = w_in.shape[0]
    nb, seq, _ = x_prompt.shape
    ns, dec_seq, _ = x_sample.shape
    assert dec_seq == 1 and seq % SUPER == 0
    past_len = 16384
    assert cache_swa_kv.shape[2] == BLOCK and cache_dil16_kv.shape[2] == SUPER

    cos_p, sin_p = _rope_tables(jnp.arange(seq, dtype=jnp.int32))
    cos_s, sin_s = _rope_tables(past_len + jnp.arange(1, dtype=jnp.int32))
    w_in_bf = w_in.astype(BF16)
    w_merge_bf = w_merge.astype(BF16)
    w_branch_bf = w_branch.astype(BF16)
    w_out_bf = w_out.astype(BF16)
    caches = [_cache_view(c) for c in (cache_swa_kv, cache_dil1_kv, cache_dil4_kv, cache_dil16_kv)]
    fg = final_norm_g.reshape(1, D_MODEL)

    xp = x_prompt.reshape(nb * seq, D_MODEL)
    xs = x_sample.reshape(ns, D_MODEL)
    conv_p, conv_s, chunk_v = [], [], []
    new_p = [[] for _ in range(4)]
    new_s = ()
    for l in range(depth):
        g = norm_g[l].reshape(1, D_MODEL)
        final = l == depth - 1
        ln_g = v_ln_g[l].reshape(1, MIX)
        ln_b = v_ln_b[l].reshape(1, MIX)

        sa, sb, sc, sd = _inproj(xs, g, w_in_bf[l], ns, F32)
        w_diag = jnp.repeat(w_spatial[l][:, 0, 0], HEAD_DIM).reshape(1, MIX)
        b_diag = jnp.repeat(b_spatial[l][:, 0], HEAD_DIM).reshape(1, MIX)
        za, zc, nstate, vn, xq, xt = _decode_prep(sa, sb, sc, sd, state_conv[l].reshape(ns, 2 * MIX),
                                                  conv_w[l], ln_g, ln_b, w_diag, b_diag, cos_s, sin_s)

        tm = 512
        steps = nb * seq // tm
        half = steps * SEQ_PER_STEP
        assert 2 * half == ns
        dec = _Decode(l, 0, steps, attn_sinks[l], xq, xt, caches, new_s)
        pa, pb, pc, pd, *new_s, yt0 = _inproj(xp, g, w_in_bf[l], tm, BF16, dec)
        shp = lambda a: a.reshape(nb, seq, a.shape[-1])
        b_tile = jnp.repeat(b_spatial[l].T, HEAD_DIM, axis=1)
        ya, yc, nconv = _mix_ac(shp(pa), shp(pc), conv_w[l], ln_g, ln_b, w_spatial[l], b_tile, 512)
        yb, kv_swa = _mix_b(shp(pb), attn_sinks[l], cos_p, sin_p, 1024)
        yd, kv1, kv4, kv16 = _mix_d(shp(pd), cos_p, sin_p)
        flat = lambda a: a.reshape(nb * seq, MIX)
        dec = _Decode(l, half, steps, attn_sinks[l], xq, xt, caches, new_s)
        xp, *new_s, yt1 = _outproj(xp, (flat(ya), flat(yb), flat(yc), flat(yd)), g, w_merge_bf[l],
                                   w_branch_bf[l], w_out_bf[l], fg, tm, final, dec)
        conv_p.append(nconv)
        new_p[0].append(_cache_unview(kv_swa))
        for i, kv in enumerate((kv1, kv4, kv16)):
            new_p[i + 1].append(_cache_unview(kv))

        zb, zd = _decode_fin(yt0, yt1, sb, sd, half)
        xs, = _outproj(xs, (za, zb, zc, zd), g, w_merge_bf[l], w_branch_bf[l], w_out_bf[l], fg, ns, final)
        conv_s.append(nstate.reshape(ns, 2, MIX))
        chunk_v.append(vn.reshape(ns, 1, MIX))

    st = jnp.stack
    new_s = [_cache_unview(c) for c in new_s]
    return (xp.reshape(nb, seq, D_MODEL), xs.reshape(ns, 1, D_MODEL),
            st(conv_p), st(conv_s),
            st(new_p[0]), new_s[0], st(new_p[1]), new_s[1],
            st(new_p[2]), new_s[2], st(new_p[3]), new_s[3],
            st(chunk_v))
```

```python
import functools
from typing import NamedTuple

import jax
import jax.numpy as jnp
from jax import lax
from jax.experimental import pallas as pl
from jax.experimental.pallas import tpu as pltpu

F32 = jnp.float32
BF16 = jnp.bfloat16

D_MODEL = 1024
HEAD_DIM = 64
HALF_HEAD = HEAD_DIM // 2
MIX = 384
BLOCK = 128
SWA_Q_HEADS = 6
SWA_KV_HEADS = 2
SWA_REP = SWA_Q_HEADS // SWA_KV_HEADS
DILATIONS = (1, 4, 16)
SUPER = BLOCK * DILATIONS[-1]
BLOCKS_PER_TRIP = 4
ROW_CHUNK = 256
ROPE_THETA = 10000.0
EPS = 1e-6
NEG_INF = -1e30
SCALE = HEAD_DIM ** -0.5

COLS_A = 4 * MIX
COLS_B = MIX + 2 * BLOCK + MIX
COLS_C = 3 * MIX
COLS_D = 4 * MIX
OFF_A, OFF_B, OFF_C, OFF_D = 0, COLS_A, COLS_A + COLS_B, COLS_A + COLS_B + COLS_C
IN_COLS = OFF_D + COLS_D
KV_CH = 4 * HEAD_DIM

V7X_VMEM_BYTES = 64 * 1024 * 1024
MATMUL_COL_CHUNK = 512


def _vmem_limit(estimate_bytes):
    return int(min(V7X_VMEM_BYTES - 8 * 1024 * 1024, estimate_bytes + 16 * 1024 * 1024))


def _const_spec(shape, single=False):
    nd = len(shape)
    if single:
        return pl.BlockSpec(shape, lambda *_: (0,) * nd, pipeline_mode=pl.Buffered(1))
    return pl.BlockSpec(shape, lambda *_: (0,) * nd)


def _layer_spec(stacked_shape, layer):
    nd = len(stacked_shape) - 1
    return pl.BlockSpec((None,) + tuple(stacked_shape[1:]), lambda *_: (layer,) + (0,) * nd,
                        pipeline_mode=pl.Buffered(1))


def _rms(x, g):
    ms = jnp.mean(x * x, axis=-1, keepdims=True)
    return x * lax.rsqrt(ms + EPS) * g


def _sigmoid(x):
    return 0.5 * jnp.tanh(0.5 * x) + 0.5


def _silu(x):
    h = 0.5 * x
    return h * jnp.tanh(h) + h


def _rope(x, cos, sin):
    w = x.shape[-1]
    lane = lax.broadcasted_iota(jnp.int32, x.shape, 1)
    fwd = pltpu.roll(x, w - HALF_HEAD, 1)
    bwd = pltpu.roll(x, HALF_HEAD, 1)
    partner = jnp.where((lane & HALF_HEAD) == 0, fwd, bwd)
    return x * cos + partner * sin


def _tile_lanes(t, reps):
    return t if reps == 1 else jnp.concatenate([t] * reps, axis=1)


def _dot(a, b):
    return jnp.dot(a, b, preferred_element_type=F32)


def _dot_nt(a, b):
    return lax.dot_general(a, b, (((1,), (1,)), ((), ())), preferred_element_type=F32)


def _inproj_body(*refs, dec):
    if dec is None:
        x_ref, g_ref, w_ref = refs[:3]
        outs = refs[3:]
    else:
        sink_ref, x_ref, g_ref, w_ref, xq_ref, xt_ref = refs[:6]
        outs = refs[10 + dec.n_alias:10 + dec.n_alias + 4]
        _decode_sequences(dec, sink_ref, xq_ref, xt_ref, refs[6:10], refs[14 + dec.n_alias:])
    h = _rms(x_ref[...], g_ref[...]).astype(BF16)
    for o_ref, off in zip(outs, (OFF_A, OFF_B, OFF_C, OFF_D)):
        width = o_ref.shape[-1]
        for j in range(0, width, MATMUL_COL_CHUNK):
            cw = min(MATMUL_COL_CHUNK, width - j)
            o_ref[:, j:j + cw] = _dot(h, w_ref[:, off + j:off + j + cw]).astype(o_ref.dtype)


def _inproj(x, g, w_bf16, layer, tm, out_dtype, dec=None):
    t = x.shape[0]
    widths = (COLS_A, COLS_B, COLS_C, COLS_D)
    osize = jnp.dtype(out_dtype).itemsize
    est = 2 * tm * D_MODEL * 4 + D_MODEL * IN_COLS * 2 + 2 * tm * IN_COLS * osize
    row = lambda i: (i, 0)
    in_specs = [pl.BlockSpec((tm, D_MODEL), row), _const_spec((1, D_MODEL)),
                _layer_spec(w_bf16.shape, layer)]
    out_specs = [pl.BlockSpec((tm, wd), row) for wd in widths]
    out_shape = [jax.ShapeDtypeStruct((t, wd), out_dtype) for wd in widths]
    args = [x, g, w_bf16]
    aliases = {}
    if dec is not None:
        assert t // tm == dec.steps
        in_specs, out_specs, out_shape, args, aliases, extra = dec.extend(in_specs, out_specs, out_shape, args)
        est += extra
    return pl.pallas_call(
        functools.partial(_inproj_body, dec=None if dec is None else dec.static),
        grid=(t // tm,),
        in_specs=in_specs, out_specs=out_specs, out_shape=out_shape,
        input_output_aliases=aliases,
        compiler_params=pltpu.CompilerParams(
            dimension_semantics=("parallel",) if dec is None else ("arbitrary",),
            vmem_limit_bytes=_vmem_limit(est)),
        name="inproj" if dec is None else "inproj_dec",
    )(*args)


def _outproj_body(*refs, final, dec):
    if dec is None:
        (x_ref, ya_ref, yb_ref, yc_ref, yd_ref, g_ref, wm_ref, wb_ref, wo_ref, fg_ref, o_ref) = refs
    else:
        sink_ref = refs[0]
        (x_ref, ya_ref, yb_ref, yc_ref, yd_ref, g_ref, wm_ref, wb_ref, wo_ref, fg_ref) = refs[1:11]
        xq_ref, xt_ref = refs[11:13]
        o_ref = refs[17 + dec.n_alias]
        _decode_sequences(dec, sink_ref, xq_ref, xt_ref, refs[13:17], refs[18 + dec.n_alias:])
    x = x_ref[...]
    h = _rms(x, g_ref[...]).astype(BF16)
    merged = None
    for n, y_ref in enumerate((ya_ref, yb_ref, yc_ref, yd_ref)):
        gate = _sigmoid(_dot(h, wm_ref[:, n * D_MODEL:(n + 1) * D_MODEL]))
        term = gate * _dot(y_ref[...].astype(BF16), wb_ref[n])
        merged = term if merged is None else merged + term
    out = x + _dot(merged.astype(BF16), wo_ref[...])
    if final:
        out = _rms(out, fg_ref[...])
    o_ref[...] = out


def _outproj(x, ys, g, wm, wb, wo, fg, layer, tm, final, dec=None):
    t = x.shape[0]
    ysize = jnp.dtype(ys[0].dtype).itemsize
    est = (4 * tm * D_MODEL * 4 + 2 * 4 * tm * MIX * ysize
           + (4 * D_MODEL * D_MODEL + 4 * MIX * D_MODEL + D_MODEL * D_MODEL) * 2
           + 4 * tm * D_MODEL * 4)
    row = lambda i: (i, 0)
    in_specs = ([pl.BlockSpec((tm, D_MODEL), row)]
                + [pl.BlockSpec((tm, MIX), row)] * 4
                + [_const_spec((1, D_MODEL)),
                   _layer_spec(wm.shape, layer), _layer_spec(wb.shape, layer), _layer_spec(wo.shape, layer),
                   _const_spec((1, D_MODEL))])
    out_specs = [pl.BlockSpec((tm, D_MODEL), row)]
    out_shape = [jax.ShapeDtypeStruct((t, D_MODEL), F32)]
    args = [x, *ys, g, wm, wb, wo, fg]
    aliases = {}
    if dec is not None:
        assert t // tm == dec.steps
        in_specs, out_specs, out_shape, args, aliases, extra = dec.extend(in_specs, out_specs, out_shape, args)
        est += extra
    return pl.pallas_call(
        functools.partial(_outproj_body, final=final, dec=None if dec is None else dec.static),
        grid=(t // tm,),
        in_specs=in_specs, out_specs=out_specs, out_shape=out_shape,
        input_output_aliases=aliases,
        compiler_params=pltpu.CompilerParams(
            dimension_semantics=("parallel",) if dec is None else ("arbitrary",),
            vmem_limit_bytes=_vmem_limit(est)),
        name="outproj" if dec is None else "outproj_dec",
    )(*args)


def _mix_ac_body(pa_ref, pc_ref, cw_ref, lg_ref, lb_ref, ws_ref, bs_ref,
                 ya_ref, yc_ref, nc_ref, tail_ref):
    ts = pa_ref.shape[0]

    @pl.when(pl.program_id(1) == 0)
    def _():
        tail_ref[...] = jnp.zeros_like(tail_ref)

    r = lax.broadcasted_iota(jnp.int32, (BLOCK, BLOCK), 0)
    c = lax.broadcasted_iota(jnp.int32, (BLOCK, BLOCK), 1)
    n_groups = MIX // HEAD_DIM
    w_causal = [jnp.where(r >= c, ws_ref[g], 0.0).astype(BF16) for g in range(n_groups)]
    row = lax.broadcasted_iota(jnp.int32, (BLOCK, MIX), 0)
    prev1 = tail_ref[1:2, :]
    prev2 = tail_ref[0:1, :]
    for ck in range(ts // BLOCK):
        rows = slice(ck * BLOCK, (ck + 1) * BLOCK)
        a_b = pa_ref[rows, 0:MIX].astype(F32)
        z = pa_ref[rows, MIX:2 * MIX].astype(F32) * pa_ref[rows, 2 * MIX:3 * MIX].astype(F32)
        a_gate = pa_ref[rows, 3 * MIX:4 * MIX].astype(F32)
        z1 = jnp.where(row == 0, prev1, pltpu.roll(z, 1, 0))
        z2 = jnp.where(row == 0, prev2, jnp.where(row == 1, prev1, pltpu.roll(z, 2, 0)))
        conv = cw_ref[0:1, :] * z2 + cw_ref[1:2, :] * z1 + cw_ref[2:3, :] * z
        ya_ref[rows, :] = (a_b * conv * _silu(a_gate)).astype(ya_ref.dtype)
        prev2 = z[BLOCK - 2:BLOCK - 1, :]
        prev1 = z[BLOCK - 1:BLOCK, :]

        c_v = pc_ref[rows, MIX:2 * MIX].astype(F32)
        mu = jnp.mean(c_v, axis=-1, keepdims=True)
        cen = c_v - mu
        var = jnp.mean(cen * cen, axis=-1, keepdims=True)
        vn_c = (cen * lax.rsqrt(var + EPS) * lg_ref[...] + lb_ref[...]).astype(BF16)
        mixed = jnp.concatenate(
            [_dot(w_causal[g], vn_c[:, g * HEAD_DIM:(g + 1) * HEAD_DIM]) for g in range(n_groups)],
            axis=1) + bs_ref[...]
        c_u = pc_ref[rows, 0:MIX].astype(F32)
        c_gate = pc_ref[rows, 2 * MIX:3 * MIX].astype(F32)
        yc_ref[rows, :] = (c_u * mixed * _silu(c_gate)).astype(yc_ref.dtype)
    last2 = jnp.concatenate([prev2, prev1], axis=0)
    tail_ref[...] = last2
    nc_ref[...] = last2


def _mix_ac(pa, pc, conv_w, ln_g, ln_b, w_s, b_tile, ts):
    b, s, _ = pa.shape
    est = 2 * ts * (COLS_A + COLS_C) * 2 + 4 * ts * MIX * 2 + 2 * 6 * BLOCK * BLOCK * 4 + 6 * ts * MIX * 4
    tile = lambda bi, si: (bi, si, 0)
    return pl.pallas_call(
        _mix_ac_body,
        grid=(b, s // ts),
        in_specs=[pl.BlockSpec((None, ts, COLS_A), tile),
                  pl.BlockSpec((None, ts, COLS_C), tile),
                  _const_spec((3, MIX)), _const_spec((1, MIX)), _const_spec((1, MIX)),
                  _const_spec((MIX // HEAD_DIM, BLOCK, BLOCK)), _const_spec((BLOCK, MIX))],
        out_specs=[pl.BlockSpec((None, ts, MIX), tile),
                   pl.BlockSpec((None, ts, MIX), tile),
                   pl.BlockSpec((None, 2, MIX), lambda bi, si: (bi, 0, 0))],
        out_shape=[jax.ShapeDtypeStruct((b, s, MIX), BF16),
                   jax.ShapeDtypeStruct((b, s, MIX), BF16),
                   jax.ShapeDtypeStruct((b, 2, MIX), F32)],
        scratch_shapes=[pltpu.VMEM((2, MIX), F32)],
        compiler_params=pltpu.CompilerParams(dimension_semantics=("arbitrary", "arbitrary"),
                                             vmem_limit_bytes=_vmem_limit(est)),
        name="mix_ac",
    )(pa, pc, conv_w, ln_g, ln_b, w_s, b_tile)


def _band_mask(first_block):
    i = lax.broadcasted_iota(jnp.int32, (BLOCK, 2 * BLOCK), 0)
    j = lax.broadcasted_iota(jnp.int32, (BLOCK, 2 * BLOCK), 1)
    band = (j >= i) & (j <= i + BLOCK)
    return band & jnp.logical_or(jnp.logical_not(first_block), j >= BLOCK)


def _mix_b_body(sink_ref, pb_ref, cos_ref, sin_ref, yb_ref, kv_ref, q_s, k_s, v_s, ksw_s, vsw_s):
    blk = pl.program_id(1)
    ts = pb_ref.shape[0]
    hist = (k_s, v_s, ksw_s, vsw_s)

    @pl.when(blk == 0)
    def _():
        for r in hist:
            r[0:BLOCK, :] = jnp.zeros((BLOCK, BLOCK), BF16)

    cos = cos_ref[...]
    sin = sin_ref[...]
    q_s[...] = (_rope(pb_ref[:, 0:MIX].astype(F32), _tile_lanes(cos, 3), _tile_lanes(sin, 3)) * SCALE
                ).astype(BF16)
    k = _rope(pb_ref[:, MIX:MIX + BLOCK].astype(F32), cos, sin)
    v = pb_ref[:, MIX + BLOCK:MIX + 2 * BLOCK].astype(F32)
    k_s[BLOCK:BLOCK + ts, :] = k.astype(BF16)
    v_s[BLOCK:BLOCK + ts, :] = v.astype(BF16)
    ksw_s[BLOCK:BLOCK + ts, :] = pltpu.roll(k, HEAD_DIM, 1).astype(BF16)
    vsw_s[BLOCK:BLOCK + ts, :] = pltpu.roll(v, HEAD_DIM, 1).astype(BF16)

    @pl.when(blk == pl.num_programs(1) - 1)
    def _():
        kv_ref[0:BLOCK, :] = k[ts - BLOCK:ts, :].T
        kv_ref[BLOCK:2 * BLOCK, :] = v[ts - BLOCK:ts, :].T

    half0 = lax.broadcasted_iota(jnp.int32, (BLOCK, BLOCK), 1) < HEAD_DIM

    def block(c, carry):
        r0 = pl.multiple_of(c * BLOCK, BLOCK)
        rows = pl.ds(r0, BLOCK)
        win = pl.ds(r0, 2 * BLOCK)
        kk, vv, kk_sw, vv_sw = k_s[win, :], v_s[win, :], ksw_s[win, :], vsw_s[win, :]
        mask = _band_mask(jnp.logical_and(blk == 0, c == 0))
        outs = []
        for h in range(SWA_Q_HEADS):
            g, chunk, half = h // SWA_REP, h // 2, h % 2
            qc = q_s[rows, chunk * BLOCK:(chunk + 1) * BLOCK]
            q_h = jnp.where(half0 if half == 0 else jnp.logical_not(half0), qc, jnp.zeros_like(qc))
            s = jnp.where(mask, _dot_nt(q_h, kk if half == g else kk_sw), NEG_INF)
            sink = sink_ref[h]
            m = jnp.maximum(jnp.max(s, axis=-1, keepdims=True), sink)
            p = jnp.exp(s - m)
            den = jnp.sum(p, axis=-1, keepdims=True) + jnp.exp(sink - m)
            outs.append(_dot(p.astype(BF16), vv if half == g else vv_sw) / den)
        y = jnp.concatenate([jnp.where(half0, outs[2 * j], outs[2 * j + 1]) for j in range(3)], axis=1)
        gate = pb_ref[rows, MIX + 2 * BLOCK:COLS_B].astype(F32)
        yb_ref[rows, :] = (y * _silu(gate)).astype(yb_ref.dtype)
        return carry

    lax.fori_loop(0, ts // BLOCK, block, 0, unroll=2)
    for r in hist:
        r[0:BLOCK, :] = r[ts:ts + BLOCK, :]


def _mix_b(pb, sinks, cos, sin, ts):
    b, s, _ = pb.shape
    est = 2 * ts * COLS_B * 2 + 4 * ts * BLOCK * 4 + 2 * ts * MIX * 2 + 12 * ts * MIX * 4
    tile = lambda bi, si: (bi, si, 0)
    return pl.pallas_call(
        _mix_b_body,
        grid=(b, s // ts),
        in_specs=[pl.BlockSpec(memory_space=pltpu.SMEM),
                  pl.BlockSpec((None, ts, COLS_B), tile),
                  pl.BlockSpec((ts, BLOCK), lambda bi, si: (si, 0)),
                  pl.BlockSpec((ts, BLOCK), lambda bi, si: (si, 0))],
        out_specs=[pl.BlockSpec((None, ts, MIX), tile),
                   pl.BlockSpec((None, 2 * BLOCK, BLOCK), lambda bi, si: (bi, 0, 0))],
        out_shape=[jax.ShapeDtypeStruct((b, s, MIX), BF16),
                   jax.ShapeDtypeStruct((b, 2 * BLOCK, BLOCK), F32)],
        scratch_shapes=[pltpu.VMEM((ts, MIX), BF16)] + [pltpu.VMEM((BLOCK + ts, BLOCK), BF16)] * 4,
        compiler_params=pltpu.CompilerParams(dimension_semantics=("arbitrary", "arbitrary"),
                                             vmem_limit_bytes=_vmem_limit(est)),
        name="mix_b",
    )(sinks, pb, cos, sin)


def _mix_d_body(pd_ref, cos_ref, sin_ref, yd_ref, c1_ref, c4_ref, c16_ref,
                q_ref, k_ref, v_ref, o_ref, l_ref):
    sb = pl.program_id(1)

    lw = 2 * HEAD_DIM
    n_groups = len(DILATIONS)

    @pl.when(sb == 0)
    def _():
        k_ref[:, 0:SUPER, :] = jnp.zeros((n_groups, SUPER, lw), F32)
        v_ref[:, 0:SUPER, :] = jnp.zeros((n_groups, SUPER, lw), F32)

    def prepare(c, carry):
        r0 = pl.multiple_of(c * ROW_CHUNK, ROW_CHUNK)
        rows = pl.ds(r0, ROW_CHUNK)
        new_rows = pl.ds(SUPER + r0, ROW_CHUNK)
        cos = cos_ref[rows, :]
        sin = sin_ref[rows, :]
        for g in range(n_groups):
            q_ref[g, rows, :] = _rope(pd_ref[rows, g * lw:(g + 1) * lw].astype(F32), cos, sin) * SCALE
            k_ref[g, new_rows, :] = _rope(pd_ref[rows, MIX + g * lw:MIX + (g + 1) * lw].astype(F32), cos, sin)
            v_ref[g, new_rows, :] = pd_ref[rows, 2 * MIX + g * lw:2 * MIX + (g + 1) * lw].astype(F32)
        return carry

    lax.fori_loop(0, SUPER // ROW_CHUNK, prepare, 0)

    n_blocks = SUPER // BLOCK
    for g, dil in enumerate(DILATIONS):

        def block(t, carry, dil=dil, g=g):
            r = t % dil
            j = t // dil
            base = j * (BLOCK * dil) + r
            qs = q_ref[g, pl.ds(base, BLOCK, stride=dil), :]
            kstart = SUPER + base - BLOCK * dil
            ks = k_ref[g, pl.ds(kstart, 2 * BLOCK, stride=dil), :]
            vs = v_ref[g, pl.ds(kstart, 2 * BLOCK, stride=dil), :]
            mask = _band_mask(jnp.logical_and(sb == 0, j == 0))
            ks_bf = ks.astype(BF16)
            vs_bf = vs.astype(BF16)
            head0 = lax.broadcasted_iota(jnp.int32, (BLOCK, lw), 1) < HEAD_DIM
            o_h, l_h = [], []
            for hh in range(2):
                q_h = jnp.where(head0 if hh == 0 else jnp.logical_not(head0), qs, 0.0).astype(BF16)
                s = jnp.where(mask, _dot_nt(q_h, ks_bf), NEG_INF)
                m = jnp.max(s, axis=-1, keepdims=True)
                p = jnp.exp(s - m)
                den = jnp.sum(p, axis=-1, keepdims=True)
                o_h.append(_dot(p.astype(BF16), vs_bf) / den)
                l_h.append(m + jnp.log(den))
            o_ref[g, pl.ds(base, BLOCK, stride=dil), :] = jnp.where(head0, o_h[0], o_h[1])
            l_ref[g, pl.ds(base, BLOCK, stride=dil), :] = jnp.where(head0, l_h[0], l_h[1])
            return carry

        lax.fori_loop(0, n_blocks, block, 0, unroll=BLOCKS_PER_TRIP)

    def merge(c, carry):
        rows = pl.ds(pl.multiple_of(c * ROW_CHUNK, ROW_CHUNK), ROW_CHUNK)
        ls = [l_ref[g, rows, :] for g in range(n_groups)]
        lmax = jnp.maximum(jnp.maximum(ls[0], ls[1]), ls[2])
        es = [jnp.exp(l - lmax) for l in ls]
        inv = 1.0 / (es[0] + es[1] + es[2])
        for g in range(n_groups):
            gate = pd_ref[rows, 3 * MIX + g * lw:3 * MIX + (g + 1) * lw].astype(F32)
            y = o_ref[g, rows, :] * (es[g] * inv)
            yd_ref[rows, g * lw:(g + 1) * lw] = (y * _silu(gate)).astype(yd_ref.dtype)
        return carry

    lax.fori_loop(0, SUPER // ROW_CHUNK, merge, 0)

    for g, (c_ref, dil) in enumerate(zip((c1_ref, c4_ref, c16_ref), DILATIONS)):
        n = BLOCK * dil
        for j in range(0, n, ROW_CHUNK if n >= ROW_CHUNK else n):
            w = min(ROW_CHUNK, n)
            src = slice(2 * SUPER - n + j, 2 * SUPER - n + j + w)
            c_ref[0:lw, j:j + w] = k_ref[g, src, :].T
            c_ref[lw:2 * lw, j:j + w] = v_ref[g, src, :].T

    k_ref[:, 0:SUPER, :] = k_ref[:, SUPER:2 * SUPER, :]
    v_ref[:, 0:SUPER, :] = v_ref[:, SUPER:2 * SUPER, :]


def _mix_d(pd, cos, sin):
    b, s, _ = pd.shape
    est = (2 * SUPER * COLS_D * 2 + 4 * SUPER * BLOCK * 4 + 2 * SUPER * MIX * 2
           + 2 * KV_CH * (BLOCK + 4 * BLOCK + SUPER) * 4 + (3 + 2 * 2) * SUPER * MIX * 4)
    tile = lambda bi, si: (bi, si, 0)
    cache = lambda bi, si: (bi, 0, 0)
    return pl.pallas_call(
        _mix_d_body,
        grid=(b, s // SUPER),
        in_specs=[pl.BlockSpec((None, SUPER, COLS_D), tile),
                  pl.BlockSpec((SUPER, BLOCK), lambda bi, si: (si, 0)),
                  pl.BlockSpec((SUPER, BLOCK), lambda bi, si: (si, 0))],
        out_specs=[pl.BlockSpec((None, SUPER, MIX), tile)]
                  + [pl.BlockSpec((None, KV_CH, BLOCK * d), cache) for d in DILATIONS],
        out_shape=[jax.ShapeDtypeStruct((b, s, MIX), BF16)]
                  + [jax.ShapeDtypeStruct((b, KV_CH, BLOCK * d), F32) for d in DILATIONS],
        scratch_shapes=[pltpu.VMEM((3, SUPER, 2 * HEAD_DIM), F32),
                        pltpu.VMEM((3, 2 * SUPER, 2 * HEAD_DIM), F32),
                        pltpu.VMEM((3, 2 * SUPER, 2 * HEAD_DIM), F32),
                        pltpu.VMEM((3, SUPER, 2 * HEAD_DIM), F32),
                        pltpu.VMEM((3, SUPER, 2 * HEAD_DIM), F32)],
        compiler_params=pltpu.CompilerParams(dimension_semantics=("arbitrary", "arbitrary"),
                                             vmem_limit_bytes=_vmem_limit(est)),
        name="mix_d",
    )(pd, cos, sin)


def _shift_pass(c_ref, o_ref, i, ch0, new_b, per_tile):
    n_tiles = c_ref.shape[2] // BLOCK
    chans = pl.ds(ch0, 2 * HEAD_DIM)
    lane = lax.broadcasted_iota(jnp.int32, (2 * HEAD_DIM, BLOCK), 1)
    prev = None
    for t in range(n_tiles):
        x = c_ref[i, chans, t * BLOCK:(t + 1) * BLOCK]
        per_tile(t, x)
        r = pltpu.roll(x, BLOCK - 1, 1)
        if t > 0:
            o_ref[i, chans, (t - 1) * BLOCK:t * BLOCK] = jnp.where(lane < BLOCK - 1, prev, r)
        prev = r
    o_ref[i, chans, (n_tiles - 1) * BLOCK:n_tiles * BLOCK] = jnp.where(lane < BLOCK - 1, prev, new_b)


def _decode_cache(c_ref, o_ref, i, heads, k_new, v_new, dil):
    hd = HEAD_DIM
    n = c_ref.shape[2]
    n_tiles = n // BLOCK
    s_tiles = [[None] * (n_tiles + 1) for _ in heads]

    def scores(t, x):
        for qi, (g, q_b, _) in enumerate(heads):
            s_tiles[qi][t] = jnp.sum(x[g * hd:(g + 1) * hd, :] * q_b, axis=0, keepdims=True)

    _shift_pass(c_ref, o_ref, i, 0, k_new, scores)
    scores(n_tiles, k_new)

    lane = lax.broadcasted_iota(jnp.int32, (1, n + BLOCK), 1)
    valid = jnp.logical_or(jnp.logical_and(lane < n, (lane & (dil - 1)) == 0), lane == n + BLOCK - 1)
    probs, dens, lses = [], [], []
    for qi, (g, q_b, sink) in enumerate(heads):
        s = jnp.where(valid, jnp.concatenate(s_tiles[qi], axis=1), NEG_INF)
        m = jnp.max(s, axis=1, keepdims=True)
        if sink is not None:
            m = jnp.maximum(m, sink)
        p = jnp.exp(s - m)
        den = jnp.sum(p, axis=1, keepdims=True)
        if sink is not None:
            den = den + jnp.exp(sink - m)
        probs.append(p)
        dens.append(den)
        lses.append(m + jnp.log(den))

    accs = [jnp.zeros((hd, BLOCK), F32) for _ in heads]

    def weighted(t, x):
        for qi, (g, _, _) in enumerate(heads):
            accs[qi] = accs[qi] + x[g * hd:(g + 1) * hd, :] * probs[qi][:, t * BLOCK:(t + 1) * BLOCK]

    _shift_pass(c_ref, o_ref, i, 2 * hd, v_new, weighted)
    weighted(n_tiles, v_new)

    outs = [jnp.sum(accs[qi], axis=1, keepdims=True) / dens[qi] for qi in range(len(heads))]
    return outs, lses


XT_KB, XT_VB, XT_KD, XT_VD = 0, BLOCK, 2 * BLOCK, 2 * BLOCK + MIX
XT_ROWS = 2 * BLOCK + 2 * MIX


SEQ_PER_STEP = 2


def _decode_prep_body(pa_ref, pb_ref, pc_ref, pd_ref, st_ref, cw_ref, lg_ref, lb_ref,
                      wd_ref, bd_ref, cos_ref, sin_ref,
                      ya_ref, yc_ref, nst_ref, vn_ref, xq_ref, xt_ref):
    a_b = pa_ref[:, 0:MIX]
    z = pa_ref[:, MIX:2 * MIX] * pa_ref[:, 2 * MIX:3 * MIX]
    prev0 = st_ref[:, 0:MIX]
    prev1 = st_ref[:, MIX:2 * MIX]
    conv = cw_ref[0:1, :] * prev0 + cw_ref[1:2, :] * prev1 + cw_ref[2:3, :] * z
    ya_ref[...] = a_b * conv * _silu(pa_ref[:, 3 * MIX:4 * MIX])
    nst_ref[:, 0:MIX] = prev1
    nst_ref[:, MIX:2 * MIX] = z

    c_v = pc_ref[:, MIX:2 * MIX]
    mu = jnp.mean(c_v, axis=-1, keepdims=True)
    cen = c_v - mu
    var = jnp.mean(cen * cen, axis=-1, keepdims=True)
    vn = cen * lax.rsqrt(var + EPS) * lg_ref[...] + lb_ref[...]
    vn_ref[...] = vn
    mixed = wd_ref[...] * vn + bd_ref[...]
    yc_ref[...] = pc_ref[:, 0:MIX] * mixed * _silu(pc_ref[:, 2 * MIX:3 * MIX])

    cos = cos_ref[...]
    sin = sin_ref[...]
    cos3, sin3 = _tile_lanes(cos, 3), _tile_lanes(sin, 3)
    xq_ref[0:MIX, :] = (_rope(pb_ref[:, 0:MIX], cos3, sin3) * SCALE).T.astype(BF16)
    xq_ref[MIX:2 * MIX, :] = (_rope(pd_ref[:, 0:MIX], cos3, sin3) * SCALE).T.astype(BF16)
    xt_ref[XT_KB:XT_VB, :] = _rope(pb_ref[:, MIX:MIX + BLOCK], cos, sin).T
    xt_ref[XT_VB:XT_KD, :] = pb_ref[:, MIX + BLOCK:MIX + 2 * BLOCK].T
    xt_ref[XT_KD:XT_VD, :] = _rope(pd_ref[:, MIX:2 * MIX], cos3, sin3).T
    xt_ref[XT_VD:XT_ROWS, :] = pd_ref[:, 2 * MIX:3 * MIX].T


def _decode_prep(pa, pb, pc, pd, state, conv_w, ln_g, ln_b, w_diag, b_diag, cos, sin):
    nb = pa.shape[0]
    assert nb == BLOCK
    shapes = [((nb, MIX), F32), ((nb, MIX), F32), ((nb, 2 * MIX), F32), ((nb, MIX), F32),
              ((2 * MIX, nb), BF16), ((XT_ROWS, nb), F32)]
    return pl.pallas_call(
        _decode_prep_body,
        out_shape=[jax.ShapeDtypeStruct(s, d) for s, d in shapes],
        name="decode_prep",
    )(pa, pb, pc, pd, state, conv_w, ln_g, ln_b, w_diag, b_diag, cos, sin)


def _decode_sequences(dec, sink_ref, xq_ref, xt_ref, cache_refs, out_refs):
    cs_ref, c1_ref, c4_ref, c16_ref = cache_refs
    ns_ref, n1_ref, n4_ref, n16_ref, yt_ref = out_refs
    step = pl.program_id(0)
    n_seq = xq_ref.shape[1]
    hd = HEAD_DIM

    @pl.when(step == 0)
    def _():
        yt_ref[...] = jnp.zeros_like(yt_ref)

    for i in range(SEQ_PER_STEP):
        b = dec.seq_base + step * SEQ_PER_STEP + i
        onehot = (lax.broadcasted_iota(jnp.int32, (n_seq, BLOCK), 0) == b).astype(BF16)
        q_b = _dot(xq_ref[...], onehot)
        new_rot = pltpu.roll(xt_ref[...], BLOCK - 1 - b, 1)
        heads = [(h // SWA_REP, q_b[h * hd:(h + 1) * hd], jnp.full((1, 1), sink_ref[h], F32))
                 for h in range(SWA_Q_HEADS)]
        outs, _ = _decode_cache(cs_ref, ns_ref, i, heads, new_rot[XT_KB:XT_VB], new_rot[XT_VB:XT_KD], 1)
        yb_col = jnp.concatenate(outs, axis=0)

        q_col = q_b[MIX:2 * MIX]
        k_new = new_rot[XT_KD:XT_VD]
        v_new = new_rot[XT_VD:XT_ROWS]
        o_heads, l_heads = [], []
        for g, (c_ref, n_ref, dil) in enumerate(zip((c1_ref, c4_ref, c16_ref),
                                                    (n1_ref, n4_ref, n16_ref), DILATIONS)):
            gc = slice(2 * g * hd, 2 * (g + 1) * hd)
            heads = [(hh, q_col[(2 * g + hh) * hd:(2 * g + hh + 1) * hd], None) for hh in range(2)]
            o, lse = _decode_cache(c_ref, n_ref, i, heads, k_new[gc], v_new[gc], dil)
            o_heads += o
            l_heads += lse
        cols = []
        for hh in range(2):
            ls = [l_heads[2 * g + hh] for g in range(3)]
            lmax = jnp.maximum(jnp.maximum(ls[0], ls[1]), ls[2])
            es = [jnp.exp(l - lmax) for l in ls]
            esum = es[0] + es[1] + es[2]
            cols.append([o_heads[2 * g + hh] * (es[g] / esum) for g in range(3)])
        yd_col = jnp.concatenate([cols[hh][g] for g in range(3) for hh in range(2)], axis=0)
        y_col = jnp.concatenate([yb_col, yd_col], axis=0)
        lane_y = lax.broadcasted_iota(jnp.int32, (2 * MIX, n_seq), 1)
        yt_ref[...] = jnp.where(lane_y == b, y_col, yt_ref[...])


def _decode_fin_body(yt0_ref, yt1_ref, pb_ref, pd_ref, yb_ref, yd_ref, *, split):
    lane = lax.broadcasted_iota(jnp.int32, yt0_ref.shape, 1)
    yt = jnp.where(lane < split, yt0_ref[...], yt1_ref[...])
    yb_ref[...] = yt[0:MIX, :].T * _silu(pb_ref[:, MIX + 2 * BLOCK:COLS_B])
    yd_ref[...] = yt[MIX:2 * MIX, :].T * _silu(pd_ref[:, 3 * MIX:4 * MIX])


def _decode_fin(yt0, yt1, pb, pd, split):
    nb = pb.shape[0]
    return pl.pallas_call(
        functools.partial(_decode_fin_body, split=split),
        out_shape=[jax.ShapeDtypeStruct((nb, MIX), F32)] * 2,
        name="decode_fin",
    )(yt0, yt1, pb, pd)


class _DecodeStatic(NamedTuple):
    seq_base: int
    n_alias: int


class _Decode:
    def __init__(self, layer, seq_base, steps, sinks, xq, xt, caches, prev_outs):
        self.layer, self.seq_base, self.steps = layer, seq_base, steps
        self.sinks, self.xq, self.xt = sinks, xq, xt
        self.caches, self.prev_outs = tuple(caches), tuple(prev_outs)
        self.static = _DecodeStatic(seq_base, len(self.prev_outs))

    def extend(self, in_specs, out_specs, out_shape, args):
        layer, block0 = self.layer, self.seq_base // SEQ_PER_STEP
        spec = lambda c: pl.BlockSpec((None, SEQ_PER_STEP) + c.shape[2:],
                                      lambda i: (layer, block0 + i, 0, 0))
        first_alias_in = 1 + len(args) + 2 + len(self.caches)
        aliases = {first_alias_in + i: len(out_specs) + i for i in range(len(self.prev_outs))}
        in_specs = ([pl.BlockSpec(memory_space=pltpu.SMEM)] + in_specs
                    + [_const_spec(self.xq.shape), _const_spec(self.xt.shape)]
                    + [spec(c) for c in self.caches]
                    + [pl.BlockSpec(memory_space=pl.ANY) for _ in self.prev_outs])
        args = [self.sinks] + args + [self.xq, self.xt, *self.caches, *self.prev_outs]
        yt_shape = (2 * MIX, self.xq.shape[1])
        out_specs = out_specs + [spec(c) for c in self.caches] + [_const_spec(yt_shape)]
        out_shape = (out_shape + [jax.ShapeDtypeStruct(c.shape, F32) for c in self.caches]
                     + [jax.ShapeDtypeStruct(yt_shape, F32)])
        rows = sum(c.shape[3] for c in self.caches)
        extra = 4 * SEQ_PER_STEP * KV_CH * rows * 4 + 4 * (2 * MIX + XT_ROWS) * BLOCK * 4
        return in_specs, out_specs, out_shape, args, aliases, extra


def _rope_tables(pos):
    inv_freq = ROPE_THETA ** (-jnp.arange(HALF_HEAD, dtype=F32) / HALF_HEAD)
    ang = pos.astype(F32)[:, None] * inv_freq[None, :]
    cos, sin = jnp.cos(ang), jnp.sin(ang)
    return (jnp.concatenate([cos, cos, cos, cos], axis=1),
            jnp.concatenate([-sin, sin, -sin, sin], axis=1))


def _cache_view(c):
    d, b, n = c.shape[:3]
    return jnp.transpose(c, (0, 1, 3, 4, 5, 2)).reshape(d, b, KV_CH, n)


def _cache_unview(c):
    lead, n = c.shape[:-2], c.shape[-1]
    nl = len(lead)
    perm = tuple(range(nl)) + (nl + 3, nl, nl + 1, nl + 2)
    return jnp.transpose(c.reshape(lead + (2, 2, HEAD_DIM, n)), perm)


def kernel(x_prompt, x_sample, state_conv, cache_swa_kv, cache_dil1_kv, cache_dil4_kv, cache_dil16_kv,
           norm_g, w_in, conv_w, attn_sinks, v_ln_g, v_ln_b, w_spatial, b_spatial,
           w_branch, w_merge, w_out, final_norm_g):
    depth = w_in.shape[0]
    nb, seq, _ = x_prompt.shape
    ns, dec_seq, _ = x_sample.shape
    assert dec_seq == 1 and seq % SUPER == 0
    past_len = 16384
    assert cache_swa_kv.shape[2] == BLOCK and cache_dil16_kv.shape[2] == SUPER

    cos_p, sin_p = _rope_tables(jnp.arange(seq, dtype=jnp.int32))
    cos_s, sin_s = _rope_tables(past_len + jnp.arange(1, dtype=jnp.int32))
    w_in_bf = w_in.astype(BF16)
    w_merge_bf = w_merge.astype(BF16)
    w_branch_bf = w_branch.astype(BF16)
    w_out_bf = w_out.astype(BF16)
    caches = [_cache_view(c) for c in (cache_swa_kv, cache_dil1_kv, cache_dil4_kv, cache_dil16_kv)]
    fg = final_norm_g.reshape(1, D_MODEL)

    xp = x_prompt.reshape(nb * seq, D_MODEL)
    xs = x_sample.reshape(ns, D_MODEL)
    conv_p, conv_s, chunk_v = [], [], []
    new_p = [[] for _ in range(4)]
    new_s = ()
    for l in range(depth):
        g = norm_g[l].reshape(1, D_MODEL)
        final = l == depth - 1
        ln_g = v_ln_g[l].reshape(1, MIX)
        ln_b = v_ln_b[l].reshape(1, MIX)

        sa, sb, sc, sd = _inproj(xs, g, w_in_bf, l, ns, F32)
        w_diag = jnp.repeat(w_spatial[l][:, 0, 0], HEAD_DIM).reshape(1, MIX)
        b_diag = jnp.repeat(b_spatial[l][:, 0], HEAD_DIM).reshape(1, MIX)
        za, zc, nstate, vn, xq, xt = _decode_prep(sa, sb, sc, sd, state_conv[l].reshape(ns, 2 * MIX),
                                                  conv_w[l], ln_g, ln_b, w_diag, b_diag, cos_s, sin_s)

        tm = 512
        steps = nb * seq // tm
        half = steps * SEQ_PER_STEP
        assert 2 * half == ns
        dec = _Decode(l, 0, steps, attn_sinks[l], xq, xt, caches, new_s)
        pa, pb, pc, pd, *new_s, yt0 = _inproj(xp, g, w_in_bf, l, tm, BF16, dec)
        shp = lambda a: a.reshape(nb, seq, a.shape[-1])
        b_tile = jnp.repeat(b_spatial[l].T, HEAD_DIM, axis=1)
        ya, yc, nconv = _mix_ac(shp(pa), shp(pc), conv_w[l], ln_g, ln_b, w_spatial[l], b_tile, 512)
        yb, kv_swa = _mix_b(shp(pb), attn_sinks[l], cos_p, sin_p, 1024)
        yd, kv1, kv4, kv16 = _mix_d(shp(pd), cos_p, sin_p)
        flat = lambda a: a.reshape(nb * seq, MIX)
        dec = _Decode(l, half, steps, attn_sinks[l], xq, xt, caches, new_s)
        xp, *new_s, yt1 = _outproj(xp, (flat(ya), flat(yb), flat(yc), flat(yd)), g, w_merge_bf,
                                   w_branch_bf, w_out_bf, fg, l, tm, final, dec)
        conv_p.append(nconv)
        new_p[0].append(_cache_unview(kv_swa))
        for i, kv in enumerate((kv1, kv4, kv16)):
            new_p[i + 1].append(_cache_unview(kv))

        zb, zd = _decode_fin(yt0, yt1, sb, sd, half)
        xs, = _outproj(xs, (za, zb, zc, zd), g, w_merge_bf, w_branch_bf, w_out_bf, fg, l, ns, final)
        conv_s.append(nstate.reshape(ns, 2, MIX))
        chunk_v.append(vn.reshape(ns, 1, MIX))

    st = jnp.stack
    new_s = [_cache_unview(c) for c in new_s]
    return (xp.reshape(nb, seq, D_MODEL), xs.reshape(ns, 1, D_MODEL),
            st(conv_p), st(conv_s),
            st(new_p[0]), new_s[0], st(new_p[1]), new_s[1],
            st(new_p[2]), new_s[2], st(new_p[3]), new_s[3],
            st(chunk_v))
```

```python
import functools
from typing import NamedTuple

import jax
import jax.numpy as jnp
from jax import lax
from jax.experimental import pallas as pl
from jax.experimental.pallas import tpu as pltpu

F32 = jnp.float32
BF16 = jnp.bfloat16

D_MODEL = 1024
HEAD_DIM = 64
HALF_HEAD = HEAD_DIM // 2
MIX = 384
BLOCK = 128
SWA_Q_HEADS = 6
SWA_KV_HEADS = 2
SWA_REP = SWA_Q_HEADS // SWA_KV_HEADS
DILATIONS = (1, 4, 16)
SUPER = BLOCK * DILATIONS[-1]
BLOCKS_PER_TRIP = 4
ROW_CHUNK = 256
ROPE_THETA = 10000.0
EPS = 1e-6
NEG_INF = -1e30
SCALE = HEAD_DIM ** -0.5

COLS_A = 4 * MIX
COLS_B = MIX + 2 * BLOCK + MIX
COLS_C = 3 * MIX
COLS_D = 4 * MIX
OFF_A, OFF_B, OFF_C, OFF_D = 0, COLS_A, COLS_A + COLS_B, COLS_A + COLS_B + COLS_C
IN_COLS = OFF_D + COLS_D
KV_CH = 4 * HEAD_DIM

V7X_VMEM_BYTES = 64 * 1024 * 1024
MATMUL_COL_CHUNK = 512


def _vmem_limit(estimate_bytes):
    return int(min(V7X_VMEM_BYTES - 8 * 1024 * 1024, estimate_bytes + 16 * 1024 * 1024))


def _const_spec(shape, single=False):
    nd = len(shape)
    if single:
        return pl.BlockSpec(shape, lambda *_: (0,) * nd, pipeline_mode=pl.Buffered(1))
    return pl.BlockSpec(shape, lambda *_: (0,) * nd)


def _layer_spec(stacked_shape, layer):
    nd = len(stacked_shape) - 1
    return pl.BlockSpec((None,) + tuple(stacked_shape[1:]), lambda *_: (layer,) + (0,) * nd,
                        pipeline_mode=pl.Buffered(1))


def _rms(x, g):
    ms = jnp.mean(x * x, axis=-1, keepdims=True)
    return x * lax.rsqrt(ms + EPS) * g


def _sigmoid(x):
    return 0.5 * jnp.tanh(0.5 * x) + 0.5


def _silu(x):
    h = 0.5 * x
    return h * jnp.tanh(h) + h


def _rope(x, cos, sin):
    w = x.shape[-1]
    lane = lax.broadcasted_iota(jnp.int32, x.shape, 1)
    fwd = pltpu.roll(x, w - HALF_HEAD, 1)
    bwd = pltpu.roll(x, HALF_HEAD, 1)
    partner = jnp.where((lane & HALF_HEAD) == 0, fwd, bwd)
    return x * cos + partner * sin


def _tile_lanes(t, reps):
    return t if reps == 1 else jnp.concatenate([t] * reps, axis=1)


def _dot(a, b):
    return jnp.dot(a, b, preferred_element_type=F32)


def _dot_nt(a, b):
    return lax.dot_general(a, b, (((1,), (1,)), ((), ())), preferred_element_type=F32)


def _project(h, w_ref, off, width, store):
    for j in range(0, width, MATMUL_COL_CHUNK):
        cw = min(MATMUL_COL_CHUNK, width - j)
        store(j, cw, _dot(h, w_ref[:, off + j:off + j + cw]))


def _inproj_body(x_ref, g_ref, w_ref, *outs):
    h = _rms(x_ref[...], g_ref[...]).astype(BF16)
    for o_ref, off in zip(outs, (OFF_A, OFF_B, OFF_C, OFF_D)):
        def store(j, cw, acc, o_ref=o_ref):
            o_ref[:, j:j + cw] = acc.astype(o_ref.dtype)
        _project(h, w_ref, off, o_ref.shape[-1], store)


def _inproj(x, g, w_bf16, layer, tm, out_dtype):
    t = x.shape[0]
    widths = (COLS_A, COLS_B, COLS_C, COLS_D)
    osize = jnp.dtype(out_dtype).itemsize
    est = 2 * tm * D_MODEL * 4 + D_MODEL * IN_COLS * 2 + 2 * tm * IN_COLS * osize
    row = lambda i: (i, 0)
    return pl.pallas_call(
        _inproj_body,
        grid=(t // tm,),
        in_specs=[pl.BlockSpec((tm, D_MODEL), row), _const_spec((1, D_MODEL)),
                  _layer_spec(w_bf16.shape, layer)],
        out_specs=[pl.BlockSpec((tm, wd), row) for wd in widths],
        out_shape=[jax.ShapeDtypeStruct((t, wd), out_dtype) for wd in widths],
        compiler_params=pltpu.CompilerParams(dimension_semantics=("parallel",),
                                             vmem_limit_bytes=_vmem_limit(est)),
        name="inproj",
    )(x, g, w_bf16)


def _outproj_body(*refs, final, dec):
    if dec is None:
        (x_ref, ya_ref, yb_ref, yc_ref, yd_ref, g_ref, wm_ref, wb_ref, wo_ref, fg_ref, o_ref) = refs
    else:
        sink_ref = refs[0]
        (x_ref, ya_ref, yb_ref, yc_ref, yd_ref, g_ref, wm_ref, wb_ref, wo_ref, fg_ref) = refs[1:11]
        xq_ref, xt_ref = refs[11:13]
        o_ref = refs[17 + dec.n_alias]
        step = pl.program_id(0)
        dec_outs = refs[18 + dec.n_alias:]

        @pl.when(step == 0)
        def _():
            dec_outs[4][...] = jnp.zeros_like(dec_outs[4])

        _decode_sequences(dec, step, sink_ref, xq_ref, xt_ref, refs[13:17], dec_outs)
    x = x_ref[...]
    h = _rms(x, g_ref[...]).astype(BF16)
    merged = None
    for n, y_ref in enumerate((ya_ref, yb_ref, yc_ref, yd_ref)):
        gate = _sigmoid(_dot(h, wm_ref[:, n * D_MODEL:(n + 1) * D_MODEL]))
        term = gate * _dot(y_ref[...].astype(BF16), wb_ref[n])
        merged = term if merged is None else merged + term
    out = x + _dot(merged.astype(BF16), wo_ref[...])
    if final:
        out = _rms(out, fg_ref[...])
    o_ref[...] = out


def _outproj(x, ys, g, wm, wb, wo, fg, layer, tm, final, dec=None):
    t = x.shape[0]
    ysize = jnp.dtype(ys[0].dtype).itemsize
    est = (4 * tm * D_MODEL * 4 + 2 * 4 * tm * MIX * ysize
           + (4 * D_MODEL * D_MODEL + 4 * MIX * D_MODEL + D_MODEL * D_MODEL) * 2
           + 4 * tm * D_MODEL * 4)
    row = lambda i: (i, 0)
    in_specs = ([pl.BlockSpec((tm, D_MODEL), row)]
                + [pl.BlockSpec((tm, MIX), row)] * 4
                + [_const_spec((1, D_MODEL)),
                   _layer_spec(wm.shape, layer), _layer_spec(wb.shape, layer), _layer_spec(wo.shape, layer),
                   _const_spec((1, D_MODEL))])
    out_specs = [pl.BlockSpec((tm, D_MODEL), row)]
    out_shape = [jax.ShapeDtypeStruct((t, D_MODEL), F32)]
    args = [x, *ys, g, wm, wb, wo, fg]
    aliases = {}
    if dec is not None:
        in_specs, out_specs, out_shape, args, aliases, extra = dec.extend(in_specs, out_specs, out_shape, args)
        est += extra
    return pl.pallas_call(
        functools.partial(_outproj_body, final=final, dec=None if dec is None else dec.static),
        grid=(t // tm,),
        in_specs=in_specs, out_specs=out_specs, out_shape=out_shape,
        input_output_aliases=aliases,
        compiler_params=pltpu.CompilerParams(
            dimension_semantics=("parallel",) if dec is None else ("arbitrary",),
            vmem_limit_bytes=_vmem_limit(est)),
        name="outproj" if dec is None else "outproj_dec",
    )(*args)


def _mix_a(pa_ref, cw_ref, ya_ref, prev2, prev1):
    row = lax.broadcasted_iota(jnp.int32, (BLOCK, MIX), 0)
    for ck in range(pa_ref.shape[0] // BLOCK):
        rows = slice(ck * BLOCK, (ck + 1) * BLOCK)
        a_b = pa_ref[rows, 0:MIX]
        z = pa_ref[rows, MIX:2 * MIX] * pa_ref[rows, 2 * MIX:3 * MIX]
        a_gate = pa_ref[rows, 3 * MIX:4 * MIX]
        z1 = jnp.where(row == 0, prev1, pltpu.roll(z, 1, 0))
        z2 = jnp.where(row == 0, prev2, jnp.where(row == 1, prev1, pltpu.roll(z, 2, 0)))
        conv = cw_ref[0:1, :] * z2 + cw_ref[1:2, :] * z1 + cw_ref[2:3, :] * z
        ya_ref[rows, :] = (a_b * conv * _silu(a_gate)).astype(ya_ref.dtype)
        prev2 = z[BLOCK - 2:BLOCK - 1, :]
        prev1 = z[BLOCK - 1:BLOCK, :]
    return prev2, prev1


def _mix_c(pc_ref, col0, lg_ref, lb_ref, ws_ref, bs_ref, yc_ref):
    r = lax.broadcasted_iota(jnp.int32, (BLOCK, BLOCK), 0)
    c = lax.broadcasted_iota(jnp.int32, (BLOCK, BLOCK), 1)
    n_groups = MIX // HEAD_DIM
    w_causal = [jnp.where(r >= c, ws_ref[g], 0.0).astype(BF16) for g in range(n_groups)]
    for ck in range(pc_ref.shape[0] // BLOCK):
        rows = slice(ck * BLOCK, (ck + 1) * BLOCK)
        c_v = pc_ref[rows, col0 + MIX:col0 + 2 * MIX]
        mu = jnp.mean(c_v, axis=-1, keepdims=True)
        cen = c_v - mu
        var = jnp.mean(cen * cen, axis=-1, keepdims=True)
        vn_c = (cen * lax.rsqrt(var + EPS) * lg_ref[...] + lb_ref[...]).astype(BF16)
        mixed = jnp.concatenate(
            [_dot(w_causal[g], vn_c[:, g * HEAD_DIM:(g + 1) * HEAD_DIM]) for g in range(n_groups)],
            axis=1) + bs_ref[...]
        c_u = pc_ref[rows, col0:col0 + MIX]
        c_gate = pc_ref[rows, col0 + 2 * MIX:col0 + 3 * MIX]
        yc_ref[rows, :] = (c_u * mixed * _silu(c_gate)).astype(yc_ref.dtype)


def _band_mask(first_block):
    i = lax.broadcasted_iota(jnp.int32, (BLOCK, 2 * BLOCK), 0)
    j = lax.broadcasted_iota(jnp.int32, (BLOCK, 2 * BLOCK), 1)
    band = (j >= i) & (j <= i + BLOCK)
    return band & jnp.logical_or(jnp.logical_not(first_block), j >= BLOCK)


def _mix_b_prepare(pb_ref, cos_ref, sin_ref, kv_ref, q_s, hist):
    k_s, v_s, ksw_s, vsw_s = hist
    ts = pb_ref.shape[0]
    for r0 in range(0, ts, ROW_CHUNK):
        rows = slice(r0, r0 + ROW_CHUNK)
        dst = slice(BLOCK + r0, BLOCK + r0 + ROW_CHUNK)
        cos = cos_ref[rows, :]
        sin = sin_ref[rows, :]
        q_s[rows, :] = (_rope(pb_ref[rows, 0:MIX], _tile_lanes(cos, 3), _tile_lanes(sin, 3)) * SCALE
                        ).astype(BF16)
        k = _rope(pb_ref[rows, MIX:MIX + BLOCK], cos, sin)
        v = pb_ref[rows, MIX + BLOCK:MIX + 2 * BLOCK]
        k_s[dst, :] = k.astype(BF16)
        v_s[dst, :] = v.astype(BF16)
        ksw_s[dst, :] = pltpu.roll(k, HEAD_DIM, 1).astype(BF16)
        vsw_s[dst, :] = pltpu.roll(v, HEAD_DIM, 1).astype(BF16)
        if r0 + ROW_CHUNK == ts:
            kv_ref[0:BLOCK, :] = k[ROW_CHUNK - BLOCK:ROW_CHUNK, :].T
            kv_ref[BLOCK:2 * BLOCK, :] = v[ROW_CHUNK - BLOCK:ROW_CHUNK, :].T


def _mix_b_attend(sink_ref, pb_ref, first_tile, yb_ref, q_s, hist):
    k_s, v_s, ksw_s, vsw_s = hist
    ts = pb_ref.shape[0]
    half0 = lax.broadcasted_iota(jnp.int32, (BLOCK, BLOCK), 1) < HEAD_DIM
    for c in range(ts // BLOCK):
        rows = slice(c * BLOCK, (c + 1) * BLOCK)
        win = slice(c * BLOCK, (c + 2) * BLOCK)
        kk, vv, kk_sw, vv_sw = k_s[win, :], v_s[win, :], ksw_s[win, :], vsw_s[win, :]
        mask = _band_mask(first_tile) if c == 0 else _band_mask(False)
        outs = []
        for h in range(SWA_Q_HEADS):
            g, chunk, half = h // SWA_REP, h // 2, h % 2
            qc = q_s[rows, chunk * BLOCK:(chunk + 1) * BLOCK]
            q_h = jnp.where(half0 if half == 0 else jnp.logical_not(half0), qc, jnp.zeros_like(qc))
            s = jnp.where(mask, _dot_nt(q_h, kk if half == g else kk_sw), NEG_INF)
            sink = sink_ref[h]
            m = jnp.maximum(jnp.max(s, axis=-1, keepdims=True), sink)
            p = jnp.exp(s - m)
            den = jnp.sum(p, axis=-1, keepdims=True) + jnp.exp(sink - m)
            outs.append(_dot(p.astype(BF16), vv if half == g else vv_sw) / den)
        y = jnp.concatenate([jnp.where(half0, outs[2 * j], outs[2 * j + 1]) for j in range(3)], axis=1)
        gate = pb_ref[rows, MIX + 2 * BLOCK:COLS_B]
        yb_ref[rows, :] = (y * _silu(gate)).astype(yb_ref.dtype)
    for r in hist:
        r[0:BLOCK, :] = r[ts:ts + BLOCK, :]


N_SUB = 2
COLS_BC = COLS_B + COLS_C


def _inproj_mix_body(*refs, dec, tiles_per_seq):
    (sink_ref, x_ref, g_ref, w_ref, cos_ref, sin_ref, cw_ref, lg_ref, lb_ref, ws_ref, bs_ref,
     xq_ref, xt_ref) = refs[:13]
    caches = refs[13:17]
    rest = refs[17 + dec.n_alias:]
    ya_ref, yb_ref, yc_ref, pd_ref, nc_ref, kv_ref = rest[:6]
    dec_outs = rest[6:11]
    h_s, pa_s, pbc_s, tail_s, q_s = rest[11:16]
    hist = rest[16:20]
    tile = pl.program_id(0)
    sub = pl.program_id(1)
    step = tile * N_SUB + sub

    @pl.when(step == 0)
    def _():
        dec_outs[4][...] = jnp.zeros_like(dec_outs[4])
        tail_s[...] = jnp.zeros_like(tail_s)
        for r in hist:
            r[0:BLOCK, :] = jnp.zeros((BLOCK, BLOCK), BF16)

    @pl.when(sub == 0)
    def _():
        h = _rms(x_ref[...], g_ref[...]).astype(BF16)
        h_s[...] = h

        def store(j, cw, acc, base):
            pbc_s[:, base + j:base + j + cw] = acc

        def store_d(j, cw, acc):
            pd_ref[:, j:j + cw] = acc.astype(pd_ref.dtype)

        _project(h, w_ref, OFF_C, COLS_C, functools.partial(store, base=COLS_B))
        _project(h, w_ref, OFF_B, COLS_B, functools.partial(store, base=0))
        _project(h, w_ref, OFF_D, COLS_D, store_d)
        _mix_c(pbc_s, COLS_B, lg_ref, lb_ref, ws_ref, bs_ref, yc_ref)
        _mix_b_prepare(pbc_s, cos_ref, sin_ref, kv_ref, q_s, hist)
        _decode_sequences(dec, step, sink_ref, xq_ref, xt_ref, caches, dec_outs)

    @pl.when(sub == 1)
    def _():
        def store_a(j, cw, acc):
            pa_s[:, j:j + cw] = acc

        _project(h_s[...], w_ref, OFF_A, COLS_A, store_a)

        first_tile = (tile % tiles_per_seq) == 0
        prev2 = jnp.where(first_tile, 0.0, tail_s[0:1, :])
        prev1 = jnp.where(first_tile, 0.0, tail_s[1:2, :])
        prev2, prev1 = _mix_a(pa_s, cw_ref, ya_ref, prev2, prev1)
        last2 = jnp.concatenate([prev2, prev1], axis=0)
        tail_s[...] = last2
        nc_ref[...] = last2
        _mix_b_attend(sink_ref, pbc_s, first_tile, yb_ref, q_s, hist)
        _decode_sequences(dec, step, sink_ref, xq_ref, xt_ref, caches, dec_outs)


def _inproj_mix(x, g, w_bf16, layer, cos, sin, conv_w, ln_g, ln_b, w_s, b_tile, tm, seq, dec):
    t = x.shape[0]
    nb = t // seq
    tiles_per_seq = seq // tm
    assert dec.n_sub == N_SUB
    row = lambda i, j: (i, 0)
    pos = lambda i, j: (i % tiles_per_seq, 0)
    per_seq = lambda i, j: (i // tiles_per_seq, 0, 0)
    in_specs = [pl.BlockSpec((tm, D_MODEL), row), _const_spec((1, D_MODEL)), _layer_spec(w_bf16.shape, layer),
                pl.BlockSpec((tm, BLOCK), pos), pl.BlockSpec((tm, BLOCK), pos),
                _const_spec((3, MIX)), _const_spec((1, MIX)), _const_spec((1, MIX)),
                _const_spec((MIX // HEAD_DIM, BLOCK, BLOCK)), _const_spec((BLOCK, MIX))]
    out_specs = ([pl.BlockSpec((tm, MIX), row)] * 3 + [pl.BlockSpec((tm, COLS_D), row),
                 pl.BlockSpec((None, 2, MIX), per_seq), pl.BlockSpec((None, KV_CH, BLOCK), per_seq)])
    out_shape = ([jax.ShapeDtypeStruct((t, MIX), BF16)] * 3 + [jax.ShapeDtypeStruct((t, COLS_D), BF16),
                 jax.ShapeDtypeStruct((nb, 2, MIX), F32), jax.ShapeDtypeStruct((nb, KV_CH, BLOCK), F32)])
    args = [x, g, w_bf16, cos, sin, conv_w, ln_g, ln_b, w_s, b_tile]
    scratch = [pltpu.VMEM((tm, D_MODEL), BF16), pltpu.VMEM((tm, COLS_A), F32), pltpu.VMEM((tm, COLS_BC), F32),
               pltpu.VMEM((2, MIX), F32), pltpu.VMEM((tm, MIX), BF16)] + [pltpu.VMEM((BLOCK + tm, BLOCK), BF16)] * 4
    est = (2 * tm * D_MODEL * 4 + D_MODEL * IN_COLS * 2 + 2 * tm * (3 * MIX + COLS_D) * 2
           + tm * (D_MODEL * 2 + (COLS_A + COLS_BC) * 4 + MIX * 2) + 4 * (BLOCK + tm) * BLOCK * 2
           + 4 * tm * BLOCK * 4)
    in_specs, out_specs, out_shape, args, aliases, extra = dec.extend(in_specs, out_specs, out_shape, args)
    return pl.pallas_call(
        functools.partial(_inproj_mix_body, dec=dec.static, tiles_per_seq=tiles_per_seq),
        grid=(t // tm, N_SUB),
        in_specs=in_specs, out_specs=out_specs, out_shape=out_shape,
        scratch_shapes=scratch,
        input_output_aliases=aliases,
        compiler_params=pltpu.CompilerParams(dimension_semantics=("arbitrary", "arbitrary"),
                                             vmem_limit_bytes=_vmem_limit(est + extra)),
        name="inproj_mix",
    )(*args)


def _mix_d_body(pd_ref, cos_ref, sin_ref, yd_ref, c1_ref, c4_ref, c16_ref,
                q_ref, k_ref, v_ref, o_ref, l_ref):
    sb = pl.program_id(1)

    lw = 2 * HEAD_DIM
    n_groups = len(DILATIONS)

    @pl.when(sb == 0)
    def _():
        k_ref[:, 0:SUPER, :] = jnp.zeros((n_groups, SUPER, lw), F32)
        v_ref[:, 0:SUPER, :] = jnp.zeros((n_groups, SUPER, lw), F32)

    def prepare(c, carry):
        r0 = pl.multiple_of(c * ROW_CHUNK, ROW_CHUNK)
        rows = pl.ds(r0, ROW_CHUNK)
        new_rows = pl.ds(SUPER + r0, ROW_CHUNK)
        cos = cos_ref[rows, :]
        sin = sin_ref[rows, :]
        for g in range(n_groups):
            q_ref[g, rows, :] = _rope(pd_ref[rows, g * lw:(g + 1) * lw].astype(F32), cos, sin) * SCALE
            k_ref[g, new_rows, :] = _rope(pd_ref[rows, MIX + g * lw:MIX + (g + 1) * lw].astype(F32), cos, sin)
            v_ref[g, new_rows, :] = pd_ref[rows, 2 * MIX + g * lw:2 * MIX + (g + 1) * lw].astype(F32)
        return carry

    lax.fori_loop(0, SUPER // ROW_CHUNK, prepare, 0)

    n_blocks = SUPER // BLOCK
    for g, dil in enumerate(DILATIONS):

        def block(t, carry, dil=dil, g=g):
            r = t % dil
            j = t // dil
            base = j * (BLOCK * dil) + r
            qs = q_ref[g, pl.ds(base, BLOCK, stride=dil), :]
            kstart = SUPER + base - BLOCK * dil
            ks = k_ref[g, pl.ds(kstart, 2 * BLOCK, stride=dil), :]
            vs = v_ref[g, pl.ds(kstart, 2 * BLOCK, stride=dil), :]
            mask = _band_mask(jnp.logical_and(sb == 0, j == 0))
            ks_bf = ks.astype(BF16)
            vs_bf = vs.astype(BF16)
            head0 = lax.broadcasted_iota(jnp.int32, (BLOCK, lw), 1) < HEAD_DIM
            o_h, l_h = [], []
            for hh in range(2):
                q_h = jnp.where(head0 if hh == 0 else jnp.logical_not(head0), qs, 0.0).astype(BF16)
                s = jnp.where(mask, _dot_nt(q_h, ks_bf), NEG_INF)
                m = jnp.max(s, axis=-1, keepdims=True)
                p = jnp.exp(s - m)
                den = jnp.sum(p, axis=-1, keepdims=True)
                o_h.append(_dot(p.astype(BF16), vs_bf) / den)
                l_h.append(m + jnp.log(den))
            o_ref[g, pl.ds(base, BLOCK, stride=dil), :] = jnp.where(head0, o_h[0], o_h[1])
            l_ref[g, pl.ds(base, BLOCK, stride=dil), :] = jnp.where(head0, l_h[0], l_h[1])
            return carry

        lax.fori_loop(0, n_blocks, block, 0, unroll=BLOCKS_PER_TRIP)

    def merge(c, carry):
        rows = pl.ds(pl.multiple_of(c * ROW_CHUNK, ROW_CHUNK), ROW_CHUNK)
        ls = [l_ref[g, rows, :] for g in range(n_groups)]
        lmax = jnp.maximum(jnp.maximum(ls[0], ls[1]), ls[2])
        es = [jnp.exp(l - lmax) for l in ls]
        inv = 1.0 / (es[0] + es[1] + es[2])
        for g in range(n_groups):
            gate = pd_ref[rows, 3 * MIX + g * lw:3 * MIX + (g + 1) * lw].astype(F32)
            y = o_ref[g, rows, :] * (es[g] * inv)
            yd_ref[rows, g * lw:(g + 1) * lw] = (y * _silu(gate)).astype(yd_ref.dtype)
        return carry

    lax.fori_loop(0, SUPER // ROW_CHUNK, merge, 0)

    for g, (c_ref, dil) in enumerate(zip((c1_ref, c4_ref, c16_ref), DILATIONS)):
        n = BLOCK * dil
        for j in range(0, n, ROW_CHUNK if n >= ROW_CHUNK else n):
            w = min(ROW_CHUNK, n)
            src = slice(2 * SUPER - n + j, 2 * SUPER - n + j + w)
            c_ref[0:lw, j:j + w] = k_ref[g, src, :].T
            c_ref[lw:2 * lw, j:j + w] = v_ref[g, src, :].T

    k_ref[:, 0:SUPER, :] = k_ref[:, SUPER:2 * SUPER, :]
    v_ref[:, 0:SUPER, :] = v_ref[:, SUPER:2 * SUPER, :]


def _mix_d(pd, cos, sin):
    b, s, _ = pd.shape
    est = (2 * SUPER * COLS_D * 2 + 4 * SUPER * BLOCK * 4 + 2 * SUPER * MIX * 2
           + 2 * KV_CH * (BLOCK + 4 * BLOCK + SUPER) * 4 + (3 + 2 * 2) * SUPER * MIX * 4)
    tile = lambda bi, si: (bi, si, 0)
    cache = lambda bi, si: (bi, 0, 0)
    return pl.pallas_call(
        _mix_d_body,
        grid=(b, s // SUPER),
        in_specs=[pl.BlockSpec((None, SUPER, COLS_D), tile),
                  pl.BlockSpec((SUPER, BLOCK), lambda bi, si: (si, 0)),
                  pl.BlockSpec((SUPER, BLOCK), lambda bi, si: (si, 0))],
        out_specs=[pl.BlockSpec((None, SUPER, MIX), tile)]
                  + [pl.BlockSpec((None, KV_CH, BLOCK * d), cache) for d in DILATIONS],
        out_shape=[jax.ShapeDtypeStruct((b, s, MIX), BF16)]
                  + [jax.ShapeDtypeStruct((b, KV_CH, BLOCK * d), F32) for d in DILATIONS],
        scratch_shapes=[pltpu.VMEM((3, SUPER, 2 * HEAD_DIM), F32),
                        pltpu.VMEM((3, 2 * SUPER, 2 * HEAD_DIM), F32),
                        pltpu.VMEM((3, 2 * SUPER, 2 * HEAD_DIM), F32),
                        pltpu.VMEM((3, SUPER, 2 * HEAD_DIM), F32),
                        pltpu.VMEM((3, SUPER, 2 * HEAD_DIM), F32)],
        compiler_params=pltpu.CompilerParams(dimension_semantics=("arbitrary", "arbitrary"),
                                             vmem_limit_bytes=_vmem_limit(est)),
        name="mix_d",
    )(pd, cos, sin)


def _shift_pass(c_ref, o_ref, i, ch0, new_b, per_tile):
    n_tiles = c_ref.shape[2] // BLOCK
    chans = pl.ds(ch0, 2 * HEAD_DIM)
    lane = lax.broadcasted_iota(jnp.int32, (2 * HEAD_DIM, BLOCK), 1)
    prev = None
    for t in range(n_tiles):
        x = c_ref[i, chans, t * BLOCK:(t + 1) * BLOCK]
        per_tile(t, x)
        r = pltpu.roll(x, BLOCK - 1, 1)
        if t > 0:
            o_ref[i, chans, (t - 1) * BLOCK:t * BLOCK] = jnp.where(lane < BLOCK - 1, prev, r)
        prev = r
    o_ref[i, chans, (n_tiles - 1) * BLOCK:n_tiles * BLOCK] = jnp.where(lane < BLOCK - 1, prev, new_b)


def _decode_cache(c_ref, o_ref, i, heads, k_new, v_new, dil):
    hd = HEAD_DIM
    n = c_ref.shape[2]
    n_tiles = n // BLOCK
    s_tiles = [[None] * (n_tiles + 1) for _ in heads]

    def scores(t, x):
        for qi, (g, q_b, _) in enumerate(heads):
            s_tiles[qi][t] = jnp.sum(x[g * hd:(g + 1) * hd, :] * q_b, axis=0, keepdims=True)

    _shift_pass(c_ref, o_ref, i, 0, k_new, scores)
    scores(n_tiles, k_new)

    lane = lax.broadcasted_iota(jnp.int32, (1, n + BLOCK), 1)
    valid = jnp.logical_or(jnp.logical_and(lane < n, (lane & (dil - 1)) == 0), lane == n + BLOCK - 1)
    probs, dens, lses = [], [], []
    for qi, (g, q_b, sink) in enumerate(heads):
        s = jnp.where(valid, jnp.concatenate(s_tiles[qi], axis=1), NEG_INF)
        m = jnp.max(s, axis=1, keepdims=True)
        if sink is not None:
            m = jnp.maximum(m, sink)
        p = jnp.exp(s - m)
        den = jnp.sum(p, axis=1, keepdims=True)
        if sink is not None:
            den = den + jnp.exp(sink - m)
        probs.append(p)
        dens.append(den)
        lses.append(m + jnp.log(den))

    accs = [jnp.zeros((hd, BLOCK), F32) for _ in heads]

    def weighted(t, x):
        for qi, (g, _, _) in enumerate(heads):
            accs[qi] = accs[qi] + x[g * hd:(g + 1) * hd, :] * probs[qi][:, t * BLOCK:(t + 1) * BLOCK]

    _shift_pass(c_ref, o_ref, i, 2 * hd, v_new, weighted)
    weighted(n_tiles, v_new)

    outs = [jnp.sum(accs[qi], axis=1, keepdims=True) / dens[qi] for qi in range(len(heads))]
    return outs, lses


XT_KB, XT_VB, XT_KD, XT_VD = 0, BLOCK, 2 * BLOCK, 2 * BLOCK + MIX
XT_ROWS = 2 * BLOCK + 2 * MIX


def _decode_prep_body(pa_ref, pb_ref, pc_ref, pd_ref, st_ref, cw_ref, lg_ref, lb_ref,
                      wd_ref, bd_ref, cos_ref, sin_ref,
                      ya_ref, yc_ref, nst_ref, vn_ref, xq_ref, xt_ref):
    a_b = pa_ref[:, 0:MIX]
    z = pa_ref[:, MIX:2 * MIX] * pa_ref[:, 2 * MIX:3 * MIX]
    prev0 = st_ref[:, 0:MIX]
    prev1 = st_ref[:, MIX:2 * MIX]
    conv = cw_ref[0:1, :] * prev0 + cw_ref[1:2, :] * prev1 + cw_ref[2:3, :] * z
    ya_ref[...] = a_b * conv * _silu(pa_ref[:, 3 * MIX:4 * MIX])
    nst_ref[:, 0:MIX] = prev1
    nst_ref[:, MIX:2 * MIX] = z

    c_v = pc_ref[:, MIX:2 * MIX]
    mu = jnp.mean(c_v, axis=-1, keepdims=True)
    cen = c_v - mu
    var = jnp.mean(cen * cen, axis=-1, keepdims=True)
    vn = cen * lax.rsqrt(var + EPS) * lg_ref[...] + lb_ref[...]
    vn_ref[...] = vn
    mixed = wd_ref[...] * vn + bd_ref[...]
    yc_ref[...] = pc_ref[:, 0:MIX] * mixed * _silu(pc_ref[:, 2 * MIX:3 * MIX])

    cos = cos_ref[...]
    sin = sin_ref[...]
    cos3, sin3 = _tile_lanes(cos, 3), _tile_lanes(sin, 3)
    xq_ref[0:MIX, :] = (_rope(pb_ref[:, 0:MIX], cos3, sin3) * SCALE).T.astype(BF16)
    xq_ref[MIX:2 * MIX, :] = (_rope(pd_ref[:, 0:MIX], cos3, sin3) * SCALE).T.astype(BF16)
    xt_ref[XT_KB:XT_VB, :] = _rope(pb_ref[:, MIX:MIX + BLOCK], cos, sin).T
    xt_ref[XT_VB:XT_KD, :] = pb_ref[:, MIX + BLOCK:MIX + 2 * BLOCK].T
    xt_ref[XT_KD:XT_VD, :] = _rope(pd_ref[:, MIX:2 * MIX], cos3, sin3).T
    xt_ref[XT_VD:XT_ROWS, :] = pd_ref[:, 2 * MIX:3 * MIX].T


def _decode_prep(pa, pb, pc, pd, state, conv_w, ln_g, ln_b, w_diag, b_diag, cos, sin):
    nb = pa.shape[0]
    assert nb == BLOCK
    shapes = [((nb, MIX), F32), ((nb, MIX), F32), ((nb, 2 * MIX), F32), ((nb, MIX), F32),
              ((2 * MIX, nb), BF16), ((XT_ROWS, nb), F32)]
    return pl.pallas_call(
        _decode_prep_body,
        out_shape=[jax.ShapeDtypeStruct(s, d) for s, d in shapes],
        name="decode_prep",
    )(pa, pb, pc, pd, state, conv_w, ln_g, ln_b, w_diag, b_diag, cos, sin)


def _decode_sequences(dec, step, sink_ref, xq_ref, xt_ref, cache_refs, out_refs):
    cs_ref, c1_ref, c4_ref, c16_ref = cache_refs
    ns_ref, n1_ref, n4_ref, n16_ref, yt_ref = out_refs
    n_seq = xq_ref.shape[1]
    hd = HEAD_DIM

    for i in range(dec.seq_per_step):
        b = dec.seq_base + step * dec.seq_per_step + i
        onehot = (lax.broadcasted_iota(jnp.int32, (n_seq, BLOCK), 0) == b).astype(BF16)
        q_b = _dot(xq_ref[...], onehot)
        new_rot = pltpu.roll(xt_ref[...], BLOCK - 1 - b, 1)
        heads = [(h // SWA_REP, q_b[h * hd:(h + 1) * hd], jnp.full((1, 1), sink_ref[h], F32))
                 for h in range(SWA_Q_HEADS)]
        outs, _ = _decode_cache(cs_ref, ns_ref, i, heads, new_rot[XT_KB:XT_VB], new_rot[XT_VB:XT_KD], 1)
        yb_col = jnp.concatenate(outs, axis=0)

        q_col = q_b[MIX:2 * MIX]
        k_new = new_rot[XT_KD:XT_VD]
        v_new = new_rot[XT_VD:XT_ROWS]
        o_heads, l_heads = [], []
        for g, (c_ref, n_ref, dil) in enumerate(zip((c1_ref, c4_ref, c16_ref),
                                                    (n1_ref, n4_ref, n16_ref), DILATIONS)):
            gc = slice(2 * g * hd, 2 * (g + 1) * hd)
            heads = [(hh, q_col[(2 * g + hh) * hd:(2 * g + hh + 1) * hd], None) for hh in range(2)]
            o, lse = _decode_cache(c_ref, n_ref, i, heads, k_new[gc], v_new[gc], dil)
            o_heads += o
            l_heads += lse
        cols = []
        for hh in range(2):
            ls = [l_heads[2 * g + hh] for g in range(3)]
            lmax = jnp.maximum(jnp.maximum(ls[0], ls[1]), ls[2])
            es = [jnp.exp(l - lmax) for l in ls]
            esum = es[0] + es[1] + es[2]
            cols.append([o_heads[2 * g + hh] * (es[g] / esum) for g in range(3)])
        yd_col = jnp.concatenate([cols[hh][g] for g in range(3) for hh in range(2)], axis=0)
        y_col = jnp.concatenate([yb_col, yd_col], axis=0)
        lane_y = lax.broadcasted_iota(jnp.int32, (2 * MIX, n_seq), 1)
        yt_ref[...] = jnp.where(lane_y == b, y_col, yt_ref[...])


def _decode_fin_body(yt0_ref, yt1_ref, pb_ref, pd_ref, yb_ref, yd_ref, *, split):
    lane = lax.broadcasted_iota(jnp.int32, yt0_ref.shape, 1)
    yt = jnp.where(lane < split, yt0_ref[...], yt1_ref[...])
    yb_ref[...] = yt[0:MIX, :].T * _silu(pb_ref[:, MIX + 2 * BLOCK:COLS_B])
    yd_ref[...] = yt[MIX:2 * MIX, :].T * _silu(pd_ref[:, 3 * MIX:4 * MIX])


def _decode_fin(yt0, yt1, pb, pd, split):
    nb = pb.shape[0]
    return pl.pallas_call(
        functools.partial(_decode_fin_body, split=split),
        out_shape=[jax.ShapeDtypeStruct((nb, MIX), F32)] * 2,
        name="decode_fin",
    )(yt0, yt1, pb, pd)


class _DecodeStatic(NamedTuple):
    seq_base: int
    seq_per_step: int
    n_alias: int


class _Decode:
    def __init__(self, layer, seq_base, seq_per_step, n_sub, sinks, xq, xt, caches, prev_outs):
        self.layer, self.seq_base, self.seq_per_step, self.n_sub = layer, seq_base, seq_per_step, n_sub
        self.sinks, self.xq, self.xt = sinks, xq, xt
        self.caches, self.prev_outs = tuple(caches), tuple(prev_outs)
        self.static = _DecodeStatic(seq_base, seq_per_step, len(self.prev_outs))

    def extend(self, in_specs, out_specs, out_shape, args):
        layer, block0, n_sub = self.layer, self.seq_base // self.seq_per_step, self.n_sub
        linear = (lambda i: i) if n_sub == 1 else (lambda i, j: i * n_sub + j)
        spec = lambda c: pl.BlockSpec((None, self.seq_per_step) + c.shape[2:],
                                      lambda *ids: (layer, block0 + linear(*ids), 0, 0))
        first_alias_in = 1 + len(args) + 2 + len(self.caches)
        aliases = {first_alias_in + i: len(out_specs) + i for i in range(len(self.prev_outs))}
        in_specs = ([pl.BlockSpec(memory_space=pltpu.SMEM)] + in_specs
                    + [_const_spec(self.xq.shape), _const_spec(self.xt.shape)]
                    + [spec(c) for c in self.caches]
                    + [pl.BlockSpec(memory_space=pl.ANY) for _ in self.prev_outs])
        args = [self.sinks] + args + [self.xq, self.xt, *self.caches, *self.prev_outs]
        yt_shape = (2 * MIX, self.xq.shape[1])
        out_specs = out_specs + [spec(c) for c in self.caches] + [_const_spec(yt_shape)]
        out_shape = (out_shape + [jax.ShapeDtypeStruct(c.shape, F32) for c in self.caches]
                     + [jax.ShapeDtypeStruct(yt_shape, F32)])
        rows = sum(c.shape[3] for c in self.caches)
        extra = 4 * self.seq_per_step * KV_CH * rows * 4 + 4 * (2 * MIX + XT_ROWS) * BLOCK * 4
        return in_specs, out_specs, out_shape, args, aliases, extra


def _rope_tables(pos):
    inv_freq = ROPE_THETA ** (-jnp.arange(HALF_HEAD, dtype=F32) / HALF_HEAD)
    ang = pos.astype(F32)[:, None] * inv_freq[None, :]
    cos, sin = jnp.cos(ang), jnp.sin(ang)
    return (jnp.concatenate([cos, cos, cos, cos], axis=1),
            jnp.concatenate([-sin, sin, -sin, sin], axis=1))


def _cache_view(c):
    d, b, n = c.shape[:3]
    return jnp.transpose(c, (0, 1, 3, 4, 5, 2)).reshape(d, b, KV_CH, n)


def _cache_unview(c):
    lead, n = c.shape[:-2], c.shape[-1]
    nl = len(lead)
    perm = tuple(range(nl)) + (nl + 3, nl, nl + 1, nl + 2)
    return jnp.transpose(c.reshape(lead + (2, 2, HEAD_DIM, n)), perm)


def kernel(x_prompt, x_sample, state_conv, cache_swa_kv, cache_dil1_kv, cache_dil4_kv, cache_dil16_kv,
           norm_g, w_in, conv_w, attn_sinks, v_ln_g, v_ln_b, w_spatial, b_spatial,
           w_branch, w_merge, w_out, final_norm_g):
    depth = w_in.shape[0]
    nb, seq, _ = x_prompt.shape
    ns, dec_seq, _ = x_sample.shape
    assert dec_seq == 1 and seq % SUPER == 0
    past_len = 16384
    assert cache_swa_kv.shape[2] == BLOCK and cache_dil16_kv.shape[2] == SUPER

    cos_p, sin_p = _rope_tables(jnp.arange(seq, dtype=jnp.int32))
    cos_s, sin_s = _rope_tables(past_len + jnp.arange(1, dtype=jnp.int32))
    w_in_bf = w_in.astype(BF16)
    w_merge_bf = w_merge.astype(BF16)
    w_branch_bf = w_branch.astype(BF16)
    w_out_bf = w_out.astype(BF16)
    caches = [_cache_view(c) for c in (cache_swa_kv, cache_dil1_kv, cache_dil4_kv, cache_dil16_kv)]
    fg = final_norm_g.reshape(1, D_MODEL)

    xp = x_prompt.reshape(nb * seq, D_MODEL)
    xs = x_sample.reshape(ns, D_MODEL)
    conv_p, conv_s, chunk_v = [], [], []
    new_p = [[] for _ in range(4)]
    new_s = ()
    for l in range(depth):
        g = norm_g[l].reshape(1, D_MODEL)
        final = l == depth - 1
        ln_g = v_ln_g[l].reshape(1, MIX)
        ln_b = v_ln_b[l].reshape(1, MIX)

        sa, sb, sc, sd = _inproj(xs, g, w_in_bf, l, ns, F32)
        w_diag = jnp.repeat(w_spatial[l][:, 0, 0], HEAD_DIM).reshape(1, MIX)
        b_diag = jnp.repeat(b_spatial[l][:, 0], HEAD_DIM).reshape(1, MIX)
        za, zc, nstate, vn, xq, xt = _decode_prep(sa, sb, sc, sd, state_conv[l].reshape(ns, 2 * MIX),
                                                  conv_w[l], ln_g, ln_b, w_diag, b_diag, cos_s, sin_s)

        tm = 512
        tiles = nb * seq // tm
        half = tiles * N_SUB
        assert ns - half == 2 * tiles
        b_tile = jnp.repeat(b_spatial[l].T, HEAD_DIM, axis=1)
        dec = _Decode(l, 0, 1, N_SUB, attn_sinks[l], xq, xt, caches, new_s)
        ya, yb, yc, pd, nconv, kv_swa, *new_s, yt0 = _inproj_mix(
            xp, g, w_in_bf, l, cos_p, sin_p, conv_w[l], ln_g, ln_b, w_spatial[l], b_tile, tm, seq, dec)
        yd, kv1, kv4, kv16 = _mix_d(pd.reshape(nb, seq, COLS_D), cos_p, sin_p)
        dec = _Decode(l, half, 2, 1, attn_sinks[l], xq, xt, caches, new_s)
        xp, *new_s, yt1 = _outproj(xp, (ya, yb, yc, yd.reshape(nb * seq, MIX)), g, w_merge_bf,
                                   w_branch_bf, w_out_bf, fg, l, tm, final, dec)
        conv_p.append(nconv)
        new_p[0].append(_cache_unview(kv_swa))
        for i, kv in enumerate((kv1, kv4, kv16)):
            new_p[i + 1].append(_cache_unview(kv))

        zb, zd = _decode_fin(yt0, yt1, sb, sd, half)
        xs, = _outproj(xs, (za, zb, zc, zd), g, w_merge_bf, w_branch_bf, w_out_bf, fg, l, ns, final)
        conv_s.append(nstate.reshape(ns, 2, MIX))
        chunk_v.append(vn.reshape(ns, 1, MIX))

    st = jnp.stack
    new_s = [_cache_unview(c) for c in new_s]
    return (xp.reshape(nb, seq, D_MODEL), xs.reshape(ns, 1, D_MODEL),
            st(conv_p), st(conv_s),
            st(new_p[0]), new_s[0], st(new_p[1]), new_s[1],
            st(new_p[2]), new_s[2], st(new_p[3]), new_s[3],
            st(chunk_v))
```

```python
import functools
from typing import NamedTuple

import jax
import jax.numpy as jnp
from jax import lax
from jax.experimental import pallas as pl
from jax.experimental.pallas import tpu as pltpu

F32 = jnp.float32
BF16 = jnp.bfloat16

D_MODEL = 1024
HEAD_DIM = 64
HALF_HEAD = HEAD_DIM // 2
MIX = 384
BLOCK = 128
SWA_Q_HEADS = 6
SWA_KV_HEADS = 2
SWA_REP = SWA_Q_HEADS // SWA_KV_HEADS
DILATIONS = (1, 4, 16)
SUPER = BLOCK * DILATIONS[-1]
BLOCKS_PER_TRIP = (16, 16, 8)
ROW_CHUNK = 256
ROPE_THETA = 10000.0
EPS = 1e-6
NEG_INF = -1e30
SCALE = HEAD_DIM ** -0.5

COLS_A = 4 * MIX
COLS_B = MIX + 2 * BLOCK + MIX
COLS_C = 3 * MIX
COLS_D = 4 * MIX
OFF_A, OFF_B, OFF_C, OFF_D = 0, COLS_A, COLS_A + COLS_B, COLS_A + COLS_B + COLS_C
IN_COLS = OFF_D + COLS_D
KV_CH = 4 * HEAD_DIM

V7X_VMEM_BYTES = 64 * 1024 * 1024
MATMUL_COL_CHUNK = 512


def _vmem_limit(estimate_bytes):
    return int(min(V7X_VMEM_BYTES - 8 * 1024 * 1024, estimate_bytes + 16 * 1024 * 1024))


def _const_spec(shape, single=False):
    nd = len(shape)
    if single:
        return pl.BlockSpec(shape, lambda *_: (0,) * nd, pipeline_mode=pl.Buffered(1))
    return pl.BlockSpec(shape, lambda *_: (0,) * nd)


def _layer_spec(stacked_shape, layer):
    nd = len(stacked_shape) - 1
    return pl.BlockSpec((None,) + tuple(stacked_shape[1:]), lambda *_: (layer,) + (0,) * nd,
                        pipeline_mode=pl.Buffered(1))


def _rms(x, g):
    ms = jnp.mean(x * x, axis=-1, keepdims=True)
    return x * lax.rsqrt(ms + EPS) * g


def _sigmoid(x):
    return 0.5 * jnp.tanh(0.5 * x) + 0.5


def _silu(x):
    h = 0.5 * x
    return h * jnp.tanh(h) + h


def _rope(x, cos, sin):
    w = x.shape[-1]
    lane = lax.broadcasted_iota(jnp.int32, x.shape, 1)
    fwd = pltpu.roll(x, w - HALF_HEAD, 1)
    bwd = pltpu.roll(x, HALF_HEAD, 1)
    partner = jnp.where((lane & HALF_HEAD) == 0, fwd, bwd)
    return x * cos + partner * sin


def _tile_lanes(t, reps):
    return t if reps == 1 else jnp.concatenate([t] * reps, axis=1)


def _dot(a, b):
    return jnp.dot(a, b, preferred_element_type=F32)


def _dot_nt(a, b):
    return lax.dot_general(a, b, (((1,), (1,)), ((), ())), preferred_element_type=F32)


def _project(h, w_ref, off, width, store):
    for j in range(0, width, MATMUL_COL_CHUNK):
        cw = min(MATMUL_COL_CHUNK, width - j)
        store(j, cw, _dot(h, w_ref[:, off + j:off + j + cw]))


def _outproj_body(*refs, final, dec):
    sink_ref = refs[0]
    (x_ref, ya_ref, yb_ref, yc_ref, yd_ref, g_ref, wm_ref, wb_ref, wo_ref, fg_ref) = refs[1:11]
    xq_ref, xt_ref = refs[11:13]
    o_ref = refs[17 + dec.n_alias]
    step = pl.program_id(0)
    dec_outs = refs[18 + dec.n_alias:]

    @pl.when(step == 0)
    def _():
        dec_outs[4][...] = jnp.zeros_like(dec_outs[4])

    _decode_sequences(dec, step, sink_ref, xq_ref, xt_ref, refs[13:17], dec_outs)
    ys = (ya_ref[...], yb_ref[...], yc_ref[...], yd_ref[...])
    o_ref[...] = _merge_project(x_ref[...], ys, g_ref, wm_ref, wb_ref, wo_ref, fg_ref, final)


def _merge_project(x, ys, g_ref, wm_ref, wb_ref, wo_ref, fg_ref, final):
    h = _rms(x, g_ref[...]).astype(BF16)
    merged = None
    for n, y in enumerate(ys):
        gate = _sigmoid(_dot(h, wm_ref[:, n * D_MODEL:(n + 1) * D_MODEL]))
        term = gate * _dot(y.astype(BF16), wb_ref[n])
        merged = term if merged is None else merged + term
    out = x + _dot(merged.astype(BF16), wo_ref[...])
    return _rms(out, fg_ref[...]) if final else out


def _outproj_sample_body(x_ref, ya_ref, yc_ref, yt0_ref, yt1_ref, gb_ref, gd_ref,
                         g_ref, wm_ref, wb_ref, wo_ref, fg_ref, o_ref, *, final, split):
    lane = lax.broadcasted_iota(jnp.int32, yt0_ref.shape, 1)
    yt = jnp.where(lane < split, yt0_ref[...], yt1_ref[...])
    yb = yt[0:MIX, :].T * _silu(gb_ref[...])
    yd = yt[MIX:2 * MIX, :].T * _silu(gd_ref[...])
    ys = (ya_ref[...], yb, yc_ref[...], yd)
    o_ref[...] = _merge_project(x_ref[...], ys, g_ref, wm_ref, wb_ref, wo_ref, fg_ref, final)


def _outproj_sample(x, ya, yc, yt0, yt1, gb, gd, g, wm, wb, wo, fg, layer, final, split):
    ns = x.shape[0]
    est = (4 * D_MODEL * D_MODEL + 4 * MIX * D_MODEL + D_MODEL * D_MODEL) * 2 + 24 * ns * D_MODEL * 4
    whole = lambda a: _const_spec(a.shape)
    return pl.pallas_call(
        functools.partial(_outproj_sample_body, final=final, split=split),
        grid=(1,),
        in_specs=[whole(a) for a in (x, ya, yc, yt0, yt1, gb, gd, g)]
                 + [_layer_spec(wm.shape, layer), _layer_spec(wb.shape, layer), _layer_spec(wo.shape, layer),
                    whole(fg)],
        out_specs=_const_spec((ns, D_MODEL)),
        out_shape=jax.ShapeDtypeStruct((ns, D_MODEL), F32),
        compiler_params=pltpu.CompilerParams(dimension_semantics=("arbitrary",),
                                             vmem_limit_bytes=_vmem_limit(est)),
        name="outproj_sample",
    )(x, ya, yc, yt0, yt1, gb, gd, g, wm, wb, wo, fg)


def _outproj(x, ys, g, wm, wb, wo, fg, layer, tm, final, dec):
    t = x.shape[0]
    ysize = jnp.dtype(ys[0].dtype).itemsize
    est = (4 * tm * D_MODEL * 4 + 2 * 4 * tm * MIX * ysize
           + (4 * D_MODEL * D_MODEL + 4 * MIX * D_MODEL + D_MODEL * D_MODEL) * 2
           + 4 * tm * D_MODEL * 4)
    row = lambda i: (i, 0)
    in_specs = ([pl.BlockSpec((tm, D_MODEL), row)]
                + [pl.BlockSpec((tm, MIX), row)] * 4
                + [_const_spec((1, D_MODEL)),
                   _layer_spec(wm.shape, layer), _layer_spec(wb.shape, layer), _layer_spec(wo.shape, layer),
                   _const_spec((1, D_MODEL))])
    out_specs = [pl.BlockSpec((tm, D_MODEL), row)]
    out_shape = [jax.ShapeDtypeStruct((t, D_MODEL), F32)]
    args = [x, *ys, g, wm, wb, wo, fg]
    in_specs, out_specs, out_shape, args, aliases, extra = dec.extend(in_specs, out_specs, out_shape, args)
    return pl.pallas_call(
        functools.partial(_outproj_body, final=final, dec=dec.static),
        grid=(t // tm,),
        in_specs=in_specs, out_specs=out_specs, out_shape=out_shape,
        input_output_aliases=aliases,
        compiler_params=pltpu.CompilerParams(dimension_semantics=("arbitrary",),
                                             vmem_limit_bytes=_vmem_limit(est + extra)),
        name="outproj_dec",
    )(*args)


def _mix_a(pa_ref, cw_ref, ya_ref, prev2, prev1):
    row = lax.broadcasted_iota(jnp.int32, (BLOCK, MIX), 0)
    for ck in range(pa_ref.shape[0] // BLOCK):
        rows = slice(ck * BLOCK, (ck + 1) * BLOCK)
        a_b = pa_ref[rows, 0:MIX]
        z = pa_ref[rows, MIX:2 * MIX] * pa_ref[rows, 2 * MIX:3 * MIX]
        a_gate = pa_ref[rows, 3 * MIX:4 * MIX]
        z1 = jnp.where(row == 0, prev1, pltpu.roll(z, 1, 0))
        z2 = jnp.where(row == 0, prev2, jnp.where(row == 1, prev1, pltpu.roll(z, 2, 0)))
        conv = cw_ref[0:1, :] * z2 + cw_ref[1:2, :] * z1 + cw_ref[2:3, :] * z
        ya_ref[rows, :] = (a_b * conv * _silu(a_gate)).astype(ya_ref.dtype)
        prev2 = z[BLOCK - 2:BLOCK - 1, :]
        prev1 = z[BLOCK - 1:BLOCK, :]
    return prev2, prev1


def _mix_c(pc_ref, col0, lg_ref, lb_ref, ws_ref, bs_ref, yc_ref):
    r = lax.broadcasted_iota(jnp.int32, (BLOCK, BLOCK), 0)
    c = lax.broadcasted_iota(jnp.int32, (BLOCK, BLOCK), 1)
    n_groups = MIX // HEAD_DIM
    w_causal = [jnp.where(r >= c, ws_ref[g], 0.0).astype(BF16) for g in range(n_groups)]
    for ck in range(pc_ref.shape[0] // BLOCK):
        rows = slice(ck * BLOCK, (ck + 1) * BLOCK)
        c_v = pc_ref[rows, col0 + MIX:col0 + 2 * MIX]
        mu = jnp.mean(c_v, axis=-1, keepdims=True)
        cen = c_v - mu
        var = jnp.mean(cen * cen, axis=-1, keepdims=True)
        vn_c = (cen * lax.rsqrt(var + EPS) * lg_ref[...] + lb_ref[...]).astype(BF16)
        mixed = jnp.concatenate(
            [_dot(w_causal[g], vn_c[:, g * HEAD_DIM:(g + 1) * HEAD_DIM]) for g in range(n_groups)],
            axis=1) + bs_ref[...]
        c_u = pc_ref[rows, col0:col0 + MIX]
        c_gate = pc_ref[rows, col0 + 2 * MIX:col0 + 3 * MIX]
        yc_ref[rows, :] = (c_u * mixed * _silu(c_gate)).astype(yc_ref.dtype)


def _band_mask(first_block):
    i = lax.broadcasted_iota(jnp.int32, (BLOCK, 2 * BLOCK), 0)
    j = lax.broadcasted_iota(jnp.int32, (BLOCK, 2 * BLOCK), 1)
    band = (j >= i) & (j <= i + BLOCK)
    return band & jnp.logical_or(jnp.logical_not(first_block), j >= BLOCK)


def _mix_b_prepare(pb_ref, cos_ref, sin_ref, kv_ref, q_s, hist):
    k_s, v_s, ksw_s, vsw_s = hist
    ts = pb_ref.shape[0]
    for r0 in range(0, ts, ROW_CHUNK):
        rows = slice(r0, r0 + ROW_CHUNK)
        dst = slice(BLOCK + r0, BLOCK + r0 + ROW_CHUNK)
        cos = cos_ref[rows, :]
        sin = sin_ref[rows, :]
        q_s[rows, :] = (_rope(pb_ref[rows, 0:MIX], _tile_lanes(cos, 3), _tile_lanes(sin, 3)) * SCALE
                        ).astype(BF16)
        k = _rope(pb_ref[rows, MIX:MIX + BLOCK], cos, sin)
        v = pb_ref[rows, MIX + BLOCK:MIX + 2 * BLOCK]
        k_s[dst, :] = k.astype(BF16)
        v_s[dst, :] = v.astype(BF16)
        ksw_s[dst, :] = pltpu.roll(k, HEAD_DIM, 1).astype(BF16)
        vsw_s[dst, :] = pltpu.roll(v, HEAD_DIM, 1).astype(BF16)
        if r0 + ROW_CHUNK == ts:
            kv_ref[0:BLOCK, :] = k[ROW_CHUNK - BLOCK:ROW_CHUNK, :].T
            kv_ref[BLOCK:2 * BLOCK, :] = v[ROW_CHUNK - BLOCK:ROW_CHUNK, :].T


def _mix_b_attend(sink_ref, pb_ref, first_tile, yb_ref, q_s, hist):
    k_s, v_s, ksw_s, vsw_s = hist
    ts = pb_ref.shape[0]
    half0 = lax.broadcasted_iota(jnp.int32, (BLOCK, BLOCK), 1) < HEAD_DIM
    for c in range(ts // BLOCK):
        rows = slice(c * BLOCK, (c + 1) * BLOCK)
        win = slice(c * BLOCK, (c + 2) * BLOCK)
        kk, vv, kk_sw, vv_sw = k_s[win, :], v_s[win, :], ksw_s[win, :], vsw_s[win, :]
        mask = _band_mask(first_tile) if c == 0 else _band_mask(False)
        outs = []
        for h in range(SWA_Q_HEADS):
            g, chunk, half = h // SWA_REP, h // 2, h % 2
            qc = q_s[rows, chunk * BLOCK:(chunk + 1) * BLOCK]
            q_h = jnp.where(half0 if half == 0 else jnp.logical_not(half0), qc, jnp.zeros_like(qc))
            s = jnp.where(mask, _dot_nt(q_h, kk if half == g else kk_sw), NEG_INF)
            sink = sink_ref[h]
            m = jnp.maximum(jnp.max(s, axis=-1, keepdims=True), sink)
            p = jnp.exp(s - m)
            den = jnp.sum(p, axis=-1, keepdims=True) + jnp.exp(sink - m)
            outs.append(_dot(p.astype(BF16), vv if half == g else vv_sw) / den)
        y = jnp.concatenate([jnp.where(half0, outs[2 * j], outs[2 * j + 1]) for j in range(3)], axis=1)
        gate = pb_ref[rows, MIX + 2 * BLOCK:COLS_B]
        yb_ref[rows, :] = (y * _silu(gate)).astype(yb_ref.dtype)
    for r in hist:
        r[0:BLOCK, :] = r[ts:ts + BLOCK, :]


N_SUB = 2
COLS_BC = COLS_B + COLS_C


def _inproj_mix_body(*refs, dec, tiles_per_seq):
    (sink_ref, x_ref, g_ref, w_ref, cos_ref, sin_ref, cw_ref, lg_ref, lb_ref, ws_ref, bs_ref,
     xq_ref, xt_ref) = refs[:13]
    caches = refs[13:17]
    rest = refs[17 + dec.n_alias:]
    ya_ref, yb_ref, yc_ref, pd_ref, nc_ref, kv_ref = rest[:6]
    dec_outs = rest[6:11]
    h_s, pa_s, pbc_s, tail_s, q_s = rest[11:16]
    hist = rest[16:20]
    tile = pl.program_id(0)
    sub = pl.program_id(1)
    step = tile * N_SUB + sub

    @pl.when(step == 0)
    def _():
        dec_outs[4][...] = jnp.zeros_like(dec_outs[4])
        tail_s[...] = jnp.zeros_like(tail_s)
        for r in hist:
            r[0:BLOCK, :] = jnp.zeros((BLOCK, BLOCK), BF16)

    @pl.when(sub == 0)
    def _():
        h = _rms(x_ref[...], g_ref[...]).astype(BF16)
        h_s[...] = h

        def store(j, cw, acc, base):
            pbc_s[:, base + j:base + j + cw] = acc

        def store_d(j, cw, acc):
            pd_ref[:, j:j + cw] = acc.astype(pd_ref.dtype)

        _project(h, w_ref, OFF_C, COLS_C, functools.partial(store, base=COLS_B))
        _project(h, w_ref, OFF_B, COLS_B, functools.partial(store, base=0))
        _project(h, w_ref, OFF_D, COLS_D, store_d)
        _mix_c(pbc_s, COLS_B, lg_ref, lb_ref, ws_ref, bs_ref, yc_ref)
        _mix_b_prepare(pbc_s, cos_ref, sin_ref, kv_ref, q_s, hist)
        _decode_sequences(dec, step, sink_ref, xq_ref, xt_ref, caches, dec_outs)

    @pl.when(sub == 1)
    def _():
        def store_a(j, cw, acc):
            pa_s[:, j:j + cw] = acc

        _project(h_s[...], w_ref, OFF_A, COLS_A, store_a)

        first_tile = (tile % tiles_per_seq) == 0
        prev2 = jnp.where(first_tile, 0.0, tail_s[0:1, :])
        prev1 = jnp.where(first_tile, 0.0, tail_s[1:2, :])
        prev2, prev1 = _mix_a(pa_s, cw_ref, ya_ref, prev2, prev1)
        last2 = jnp.concatenate([prev2, prev1], axis=0)
        tail_s[...] = last2
        nc_ref[...] = last2
        _mix_b_attend(sink_ref, pbc_s, first_tile, yb_ref, q_s, hist)
        _decode_sequences(dec, step, sink_ref, xq_ref, xt_ref, caches, dec_outs)


def _inproj_mix(x, g, w_bf16, layer, cos, sin, conv_w, ln_g, ln_b, w_s, b_tile, tm, seq, dec):
    t = x.shape[0]
    nb = t // seq
    tiles_per_seq = seq // tm
    assert dec.n_sub == N_SUB
    row = lambda i, j: (i, 0)
    pos = lambda i, j: (i % tiles_per_seq, 0)
    per_seq = lambda i, j: (i // tiles_per_seq, 0, 0)
    in_specs = [pl.BlockSpec((tm, D_MODEL), row), _const_spec((1, D_MODEL)), _layer_spec(w_bf16.shape, layer),
                pl.BlockSpec((tm, BLOCK), pos), pl.BlockSpec((tm, BLOCK), pos),
                _const_spec((3, MIX)), _const_spec((1, MIX)), _const_spec((1, MIX)),
                _const_spec((MIX // HEAD_DIM, BLOCK, BLOCK)), _const_spec((BLOCK, MIX))]
    out_specs = ([pl.BlockSpec((tm, MIX), row)] * 3 + [pl.BlockSpec((tm, COLS_D), row),
                 pl.BlockSpec((None, 2, MIX), per_seq), pl.BlockSpec((None, KV_CH, BLOCK), per_seq)])
    out_shape = ([jax.ShapeDtypeStruct((t, MIX), BF16)] * 3 + [jax.ShapeDtypeStruct((t, COLS_D), BF16),
                 jax.ShapeDtypeStruct((nb, 2, MIX), F32), jax.ShapeDtypeStruct((nb, KV_CH, BLOCK), F32)])
    args = [x, g, w_bf16, cos, sin, conv_w, ln_g, ln_b, w_s, b_tile]
    scratch = [pltpu.VMEM((tm, D_MODEL), BF16), pltpu.VMEM((tm, COLS_A), F32), pltpu.VMEM((tm, COLS_BC), F32),
               pltpu.VMEM((2, MIX), F32), pltpu.VMEM((tm, MIX), BF16)] + [pltpu.VMEM((BLOCK + tm, BLOCK), BF16)] * 4
    est = (2 * tm * D_MODEL * 4 + D_MODEL * IN_COLS * 2 + 2 * tm * (3 * MIX + COLS_D) * 2
           + tm * (D_MODEL * 2 + (COLS_A + COLS_BC) * 4 + MIX * 2) + 4 * (BLOCK + tm) * BLOCK * 2
           + 4 * tm * BLOCK * 4)
    in_specs, out_specs, out_shape, args, aliases, extra = dec.extend(in_specs, out_specs, out_shape, args)
    return pl.pallas_call(
        functools.partial(_inproj_mix_body, dec=dec.static, tiles_per_seq=tiles_per_seq),
        grid=(t // tm, N_SUB),
        in_specs=in_specs, out_specs=out_specs, out_shape=out_shape,
        scratch_shapes=scratch,
        input_output_aliases=aliases,
        compiler_params=pltpu.CompilerParams(dimension_semantics=("arbitrary", "arbitrary"),
                                             vmem_limit_bytes=_vmem_limit(est + extra)),
        name="inproj_mix",
    )(*args)


def _mix_d_body(pd_ref, cos_ref, sin_ref, yd_ref, c1_ref, c4_ref, c16_ref,
                q_ref, k_ref, v_ref, o_ref, m_ref, d_ref):
    sb = pl.program_id(1)

    lw = 2 * HEAD_DIM
    n_groups = len(DILATIONS)

    @pl.when(sb == 0)
    def _():
        k_ref[:, 0:SUPER, :] = jnp.zeros((n_groups, SUPER, lw), F32)
        v_ref[:, 0:SUPER, :] = jnp.zeros((n_groups, SUPER, lw), F32)

    def prepare(c, carry):
        r0 = pl.multiple_of(c * ROW_CHUNK, ROW_CHUNK)
        rows = pl.ds(r0, ROW_CHUNK)
        new_rows = pl.ds(SUPER + r0, ROW_CHUNK)
        cos = cos_ref[rows, :]
        sin = sin_ref[rows, :]
        for g in range(n_groups):
            q_ref[g, rows, :] = _rope(pd_ref[rows, g * lw:(g + 1) * lw].astype(F32), cos, sin) * SCALE
            k_ref[g, new_rows, :] = _rope(pd_ref[rows, MIX + g * lw:MIX + (g + 1) * lw].astype(F32), cos, sin)
            v_ref[g, new_rows, :] = pd_ref[rows, 2 * MIX + g * lw:2 * MIX + (g + 1) * lw].astype(F32)
        return carry

    lax.fori_loop(0, SUPER // ROW_CHUNK, prepare, 0)

    n_blocks = SUPER // BLOCK
    for g, dil in enumerate(DILATIONS):

        def block(t, carry, dil=dil, g=g):
            r = t % dil
            j = t // dil
            base = j * (BLOCK * dil) + r
            qs = q_ref[g, pl.ds(base, BLOCK, stride=dil), :]
            kstart = SUPER + base - BLOCK * dil
            ks = k_ref[g, pl.ds(kstart, 2 * BLOCK, stride=dil), :]
            vs = v_ref[g, pl.ds(kstart, 2 * BLOCK, stride=dil), :]
            mask = _band_mask(jnp.logical_and(sb == 0, j == 0))
            ks_bf = ks.astype(BF16)
            vs_bf = vs.astype(BF16)
            head0 = lax.broadcasted_iota(jnp.int32, (BLOCK, lw), 1) < HEAD_DIM
            o_h, m_h, d_h = [], [], []
            for hh in range(2):
                q_h = jnp.where(head0 if hh == 0 else jnp.logical_not(head0), qs, 0.0).astype(BF16)
                s = jnp.where(mask, _dot_nt(q_h, ks_bf), NEG_INF)
                m = jnp.max(s, axis=-1, keepdims=True)
                p = jnp.exp(s - m)
                o_h.append(_dot(p.astype(BF16), vs_bf))
                m_h.append(m)
                d_h.append(jnp.sum(p, axis=-1, keepdims=True))
            rows = pl.ds(base, BLOCK, stride=dil)
            o_ref[g, rows, :] = jnp.where(head0, o_h[0], o_h[1])
            m_ref[g, rows, :] = jnp.where(head0, m_h[0], m_h[1])
            d_ref[g, rows, :] = jnp.where(head0, d_h[0], d_h[1])
            return carry

        lax.fori_loop(0, n_blocks, block, 0, unroll=BLOCKS_PER_TRIP[g])

    def merge(c, carry):
        rows = pl.ds(pl.multiple_of(c * ROW_CHUNK, ROW_CHUNK), ROW_CHUNK)
        ms = [m_ref[g, rows, :] for g in range(n_groups)]
        mmax = jnp.maximum(jnp.maximum(ms[0], ms[1]), ms[2])
        es = [jnp.exp(m - mmax) for m in ms]
        inv = 1.0 / (d_ref[0, rows, :] * es[0] + d_ref[1, rows, :] * es[1] + d_ref[2, rows, :] * es[2])
        for g in range(n_groups):
            gate = pd_ref[rows, 3 * MIX + g * lw:3 * MIX + (g + 1) * lw].astype(F32)
            y = o_ref[g, rows, :] * (es[g] * inv)
            yd_ref[rows, g * lw:(g + 1) * lw] = (y * _silu(gate)).astype(yd_ref.dtype)
        return carry

    lax.fori_loop(0, SUPER // ROW_CHUNK, merge, 0)

    for g, (c_ref, dil) in enumerate(zip((c1_ref, c4_ref, c16_ref), DILATIONS)):
        n = BLOCK * dil
        for j in range(0, n, ROW_CHUNK if n >= ROW_CHUNK else n):
            w = min(ROW_CHUNK, n)
            src = slice(2 * SUPER - n + j, 2 * SUPER - n + j + w)
            c_ref[0:lw, j:j + w] = k_ref[g, src, :].T
            c_ref[lw:2 * lw, j:j + w] = v_ref[g, src, :].T

    k_ref[:, 0:SUPER, :] = k_ref[:, SUPER:2 * SUPER, :]
    v_ref[:, 0:SUPER, :] = v_ref[:, SUPER:2 * SUPER, :]


def _mix_d(pd, cos, sin):
    b, s, _ = pd.shape
    est = (2 * SUPER * COLS_D * 2 + 4 * SUPER * BLOCK * 4 + 2 * SUPER * MIX * 2
           + 2 * KV_CH * (BLOCK + 4 * BLOCK + SUPER) * 4 + (4 + 2 * 2) * SUPER * MIX * 4)
    tile = lambda bi, si: (bi, si, 0)
    cache = lambda bi, si: (bi, 0, 0)
    return pl.pallas_call(
        _mix_d_body,
        grid=(b, s // SUPER),
        in_specs=[pl.BlockSpec((None, SUPER, COLS_D), tile),
                  pl.BlockSpec((SUPER, BLOCK), lambda bi, si: (si, 0)),
                  pl.BlockSpec((SUPER, BLOCK), lambda bi, si: (si, 0))],
        out_specs=[pl.BlockSpec((None, SUPER, MIX), tile)]
                  + [pl.BlockSpec((None, KV_CH, BLOCK * d), cache) for d in DILATIONS],
        out_shape=[jax.ShapeDtypeStruct((b, s, MIX), BF16)]
                  + [jax.ShapeDtypeStruct((b, KV_CH, BLOCK * d), F32) for d in DILATIONS],
        scratch_shapes=[pltpu.VMEM((3, SUPER, 2 * HEAD_DIM), F32),
                        pltpu.VMEM((3, 2 * SUPER, 2 * HEAD_DIM), F32),
                        pltpu.VMEM((3, 2 * SUPER, 2 * HEAD_DIM), F32),
                        pltpu.VMEM((3, SUPER, 2 * HEAD_DIM), F32),
                        pltpu.VMEM((3, SUPER, 2 * HEAD_DIM), F32),
                        pltpu.VMEM((3, SUPER, 2 * HEAD_DIM), F32)],
        compiler_params=pltpu.CompilerParams(dimension_semantics=("arbitrary", "arbitrary"),
                                             vmem_limit_bytes=_vmem_limit(est)),
        name="mix_d",
    )(pd, cos, sin)


def _shift_pass(c_ref, o_ref, i, ch0, new_b, per_tile):
    n_tiles = c_ref.shape[2] // BLOCK
    chans = pl.ds(ch0, 2 * HEAD_DIM)
    lane = lax.broadcasted_iota(jnp.int32, (2 * HEAD_DIM, BLOCK), 1)
    prev = None
    for t in range(n_tiles):
        x = c_ref[i, chans, t * BLOCK:(t + 1) * BLOCK]
        per_tile(t, x)
        r = pltpu.roll(x, BLOCK - 1, 1)
        if t > 0:
            o_ref[i, chans, (t - 1) * BLOCK:t * BLOCK] = jnp.where(lane < BLOCK - 1, prev, r)
        prev = r
    o_ref[i, chans, (n_tiles - 1) * BLOCK:n_tiles * BLOCK] = jnp.where(lane < BLOCK - 1, prev, new_b)


def _decode_cache(c_ref, o_ref, i, heads, k_new, v_new, dil):
    hd = HEAD_DIM
    n = c_ref.shape[2]
    n_tiles = n // BLOCK
    s_tiles = [[None] * (n_tiles + 1) for _ in heads]

    def scores(t, x):
        for qi, (g, q_b, _) in enumerate(heads):
            s_tiles[qi][t] = jnp.sum(x[g * hd:(g + 1) * hd, :] * q_b, axis=0, keepdims=True)

    _shift_pass(c_ref, o_ref, i, 0, k_new, scores)
    scores(n_tiles, k_new)

    lane = lax.broadcasted_iota(jnp.int32, (1, n + BLOCK), 1)
    valid = jnp.logical_or(jnp.logical_and(lane < n, (lane & (dil - 1)) == 0), lane == n + BLOCK - 1)
    probs, dens, lses = [], [], []
    for qi, (g, q_b, sink) in enumerate(heads):
        s = jnp.where(valid, jnp.concatenate(s_tiles[qi], axis=1), NEG_INF)
        m = jnp.max(s, axis=1, keepdims=True)
        if sink is not None:
            m = jnp.maximum(m, sink)
        p = jnp.exp(s - m)
        den = jnp.sum(p, axis=1, keepdims=True)
        if sink is not None:
            den = den + jnp.exp(sink - m)
        probs.append(p)
        dens.append(den)
        lses.append(m + jnp.log(den))

    accs = [jnp.zeros((hd, BLOCK), F32) for _ in heads]

    def weighted(t, x):
        for qi, (g, _, _) in enumerate(heads):
            accs[qi] = accs[qi] + x[g * hd:(g + 1) * hd, :] * probs[qi][:, t * BLOCK:(t + 1) * BLOCK]

    _shift_pass(c_ref, o_ref, i, 2 * hd, v_new, weighted)
    weighted(n_tiles, v_new)

    outs = [jnp.sum(accs[qi], axis=1, keepdims=True) / dens[qi] for qi in range(len(heads))]
    return outs, lses


XT_KB, XT_VB, XT_KD, XT_VD = 0, BLOCK, 2 * BLOCK, 2 * BLOCK + MIX
XT_ROWS = 2 * BLOCK + 2 * MIX


def _decode_prep_body(pa_ref, pb_ref, pc_ref, pd_ref, st_ref, cw_ref, lg_ref, lb_ref,
                      wd_ref, bd_ref, cos_ref, sin_ref,
                      ya_ref, yc_ref, nst_ref, vn_ref, xq_ref, xt_ref):
    a_b = pa_ref[:, 0:MIX]
    z = pa_ref[:, MIX:2 * MIX] * pa_ref[:, 2 * MIX:3 * MIX]
    prev0 = st_ref[:, 0:MIX]
    prev1 = st_ref[:, MIX:2 * MIX]
    conv = cw_ref[0:1, :] * prev0 + cw_ref[1:2, :] * prev1 + cw_ref[2:3, :] * z
    ya_ref[...] = a_b * conv * _silu(pa_ref[:, 3 * MIX:4 * MIX])
    nst_ref[:, 0:MIX] = prev1
    nst_ref[:, MIX:2 * MIX] = z

    c_v = pc_ref[:, MIX:2 * MIX]
    mu = jnp.mean(c_v, axis=-1, keepdims=True)
    cen = c_v - mu
    var = jnp.mean(cen * cen, axis=-1, keepdims=True)
    vn = cen * lax.rsqrt(var + EPS) * lg_ref[...] + lb_ref[...]
    vn_ref[...] = vn
    mixed = wd_ref[...] * vn + bd_ref[...]
    yc_ref[...] = pc_ref[:, 0:MIX] * mixed * _silu(pc_ref[:, 2 * MIX:3 * MIX])

    cos = cos_ref[...]
    sin = sin_ref[...]
    cos3, sin3 = _tile_lanes(cos, 3), _tile_lanes(sin, 3)
    xq_ref[0:MIX, :] = (_rope(pb_ref[:, 0:MIX], cos3, sin3) * SCALE).T.astype(BF16)
    xq_ref[MIX:2 * MIX, :] = (_rope(pd_ref[:, 0:MIX], cos3, sin3) * SCALE).T.astype(BF16)
    xt_ref[XT_KB:XT_VB, :] = _rope(pb_ref[:, MIX:MIX + BLOCK], cos, sin).T
    xt_ref[XT_VB:XT_KD, :] = pb_ref[:, MIX + BLOCK:MIX + 2 * BLOCK].T
    xt_ref[XT_KD:XT_VD, :] = _rope(pd_ref[:, MIX:2 * MIX], cos3, sin3).T
    xt_ref[XT_VD:XT_ROWS, :] = pd_ref[:, 2 * MIX:3 * MIX].T


def _inproj_sample_body(x_ref, g_ref, w_ref, st_ref, cw_ref, lg_ref, lb_ref, wd_ref, bd_ref, cos_ref, sin_ref,
                        ya_ref, yc_ref, nst_ref, vn_ref, xq_ref, xt_ref, gb_ref, gd_ref,
                        pa_s, pb_s, pc_s, pd_s):
    h = _rms(x_ref[...], g_ref[...]).astype(BF16)
    for p_s, off in zip((pa_s, pb_s, pc_s, pd_s), (OFF_A, OFF_B, OFF_C, OFF_D)):
        def store(j, cw, acc, p_s=p_s):
            p_s[:, j:j + cw] = acc
        _project(h, w_ref, off, p_s.shape[-1], store)
    _decode_prep_body(pa_s, pb_s, pc_s, pd_s, st_ref, cw_ref, lg_ref, lb_ref, wd_ref, bd_ref, cos_ref, sin_ref,
                      ya_ref, yc_ref, nst_ref, vn_ref, xq_ref, xt_ref)
    gb_ref[...] = pb_s[:, MIX + 2 * BLOCK:COLS_B]
    gd_ref[...] = pd_s[:, 3 * MIX:4 * MIX]


def _inproj_sample(x, g, w_bf16, layer, state, conv_w, ln_g, ln_b, w_diag, b_diag, cos, sin):
    ns = x.shape[0]
    assert ns == BLOCK
    shapes = [((ns, MIX), F32), ((ns, MIX), F32), ((ns, 2 * MIX), F32), ((ns, MIX), F32),
              ((2 * MIX, ns), BF16), ((XT_ROWS, ns), F32), ((ns, MIX), F32), ((ns, MIX), F32)]
    small = (state, conv_w, ln_g, ln_b, w_diag, b_diag, cos, sin)
    est = D_MODEL * IN_COLS * 2 + 8 * ns * IN_COLS * 4
    return pl.pallas_call(
        _inproj_sample_body,
        grid=(1,),
        in_specs=[_const_spec(x.shape), _const_spec(g.shape), _layer_spec(w_bf16.shape, layer)]
                 + [_const_spec(a.shape) for a in small],
        out_specs=[_const_spec(s) for s, _ in shapes],
        out_shape=[jax.ShapeDtypeStruct(s, d) for s, d in shapes],
        scratch_shapes=[pltpu.VMEM((ns, wd), F32) for wd in (COLS_A, COLS_B, COLS_C, COLS_D)],
        compiler_params=pltpu.CompilerParams(dimension_semantics=("arbitrary",),
                                             vmem_limit_bytes=_vmem_limit(est)),
        name="inproj_sample",
    )(x, g, w_bf16, *small)


def _decode_sequences(dec, step, sink_ref, xq_ref, xt_ref, cache_refs, out_refs):
    cs_ref, c1_ref, c4_ref, c16_ref = cache_refs
    ns_ref, n1_ref, n4_ref, n16_ref, yt_ref = out_refs
    n_seq = xq_ref.shape[1]
    hd = HEAD_DIM

    for i in range(dec.seq_per_step):
        b = dec.seq_base + step * dec.seq_per_step + i
        onehot = (lax.broadcasted_iota(jnp.int32, (n_seq, BLOCK), 0) == b).astype(BF16)
        q_b = _dot(xq_ref[...], onehot)
        new_rot = pltpu.roll(xt_ref[...], BLOCK - 1 - b, 1)
        heads = [(h // SWA_REP, q_b[h * hd:(h + 1) * hd], jnp.full((1, 1), sink_ref[h], F32))
                 for h in range(SWA_Q_HEADS)]
        outs, _ = _decode_cache(cs_ref, ns_ref, i, heads, new_rot[XT_KB:XT_VB], new_rot[XT_VB:XT_KD], 1)
        yb_col = jnp.concatenate(outs, axis=0)

        q_col = q_b[MIX:2 * MIX]
        k_new = new_rot[XT_KD:XT_VD]
        v_new = new_rot[XT_VD:XT_ROWS]
        o_heads, l_heads = [], []
        for g, (c_ref, n_ref, dil) in enumerate(zip((c1_ref, c4_ref, c16_ref),
                                                    (n1_ref, n4_ref, n16_ref), DILATIONS)):
            gc = slice(2 * g * hd, 2 * (g + 1) * hd)
            heads = [(hh, q_col[(2 * g + hh) * hd:(2 * g + hh + 1) * hd], None) for hh in range(2)]
            o, lse = _decode_cache(c_ref, n_ref, i, heads, k_new[gc], v_new[gc], dil)
            o_heads += o
            l_heads += lse
        cols = []
        for hh in range(2):
            ls = [l_heads[2 * g + hh] for g in range(3)]
            lmax = jnp.maximum(jnp.maximum(ls[0], ls[1]), ls[2])
            es = [jnp.exp(l - lmax) for l in ls]
            esum = es[0] + es[1] + es[2]
            cols.append([o_heads[2 * g + hh] * (es[g] / esum) for g in range(3)])
        yd_col = jnp.concatenate([cols[hh][g] for g in range(3) for hh in range(2)], axis=0)
        y_col = jnp.concatenate([yb_col, yd_col], axis=0)
        lane_y = lax.broadcasted_iota(jnp.int32, (2 * MIX, n_seq), 1)
        yt_ref[...] = jnp.where(lane_y == b, y_col, yt_ref[...])


class _DecodeStatic(NamedTuple):
    seq_base: int
    seq_per_step: int
    n_alias: int


class _Decode:
    def __init__(self, layer, seq_base, seq_per_step, n_sub, sinks, xq, xt, caches, prev_outs):
        self.layer, self.seq_base, self.seq_per_step, self.n_sub = layer, seq_base, seq_per_step, n_sub
        self.sinks, self.xq, self.xt = sinks, xq, xt
        self.caches, self.prev_outs = tuple(caches), tuple(prev_outs)
        self.static = _DecodeStatic(seq_base, seq_per_step, len(self.prev_outs))

    def extend(self, in_specs, out_specs, out_shape, args):
        layer, block0, n_sub = self.layer, self.seq_base // self.seq_per_step, self.n_sub
        linear = (lambda i: i) if n_sub == 1 else (lambda i, j: i * n_sub + j)
        spec = lambda c: pl.BlockSpec((None, self.seq_per_step) + c.shape[2:],
                                      lambda *ids: (layer, block0 + linear(*ids), 0, 0))
        first_alias_in = 1 + len(args) + 2 + len(self.caches)
        aliases = {first_alias_in + i: len(out_specs) + i for i in range(len(self.prev_outs))}
        in_specs = ([pl.BlockSpec(memory_space=pltpu.SMEM)] + in_specs
                    + [_const_spec(self.xq.shape), _const_spec(self.xt.shape)]
                    + [spec(c) for c in self.caches]
                    + [pl.BlockSpec(memory_space=pl.ANY) for _ in self.prev_outs])
        args = [self.sinks] + args + [self.xq, self.xt, *self.caches, *self.prev_outs]
        yt_shape = (2 * MIX, self.xq.shape[1])
        out_specs = out_specs + [spec(c) for c in self.caches] + [_const_spec(yt_shape)]
        out_shape = (out_shape + [jax.ShapeDtypeStruct(c.shape, F32) for c in self.caches]
                     + [jax.ShapeDtypeStruct(yt_shape, F32)])
        rows = sum(c.shape[3] for c in self.caches)
        extra = 4 * self.seq_per_step * KV_CH * rows * 4 + 4 * (2 * MIX + XT_ROWS) * BLOCK * 4
        return in_specs, out_specs, out_shape, args, aliases, extra


def _rope_tables(pos):
    inv_freq = ROPE_THETA ** (-jnp.arange(HALF_HEAD, dtype=F32) / HALF_HEAD)
    ang = pos.astype(F32)[:, None] * inv_freq[None, :]
    cos, sin = jnp.cos(ang), jnp.sin(ang)
    return (jnp.concatenate([cos, cos, cos, cos], axis=1),
            jnp.concatenate([-sin, sin, -sin, sin], axis=1))


def _cache_view(c):
    d, b, n = c.shape[:3]
    return jnp.transpose(c, (0, 1, 3, 4, 5, 2)).reshape(d, b, KV_CH, n)


def _cache_unview(c):
    lead, n = c.shape[:-2], c.shape[-1]
    nl = len(lead)
    perm = tuple(range(nl)) + (nl + 3, nl, nl + 1, nl + 2)
    return jnp.transpose(c.reshape(lead + (2, 2, HEAD_DIM, n)), perm)


def kernel(x_prompt, x_sample, state_conv, cache_swa_kv, cache_dil1_kv, cache_dil4_kv, cache_dil16_kv,
           norm_g, w_in, conv_w, attn_sinks, v_ln_g, v_ln_b, w_spatial, b_spatial,
           w_branch, w_merge, w_out, final_norm_g):
    depth = w_in.shape[0]
    nb, seq, _ = x_prompt.shape
    ns, dec_seq, _ = x_sample.shape
    assert dec_seq == 1 and seq % SUPER == 0
    past_len = 16384
    assert cache_swa_kv.shape[2] == BLOCK and cache_dil16_kv.shape[2] == SUPER

    cos_p, sin_p = _rope_tables(jnp.arange(seq, dtype=jnp.int32))
    cos_s, sin_s = _rope_tables(past_len + jnp.arange(1, dtype=jnp.int32))
    w_in_bf = w_in.astype(BF16)
    w_merge_bf = w_merge.astype(BF16)
    w_branch_bf = w_branch.astype(BF16)
    w_out_bf = w_out.astype(BF16)
    caches = [_cache_view(c) for c in (cache_swa_kv, cache_dil1_kv, cache_dil4_kv, cache_dil16_kv)]
    fg = final_norm_g.reshape(1, D_MODEL)

    xp = x_prompt.reshape(nb * seq, D_MODEL)
    xs = x_sample.reshape(ns, D_MODEL)
    conv_p, conv_s, chunk_v = [], [], []
    new_p = [[] for _ in range(4)]
    new_s = ()
    for l in range(depth):
        g = norm_g[l].reshape(1, D_MODEL)
        final = l == depth - 1
        ln_g = v_ln_g[l].reshape(1, MIX)
        ln_b = v_ln_b[l].reshape(1, MIX)

        w_diag = jnp.repeat(w_spatial[l][:, 0, 0], HEAD_DIM).reshape(1, MIX)
        b_diag = jnp.repeat(b_spatial[l][:, 0], HEAD_DIM).reshape(1, MIX)
        za, zc, nstate, vn, xq, xt, gate_b, gate_d = _inproj_sample(
            xs, g, w_in_bf, l, state_conv[l].reshape(ns, 2 * MIX), conv_w[l], ln_g, ln_b, w_diag, b_diag,
            cos_s, sin_s)

        tm = 512
        tiles = nb * seq // tm
        half = tiles * N_SUB
        assert ns - half == 2 * tiles
        b_tile = jnp.repeat(b_spatial[l].T, HEAD_DIM, axis=1)
        dec = _Decode(l, 0, 1, N_SUB, attn_sinks[l], xq, xt, caches, new_s)
        ya, yb, yc, pd, nconv, kv_swa, *new_s, yt0 = _inproj_mix(
            xp, g, w_in_bf, l, cos_p, sin_p, conv_w[l], ln_g, ln_b, w_spatial[l], b_tile, tm, seq, dec)
        yd, kv1, kv4, kv16 = _mix_d(pd.reshape(nb, seq, COLS_D), cos_p, sin_p)
        dec = _Decode(l, half, 2, 1, attn_sinks[l], xq, xt, caches, new_s)
        xp, *new_s, yt1 = _outproj(xp, (ya, yb, yc, yd.reshape(nb * seq, MIX)), g, w_merge_bf,
                                   w_branch_bf, w_out_bf, fg, l, tm, final, dec)
        conv_p.append(nconv)
        new_p[0].append(_cache_unview(kv_swa))
        for i, kv in enumerate((kv1, kv4, kv16)):
            new_p[i + 1].append(_cache_unview(kv))

        xs = _outproj_sample(xs, za, zc, yt0, yt1, gate_b, gate_d, g, w_merge_bf, w_branch_bf, w_out_bf, fg,
                             l, final, half)
        conv_s.append(nstate.reshape(ns, 2, MIX))
        chunk_v.append(vn.reshape(ns, 1, MIX))

    st = jnp.stack
    new_s = [_cache_unview(c) for c in new_s]
    return (xp.reshape(nb, seq, D_MODEL), xs.reshape(ns, 1, D_MODEL),
            st(conv_p), st(conv_s),
            st(new_p[0]), new_s[0], st(new_p[1]), new_s[1],
            st(new_p[2]), new_s[2], st(new_p[3]), new_s[3],
            st(chunk_v))
```

```python
import functools
from typing import NamedTuple

import jax
import jax.numpy as jnp
from jax import lax
from jax.experimental import pallas as pl
from jax.experimental.pallas import tpu as pltpu

F32 = jnp.float32
BF16 = jnp.bfloat16

D_MODEL = 1024
HEAD_DIM = 64
HALF_HEAD = HEAD_DIM // 2
MIX = 384
BLOCK = 128
SWA_Q_HEADS = 6
SWA_KV_HEADS = 2
SWA_REP = SWA_Q_HEADS // SWA_KV_HEADS
DILATIONS = (1, 4, 16)
SUPER = BLOCK * DILATIONS[-1]
BLOCKS_PER_TRIP = (16, 16, 8)
ROW_CHUNK = 256
ROPE_THETA = 10000.0
EPS = 1e-6
NEG_INF = -1e30
SCALE = HEAD_DIM ** -0.5
LOG2E = 1.4426950408889634
SCALE_LOG2 = SCALE * LOG2E

COLS_A = 4 * MIX
COLS_B = MIX + 2 * BLOCK + MIX
COLS_C = 3 * MIX
COLS_D = 4 * MIX
OFF_A, OFF_B, OFF_C, OFF_D = 0, COLS_A, COLS_A + COLS_B, COLS_A + COLS_B + COLS_C
IN_COLS = OFF_D + COLS_D
KV_CH = 4 * HEAD_DIM

V7X_VMEM_BYTES = 64 * 1024 * 1024
MATMUL_COL_CHUNK = 512


def _vmem_limit(estimate_bytes):
    return int(min(V7X_VMEM_BYTES - 8 * 1024 * 1024, estimate_bytes + 16 * 1024 * 1024))


def _const_spec(shape, single=False):
    nd = len(shape)
    if single:
        return pl.BlockSpec(shape, lambda *_: (0,) * nd, pipeline_mode=pl.Buffered(1))
    return pl.BlockSpec(shape, lambda *_: (0,) * nd)


def _layer_spec(stacked_shape, layer):
    nd = len(stacked_shape) - 1
    return pl.BlockSpec((None,) + tuple(stacked_shape[1:]), lambda *_: (layer,) + (0,) * nd,
                        pipeline_mode=pl.Buffered(1))


def _rms(x, g):
    ms = jnp.mean(x * x, axis=-1, keepdims=True)
    return x * lax.rsqrt(ms + EPS) * g


def _sigmoid(x):
    return 0.5 * jnp.tanh(0.5 * x) + 0.5


def _silu(x):
    h = 0.5 * x
    return h * jnp.tanh(h) + h


def _rope(x, cos, sin):
    w = x.shape[-1]
    lane = lax.broadcasted_iota(jnp.int32, x.shape, 1)
    fwd = pltpu.roll(x, w - HALF_HEAD, 1)
    bwd = pltpu.roll(x, HALF_HEAD, 1)
    partner = jnp.where((lane & HALF_HEAD) == 0, fwd, bwd)
    return x * cos + partner * sin


def _tile_lanes(t, reps):
    return t if reps == 1 else jnp.concatenate([t] * reps, axis=1)


def _dot(a, b):
    return jnp.dot(a, b, preferred_element_type=F32)


def _dot_nt(a, b):
    return lax.dot_general(a, b, (((1,), (1,)), ((), ())), preferred_element_type=F32)


def _project(h, w_ref, off, width, store):
    for j in range(0, width, MATMUL_COL_CHUNK):
        cw = min(MATMUL_COL_CHUNK, width - j)
        store(j, cw, _dot(h, w_ref[:, off + j:off + j + cw]))


def _outproj_body(*refs, final, dec):
    sink_ref = refs[0]
    (x_ref, ya_ref, yb_ref, yc_ref, yd_ref, g_ref, wm_ref, wb_ref, wo_ref, fg_ref) = refs[1:11]
    xq_ref, xt_ref = refs[11:13]
    o_ref = refs[17 + dec.n_alias]
    step = pl.program_id(0)
    dec_outs = refs[18 + dec.n_alias:]

    @pl.when(step == 0)
    def _():
        dec_outs[4][...] = jnp.zeros_like(dec_outs[4])

    _decode_sequences(dec, step, sink_ref, xq_ref, xt_ref, refs[13:17], dec_outs)
    ys = (ya_ref[...], yb_ref[...], yc_ref[...], yd_ref[...])
    o_ref[...] = _merge_project(x_ref[...], ys, g_ref, wm_ref, wb_ref, wo_ref, fg_ref, final)


def _merge_project(x, ys, g_ref, wm_ref, wb_ref, wo_ref, fg_ref, final):
    h = _rms(x, g_ref[...]).astype(BF16)
    merged = None
    for n, y in enumerate(ys):
        gate = _sigmoid(_dot(h, wm_ref[:, n * D_MODEL:(n + 1) * D_MODEL]))
        term = gate * _dot(y.astype(BF16), wb_ref[n])
        merged = term if merged is None else merged + term
    out = x + _dot(merged.astype(BF16), wo_ref[...])
    return _rms(out, fg_ref[...]) if final else out


def _outproj_sample_body(x_ref, ya_ref, yc_ref, yt0_ref, yt1_ref, gb_ref, gd_ref,
                         g_ref, wm_ref, wb_ref, wo_ref, fg_ref, o_ref, *, final, split):
    lane = lax.broadcasted_iota(jnp.int32, yt0_ref.shape, 1)
    yt = jnp.where(lane < split, yt0_ref[...], yt1_ref[...])
    yb = yt[0:MIX, :].T * _silu(gb_ref[...])
    yd = yt[MIX:2 * MIX, :].T * _silu(gd_ref[...])
    ys = (ya_ref[...], yb, yc_ref[...], yd)
    o_ref[...] = _merge_project(x_ref[...], ys, g_ref, wm_ref, wb_ref, wo_ref, fg_ref, final)


def _outproj_sample(x, ya, yc, yt0, yt1, gb, gd, g, wm, wb, wo, fg, layer, final, split):
    ns = x.shape[0]
    est = (4 * D_MODEL * D_MODEL + 4 * MIX * D_MODEL + D_MODEL * D_MODEL) * 2 + 24 * ns * D_MODEL * 4
    whole = lambda a: _const_spec(a.shape)
    return pl.pallas_call(
        functools.partial(_outproj_sample_body, final=final, split=split),
        grid=(1,),
        in_specs=[whole(a) for a in (x, ya, yc, yt0, yt1, gb, gd, g)]
                 + [_layer_spec(wm.shape, layer), _layer_spec(wb.shape, layer), _layer_spec(wo.shape, layer),
                    whole(fg)],
        out_specs=_const_spec((ns, D_MODEL)),
        out_shape=jax.ShapeDtypeStruct((ns, D_MODEL), F32),
        compiler_params=pltpu.CompilerParams(dimension_semantics=("arbitrary",),
                                             vmem_limit_bytes=_vmem_limit(est)),
        name="outproj_sample",
    )(x, ya, yc, yt0, yt1, gb, gd, g, wm, wb, wo, fg)


def _outproj(x, ys, g, wm, wb, wo, fg, layer, tm, final, dec):
    t = x.shape[0]
    ysize = jnp.dtype(ys[0].dtype).itemsize
    est = (4 * tm * D_MODEL * 4 + 2 * 4 * tm * MIX * ysize
           + (4 * D_MODEL * D_MODEL + 4 * MIX * D_MODEL + D_MODEL * D_MODEL) * 2
           + 4 * tm * D_MODEL * 4)
    row = lambda i: (i, 0)
    in_specs = ([pl.BlockSpec((tm, D_MODEL), row)]
                + [pl.BlockSpec((tm, MIX), row)] * 4
                + [_const_spec((1, D_MODEL)),
                   _layer_spec(wm.shape, layer), _layer_spec(wb.shape, layer), _layer_spec(wo.shape, layer),
                   _const_spec((1, D_MODEL))])
    out_specs = [pl.BlockSpec((tm, D_MODEL), row)]
    out_shape = [jax.ShapeDtypeStruct((t, D_MODEL), F32)]
    args = [x, *ys, g, wm, wb, wo, fg]
    in_specs, out_specs, out_shape, args, aliases, extra = dec.extend(in_specs, out_specs, out_shape, args)
    return pl.pallas_call(
        functools.partial(_outproj_body, final=final, dec=dec.static),
        grid=(t // tm,),
        in_specs=in_specs, out_specs=out_specs, out_shape=out_shape,
        input_output_aliases=aliases,
        compiler_params=pltpu.CompilerParams(dimension_semantics=("arbitrary",),
                                             vmem_limit_bytes=_vmem_limit(est + extra)),
        name="outproj_dec",
    )(*args)


def _mix_a(pa_ref, cw_ref, ya_ref, prev2, prev1):
    row = lax.broadcasted_iota(jnp.int32, (BLOCK, MIX), 0)
    for ck in range(pa_ref.shape[0] // BLOCK):
        rows = slice(ck * BLOCK, (ck + 1) * BLOCK)
        a_b = pa_ref[rows, 0:MIX]
        z = pa_ref[rows, MIX:2 * MIX] * pa_ref[rows, 2 * MIX:3 * MIX]
        a_gate = pa_ref[rows, 3 * MIX:4 * MIX]
        z1 = jnp.where(row == 0, prev1, pltpu.roll(z, 1, 0))
        z2 = jnp.where(row == 0, prev2, jnp.where(row == 1, prev1, pltpu.roll(z, 2, 0)))
        conv = cw_ref[0:1, :] * z2 + cw_ref[1:2, :] * z1 + cw_ref[2:3, :] * z
        ya_ref[rows, :] = (a_b * conv * _silu(a_gate)).astype(ya_ref.dtype)
        prev2 = z[BLOCK - 2:BLOCK - 1, :]
        prev1 = z[BLOCK - 1:BLOCK, :]
    return prev2, prev1


def _mix_c(pc_ref, col0, lg_ref, lb_ref, ws_ref, bs_ref, yc_ref):
    r = lax.broadcasted_iota(jnp.int32, (BLOCK, BLOCK), 0)
    c = lax.broadcasted_iota(jnp.int32, (BLOCK, BLOCK), 1)
    n_groups = MIX // HEAD_DIM
    w_causal = [jnp.where(r >= c, ws_ref[g], 0.0).astype(BF16) for g in range(n_groups)]
    for ck in range(pc_ref.shape[0] // BLOCK):
        rows = slice(ck * BLOCK, (ck + 1) * BLOCK)
        c_v = pc_ref[rows, col0 + MIX:col0 + 2 * MIX]
        mu = jnp.mean(c_v, axis=-1, keepdims=True)
        cen = c_v - mu
        var = jnp.mean(cen * cen, axis=-1, keepdims=True)
        vn_c = (cen * lax.rsqrt(var + EPS) * lg_ref[...] + lb_ref[...]).astype(BF16)
        mixed = jnp.concatenate(
            [_dot(w_causal[g], vn_c[:, g * HEAD_DIM:(g + 1) * HEAD_DIM]) for g in range(n_groups)],
            axis=1) + bs_ref[...]
        c_u = pc_ref[rows, col0:col0 + MIX]
        c_gate = pc_ref[rows, col0 + 2 * MIX:col0 + 3 * MIX]
        yc_ref[rows, :] = (c_u * mixed * _silu(c_gate)).astype(yc_ref.dtype)


def _band_mask(first_block):
    i = lax.broadcasted_iota(jnp.int32, (BLOCK, 2 * BLOCK), 0)
    j = lax.broadcasted_iota(jnp.int32, (BLOCK, 2 * BLOCK), 1)
    band = (j >= i) & (j <= i + BLOCK)
    return band & jnp.logical_or(jnp.logical_not(first_block), j >= BLOCK)


def _mix_b_prepare(pb_ref, cos_ref, sin_ref, kv_ref, q_s, hist):
    k_s, v_s, ksw_s, vsw_s = hist
    ts = pb_ref.shape[0]
    for r0 in range(0, ts, ROW_CHUNK):
        rows = slice(r0, r0 + ROW_CHUNK)
        dst = slice(BLOCK + r0, BLOCK + r0 + ROW_CHUNK)
        cos = cos_ref[rows, :]
        sin = sin_ref[rows, :]
        q_s[rows, :] = (_rope(pb_ref[rows, 0:MIX], _tile_lanes(cos, 3), _tile_lanes(sin, 3)) * SCALE_LOG2
                        ).astype(BF16)
        k = _rope(pb_ref[rows, MIX:MIX + BLOCK], cos, sin)
        v = pb_ref[rows, MIX + BLOCK:MIX + 2 * BLOCK]
        k_s[dst, :] = k.astype(BF16)
        v_s[dst, :] = v.astype(BF16)
        ksw_s[dst, :] = pltpu.roll(k, HEAD_DIM, 1).astype(BF16)
        vsw_s[dst, :] = pltpu.roll(v, HEAD_DIM, 1).astype(BF16)
        if r0 + ROW_CHUNK == ts:
            kv_ref[0:BLOCK, :] = k[ROW_CHUNK - BLOCK:ROW_CHUNK, :].T
            kv_ref[BLOCK:2 * BLOCK, :] = v[ROW_CHUNK - BLOCK:ROW_CHUNK, :].T


def _mix_b_attend(sink_ref, pb_ref, first_tile, yb_ref, q_s, hist):
    k_s, v_s, ksw_s, vsw_s = hist
    ts = pb_ref.shape[0]
    half0 = lax.broadcasted_iota(jnp.int32, (BLOCK, BLOCK), 1) < HEAD_DIM
    bias_rest = jnp.where(_band_mask(False), 0.0, NEG_INF)
    bias_first = jnp.where(_band_mask(first_tile), 0.0, NEG_INF)
    for c in range(ts // BLOCK):
        rows = slice(c * BLOCK, (c + 1) * BLOCK)
        win = slice(c * BLOCK, (c + 2) * BLOCK)
        kk, vv, kk_sw, vv_sw = k_s[win, :], v_s[win, :], ksw_s[win, :], vsw_s[win, :]
        bias = bias_first if c == 0 else bias_rest
        outs, dens = [], []
        for h in range(SWA_Q_HEADS):
            g, chunk, half = h // SWA_REP, h // 2, h % 2
            qc = q_s[rows, chunk * BLOCK:(chunk + 1) * BLOCK]
            q_h = jnp.where(half0 if half == 0 else jnp.logical_not(half0), qc, jnp.zeros_like(qc))
            s = _dot_nt(q_h, kk if half == g else kk_sw) + bias
            sink = sink_ref[h] * LOG2E
            m = jnp.maximum(jnp.max(s, axis=-1, keepdims=True), sink)
            p = jnp.exp2(s - m)
            dens.append(jnp.sum(p, axis=-1, keepdims=True) + jnp.exp2(sink - m))
            outs.append(_dot(p.astype(BF16), vv if half == g else vv_sw))
        y = jnp.concatenate([jnp.where(half0, outs[2 * j], outs[2 * j + 1])
                             / jnp.where(half0, dens[2 * j], dens[2 * j + 1]) for j in range(3)], axis=1)
        gate = pb_ref[rows, MIX + 2 * BLOCK:COLS_B]
        yb_ref[rows, :] = (y * _silu(gate)).astype(yb_ref.dtype)
    for r in hist:
        r[0:BLOCK, :] = r[ts:ts + BLOCK, :]


N_SUB = 2
COLS_BC = COLS_B + COLS_C


def _inproj_mix_body(*refs, dec, tiles_per_seq, n_prev):
    (sink_ref, x_ref, g_ref, w_ref, cos_ref, sin_ref, cw_ref, lg_ref, lb_ref, ws_ref, bs_ref) = refs[:11]
    refs = refs[11 + n_prev:]
    xq_ref, xt_ref = refs[:2]
    caches = refs[2:6]
    rest = refs[6 + dec.n_alias:]
    ya_ref, yb_ref, yc_ref, pd_ref, nc_ref, kv_ref = rest[:6]
    dec_outs = rest[6:11]
    h_s, pa_s, pbc_s, tail_s, q_s = rest[11:16]
    hist = rest[16:20]
    tile = pl.program_id(0)
    sub = pl.program_id(1)
    step = tile * N_SUB + sub

    @pl.when(step == 0)
    def _():
        dec_outs[4][...] = jnp.zeros_like(dec_outs[4])
        tail_s[...] = jnp.zeros_like(tail_s)
        for r in hist:
            r[0:BLOCK, :] = jnp.zeros((BLOCK, BLOCK), BF16)

    @pl.when(sub == 0)
    def _():
        h = _rms(x_ref[...], g_ref[...]).astype(BF16)
        h_s[...] = h

        def store(j, cw, acc, base):
            pbc_s[:, base + j:base + j + cw] = acc

        def store_d(j, cw, acc):
            pd_ref[:, j:j + cw] = acc.astype(pd_ref.dtype)

        _project(h, w_ref, OFF_C, COLS_C, functools.partial(store, base=COLS_B))
        _project(h, w_ref, OFF_B, COLS_B, functools.partial(store, base=0))
        _project(h, w_ref, OFF_D, COLS_D, store_d)
        _mix_c(pbc_s, COLS_B, lg_ref, lb_ref, ws_ref, bs_ref, yc_ref)
        _mix_b_prepare(pbc_s, cos_ref, sin_ref, kv_ref, q_s, hist)
        _decode_sequences(dec, step, sink_ref, xq_ref, xt_ref, caches, dec_outs)

    @pl.when(sub == 1)
    def _():
        def store_a(j, cw, acc):
            pa_s[:, j:j + cw] = acc

        _project(h_s[...], w_ref, OFF_A, COLS_A, store_a)

        first_tile = (tile % tiles_per_seq) == 0
        prev2 = jnp.where(first_tile, 0.0, tail_s[0:1, :])
        prev1 = jnp.where(first_tile, 0.0, tail_s[1:2, :])
        prev2, prev1 = _mix_a(pa_s, cw_ref, ya_ref, prev2, prev1)
        last2 = jnp.concatenate([prev2, prev1], axis=0)
        tail_s[...] = last2
        nc_ref[...] = last2
        _mix_b_attend(sink_ref, pbc_s, first_tile, yb_ref, q_s, hist)
        _decode_sequences(dec, step, sink_ref, xq_ref, xt_ref, caches, dec_outs)


def _inproj_mix(x, g, w_bf16, layer, cos, sin, conv_w, ln_g, ln_b, w_s, b_tile, tm, seq, dec, prev_state):
    t = x.shape[0]
    nb = t // seq
    depth = w_bf16.shape[0]
    tiles_per_seq = seq // tm
    assert dec.n_sub == N_SUB
    row = lambda i, j: (i, 0)
    pos = lambda i, j: (i % tiles_per_seq, 0)
    per_seq = lambda i, j: (layer, i // tiles_per_seq, 0, 0)
    in_specs = ([pl.BlockSpec((tm, D_MODEL), row), _const_spec((1, D_MODEL)), _layer_spec(w_bf16.shape, layer),
                 pl.BlockSpec((tm, BLOCK), pos), pl.BlockSpec((tm, BLOCK), pos),
                 _const_spec((3, MIX)), _const_spec((1, MIX)), _const_spec((1, MIX)),
                 _const_spec((MIX // HEAD_DIM, BLOCK, BLOCK)), _const_spec((BLOCK, MIX))]
                + [pl.BlockSpec(memory_space=pl.ANY) for _ in prev_state])
    out_specs = ([pl.BlockSpec((tm, MIX), row)] * 3 + [pl.BlockSpec((tm, COLS_D), row),
                 pl.BlockSpec((None, None, 2, MIX), per_seq), pl.BlockSpec((None, None, KV_CH, BLOCK), per_seq)])
    out_shape = ([jax.ShapeDtypeStruct((t, MIX), BF16)] * 3 + [jax.ShapeDtypeStruct((t, COLS_D), BF16),
                 jax.ShapeDtypeStruct((depth, nb, 2, MIX), F32),
                 jax.ShapeDtypeStruct((depth, nb, KV_CH, BLOCK), F32)])
    args = [x, g, w_bf16, cos, sin, conv_w, ln_g, ln_b, w_s, b_tile, *prev_state]
    state_aliases = {1 + 10 + k: 4 + k for k in range(len(prev_state))}
    scratch = [pltpu.VMEM((tm, D_MODEL), BF16), pltpu.VMEM((tm, COLS_A), F32), pltpu.VMEM((tm, COLS_BC), F32),
               pltpu.VMEM((2, MIX), F32), pltpu.VMEM((tm, MIX), BF16)] + [pltpu.VMEM((BLOCK + tm, BLOCK), BF16)] * 4
    est = (2 * tm * D_MODEL * 4 + D_MODEL * IN_COLS * 2 + 2 * tm * (3 * MIX + COLS_D) * 2
           + tm * (D_MODEL * 2 + (COLS_A + COLS_BC) * 4 + MIX * 2) + 4 * (BLOCK + tm) * BLOCK * 2
           + 4 * tm * BLOCK * 4)
    in_specs, out_specs, out_shape, args, aliases, extra = dec.extend(in_specs, out_specs, out_shape, args)
    return pl.pallas_call(
        functools.partial(_inproj_mix_body, dec=dec.static, tiles_per_seq=tiles_per_seq,
                          n_prev=len(prev_state)),
        grid=(t // tm, N_SUB),
        in_specs=in_specs, out_specs=out_specs, out_shape=out_shape,
        scratch_shapes=scratch,
        input_output_aliases={**aliases, **state_aliases},
        compiler_params=pltpu.CompilerParams(dimension_semantics=("arbitrary", "arbitrary"),
                                             vmem_limit_bytes=_vmem_limit(est + extra)),
        name="inproj_mix",
    )(*args)


def _mix_d_body(pd_ref, cos_ref, sin_ref, yd_ref, c1_ref, c4_ref, c16_ref,
                q_ref, k_ref, v_ref, o_ref, m_ref, d_ref):
    sb = pl.program_id(1)

    lw = 2 * HEAD_DIM
    n_groups = len(DILATIONS)

    @pl.when(sb == 0)
    def _():
        k_ref[:, 0:SUPER, :] = jnp.zeros((n_groups, SUPER, lw), F32)
        v_ref[:, 0:SUPER, :] = jnp.zeros((n_groups, SUPER, lw), F32)

    def prepare(c, carry):
        r0 = pl.multiple_of(c * ROW_CHUNK, ROW_CHUNK)
        rows = pl.ds(r0, ROW_CHUNK)
        new_rows = pl.ds(SUPER + r0, ROW_CHUNK)
        cos = cos_ref[rows, :]
        sin = sin_ref[rows, :]
        for g in range(n_groups):
            q_ref[g, rows, :] = _rope(pd_ref[rows, g * lw:(g + 1) * lw].astype(F32), cos, sin) * SCALE_LOG2
            k_ref[g, new_rows, :] = _rope(pd_ref[rows, MIX + g * lw:MIX + (g + 1) * lw].astype(F32), cos, sin)
            v_ref[g, new_rows, :] = pd_ref[rows, 2 * MIX + g * lw:2 * MIX + (g + 1) * lw].astype(F32)
        return carry

    lax.fori_loop(0, SUPER // ROW_CHUNK, prepare, 0)

    n_blocks = SUPER // BLOCK
    bias_rest = jnp.where(_band_mask(False), 0.0, NEG_INF)
    bias_first = jnp.where(_band_mask(True), 0.0, NEG_INF)
    for g, dil in enumerate(DILATIONS):

        def block(t, carry, dil=dil, g=g):
            r = t % dil
            j = t // dil
            base = j * (BLOCK * dil) + r
            qs = q_ref[g, pl.ds(base, BLOCK, stride=dil), :]
            kstart = SUPER + base - BLOCK * dil
            ks = k_ref[g, pl.ds(kstart, 2 * BLOCK, stride=dil), :]
            vs = v_ref[g, pl.ds(kstart, 2 * BLOCK, stride=dil), :]
            bias = jnp.where(jnp.logical_and(sb == 0, j == 0), bias_first, bias_rest)
            ks_bf = ks.astype(BF16)
            vs_bf = vs.astype(BF16)
            head0 = lax.broadcasted_iota(jnp.int32, (BLOCK, lw), 1) < HEAD_DIM
            o_h, m_h, d_h = [], [], []
            for hh in range(2):
                q_h = jnp.where(head0 if hh == 0 else jnp.logical_not(head0), qs, 0.0).astype(BF16)
                s = _dot_nt(q_h, ks_bf) + bias
                m = jnp.max(s, axis=-1, keepdims=True)
                p = jnp.exp2(s - m)
                o_h.append(_dot(p.astype(BF16), vs_bf))
                m_h.append(m)
                d_h.append(jnp.sum(p, axis=-1, keepdims=True))
            rows = pl.ds(base, BLOCK, stride=dil)
            o_ref[g, rows, :] = jnp.where(head0, o_h[0], o_h[1])
            m_ref[g, rows, :] = jnp.where(head0, m_h[0], m_h[1])
            d_ref[g, rows, :] = jnp.where(head0, d_h[0], d_h[1])
            return carry

        lax.fori_loop(0, n_blocks, block, 0, unroll=BLOCKS_PER_TRIP[g])

    def merge(c, carry):
        rows = pl.ds(pl.multiple_of(c * ROW_CHUNK, ROW_CHUNK), ROW_CHUNK)
        ms = [m_ref[g, rows, :] for g in range(n_groups)]
        mmax = jnp.maximum(jnp.maximum(ms[0], ms[1]), ms[2])
        es = [jnp.exp2(m - mmax) for m in ms]
        inv = 1.0 / (d_ref[0, rows, :] * es[0] + d_ref[1, rows, :] * es[1] + d_ref[2, rows, :] * es[2])
        for g in range(n_groups):
            gate = pd_ref[rows, 3 * MIX + g * lw:3 * MIX + (g + 1) * lw].astype(F32)
            y = o_ref[g, rows, :] * (es[g] * inv)
            yd_ref[rows, g * lw:(g + 1) * lw] = (y * _silu(gate)).astype(yd_ref.dtype)
        return carry

    lax.fori_loop(0, SUPER // ROW_CHUNK, merge, 0)

    for g, (c_ref, dil) in enumerate(zip((c1_ref, c4_ref, c16_ref), DILATIONS)):
        n = BLOCK * dil
        for j in range(0, n, ROW_CHUNK if n >= ROW_CHUNK else n):
            w = min(ROW_CHUNK, n)
            src = slice(2 * SUPER - n + j, 2 * SUPER - n + j + w)
            c_ref[0:lw, j:j + w] = k_ref[g, src, :].T
            c_ref[lw:2 * lw, j:j + w] = v_ref[g, src, :].T

    k_ref[:, 0:SUPER, :] = k_ref[:, SUPER:2 * SUPER, :]
    v_ref[:, 0:SUPER, :] = v_ref[:, SUPER:2 * SUPER, :]


def _mix_d_entry(pd_ref, cos_ref, sin_ref, *rest, n_prev):
    _mix_d_body(pd_ref, cos_ref, sin_ref, *rest[n_prev:])


def _mix_d(pd, cos, sin, layer, depth, prev_caches):
    b, s, _ = pd.shape
    est = (2 * SUPER * COLS_D * 2 + 4 * SUPER * BLOCK * 4 + 2 * SUPER * MIX * 2
           + 2 * KV_CH * (BLOCK + 4 * BLOCK + SUPER) * 4 + (4 + 2 * 2) * SUPER * MIX * 4)
    tile = lambda bi, si: (bi, si, 0)
    cache = lambda bi, si: (layer, bi, 0, 0)
    return pl.pallas_call(
        functools.partial(_mix_d_entry, n_prev=len(prev_caches)),
        grid=(b, s // SUPER),
        in_specs=[pl.BlockSpec((None, SUPER, COLS_D), tile),
                  pl.BlockSpec((SUPER, BLOCK), lambda bi, si: (si, 0)),
                  pl.BlockSpec((SUPER, BLOCK), lambda bi, si: (si, 0))]
                 + [pl.BlockSpec(memory_space=pl.ANY) for _ in prev_caches],
        out_specs=[pl.BlockSpec((None, SUPER, MIX), tile)]
                  + [pl.BlockSpec((None, None, KV_CH, BLOCK * d), cache) for d in DILATIONS],
        out_shape=[jax.ShapeDtypeStruct((b, s, MIX), BF16)]
                  + [jax.ShapeDtypeStruct((depth, b, KV_CH, BLOCK * d), F32) for d in DILATIONS],
        input_output_aliases={3 + k: 1 + k for k in range(len(prev_caches))},
        scratch_shapes=[pltpu.VMEM((3, SUPER, 2 * HEAD_DIM), F32),
                        pltpu.VMEM((3, 2 * SUPER, 2 * HEAD_DIM), F32),
                        pltpu.VMEM((3, 2 * SUPER, 2 * HEAD_DIM), F32),
                        pltpu.VMEM((3, SUPER, 2 * HEAD_DIM), F32),
                        pltpu.VMEM((3, SUPER, 2 * HEAD_DIM), F32),
                        pltpu.VMEM((3, SUPER, 2 * HEAD_DIM), F32)],
        compiler_params=pltpu.CompilerParams(dimension_semantics=("arbitrary", "arbitrary"),
                                             vmem_limit_bytes=_vmem_limit(est)),
        name="mix_d",
    )(pd, cos, sin, *prev_caches)


def _shift_pass(c_ref, o_ref, i, ch0, new_b, per_tile):
    n_tiles = c_ref.shape[2] // BLOCK
    chans = pl.ds(ch0, 2 * HEAD_DIM)
    lane = lax.broadcasted_iota(jnp.int32, (2 * HEAD_DIM, BLOCK), 1)
    prev = None
    for t in range(n_tiles):
        x = c_ref[i, chans, t * BLOCK:(t + 1) * BLOCK]
        per_tile(t, x)
        r = pltpu.roll(x, BLOCK - 1, 1)
        if t > 0:
            o_ref[i, chans, (t - 1) * BLOCK:t * BLOCK] = jnp.where(lane < BLOCK - 1, prev, r)
        prev = r
    o_ref[i, chans, (n_tiles - 1) * BLOCK:n_tiles * BLOCK] = jnp.where(lane < BLOCK - 1, prev, new_b)


def _decode_cache(c_ref, o_ref, i, heads, k_new, v_new, dil):
    hd = HEAD_DIM
    n = c_ref.shape[2]
    n_tiles = n // BLOCK
    s_tiles = [[None] * (n_tiles + 1) for _ in heads]

    def scores(t, x):
        for qi, (g, q_b, _) in enumerate(heads):
            s_tiles[qi][t] = jnp.sum(x[g * hd:(g + 1) * hd, :] * q_b, axis=0, keepdims=True)

    _shift_pass(c_ref, o_ref, i, 0, k_new, scores)
    scores(n_tiles, k_new)

    lane = lax.broadcasted_iota(jnp.int32, (1, n + BLOCK), 1)
    valid = jnp.logical_or(jnp.logical_and(lane < n, (lane & (dil - 1)) == 0), lane == n + BLOCK - 1)
    probs, dens, lses = [], [], []
    for qi, (g, q_b, sink) in enumerate(heads):
        s = jnp.where(valid, jnp.concatenate(s_tiles[qi], axis=1), NEG_INF)
        m = jnp.max(s, axis=1, keepdims=True)
        if sink is not None:
            m = jnp.maximum(m, sink)
        p = jnp.exp(s - m)
        den = jnp.sum(p, axis=1, keepdims=True)
        if sink is not None:
            den = den + jnp.exp(sink - m)
        probs.append(p)
        dens.append(den)
        lses.append(m + jnp.log(den))

    accs = [jnp.zeros((hd, BLOCK), F32) for _ in heads]

    def weighted(t, x):
        for qi, (g, _, _) in enumerate(heads):
            accs[qi] = accs[qi] + x[g * hd:(g + 1) * hd, :] * probs[qi][:, t * BLOCK:(t + 1) * BLOCK]

    _shift_pass(c_ref, o_ref, i, 2 * hd, v_new, weighted)
    weighted(n_tiles, v_new)

    outs = [jnp.sum(accs[qi], axis=1, keepdims=True) / dens[qi] for qi in range(len(heads))]
    return outs, lses


XT_KB, XT_VB, XT_KD, XT_VD = 0, BLOCK, 2 * BLOCK, 2 * BLOCK + MIX
XT_ROWS = 2 * BLOCK + 2 * MIX


def _decode_prep_body(pa_ref, pb_ref, pc_ref, pd_ref, st_ref, cw_ref, lg_ref, lb_ref,
                      wd_ref, bd_ref, cos_ref, sin_ref,
                      ya_ref, yc_ref, nst_ref, vn_ref, xq_ref, xt_ref):
    a_b = pa_ref[:, 0:MIX]
    z = pa_ref[:, MIX:2 * MIX] * pa_ref[:, 2 * MIX:3 * MIX]
    prev0 = st_ref[:, 0:MIX]
    prev1 = st_ref[:, MIX:2 * MIX]
    conv = cw_ref[0:1, :] * prev0 + cw_ref[1:2, :] * prev1 + cw_ref[2:3, :] * z
    ya_ref[...] = a_b * conv * _silu(pa_ref[:, 3 * MIX:4 * MIX])
    nst_ref[:, 0:MIX] = prev1
    nst_ref[:, MIX:2 * MIX] = z

    c_v = pc_ref[:, MIX:2 * MIX]
    mu = jnp.mean(c_v, axis=-1, keepdims=True)
    cen = c_v - mu
    var = jnp.mean(cen * cen, axis=-1, keepdims=True)
    vn = cen * lax.rsqrt(var + EPS) * lg_ref[...] + lb_ref[...]
    vn_ref[...] = vn
    mixed = wd_ref[...] * vn + bd_ref[...]
    yc_ref[...] = pc_ref[:, 0:MIX] * mixed * _silu(pc_ref[:, 2 * MIX:3 * MIX])

    cos = cos_ref[...]
    sin = sin_ref[...]
    cos3, sin3 = _tile_lanes(cos, 3), _tile_lanes(sin, 3)
    xq_ref[0:MIX, :] = (_rope(pb_ref[:, 0:MIX], cos3, sin3) * SCALE).T.astype(BF16)
    xq_ref[MIX:2 * MIX, :] = (_rope(pd_ref[:, 0:MIX], cos3, sin3) * SCALE).T.astype(BF16)
    xt_ref[XT_KB:XT_VB, :] = _rope(pb_ref[:, MIX:MIX + BLOCK], cos, sin).T
    xt_ref[XT_VB:XT_KD, :] = pb_ref[:, MIX + BLOCK:MIX + 2 * BLOCK].T
    xt_ref[XT_KD:XT_VD, :] = _rope(pd_ref[:, MIX:2 * MIX], cos3, sin3).T
    xt_ref[XT_VD:XT_ROWS, :] = pd_ref[:, 2 * MIX:3 * MIX].T


def _inproj_sample_body(x_ref, g_ref, w_ref, st_ref, cw_ref, lg_ref, lb_ref, wd_ref, bd_ref, cos_ref, sin_ref,
                        ya_ref, yc_ref, nst_ref, vn_ref, xq_ref, xt_ref, gb_ref, gd_ref,
                        pa_s, pb_s, pc_s, pd_s):
    h = _rms(x_ref[...], g_ref[...]).astype(BF16)
    for p_s, off in zip((pa_s, pb_s, pc_s, pd_s), (OFF_A, OFF_B, OFF_C, OFF_D)):
        def store(j, cw, acc, p_s=p_s):
            p_s[:, j:j + cw] = acc
        _project(h, w_ref, off, p_s.shape[-1], store)
    _decode_prep_body(pa_s, pb_s, pc_s, pd_s, st_ref, cw_ref, lg_ref, lb_ref, wd_ref, bd_ref, cos_ref, sin_ref,
                      ya_ref, yc_ref, nst_ref, vn_ref, xq_ref, xt_ref)
    gb_ref[...] = pb_s[:, MIX + 2 * BLOCK:COLS_B]
    gd_ref[...] = pd_s[:, 3 * MIX:4 * MIX]


def _inproj_sample(x, g, w_bf16, layer, state, conv_w, ln_g, ln_b, w_diag, b_diag, cos, sin):
    ns = x.shape[0]
    assert ns == BLOCK
    shapes = [((ns, MIX), F32), ((ns, MIX), F32), ((ns, 2 * MIX), F32), ((ns, MIX), F32),
              ((2 * MIX, ns), BF16), ((XT_ROWS, ns), F32), ((ns, MIX), F32), ((ns, MIX), F32)]
    small = (state, conv_w, ln_g, ln_b, w_diag, b_diag, cos, sin)
    est = D_MODEL * IN_COLS * 2 + 8 * ns * IN_COLS * 4
    return pl.pallas_call(
        _inproj_sample_body,
        grid=(1,),
        in_specs=[_const_spec(x.shape), _const_spec(g.shape), _layer_spec(w_bf16.shape, layer)]
                 + [_const_spec(a.shape) for a in small],
        out_specs=[_const_spec(s) for s, _ in shapes],
        out_shape=[jax.ShapeDtypeStruct(s, d) for s, d in shapes],
        scratch_shapes=[pltpu.VMEM((ns, wd), F32) for wd in (COLS_A, COLS_B, COLS_C, COLS_D)],
        compiler_params=pltpu.CompilerParams(dimension_semantics=("arbitrary",),
                                             vmem_limit_bytes=_vmem_limit(est)),
        name="inproj_sample",
    )(x, g, w_bf16, *small)


def _decode_sequences(dec, step, sink_ref, xq_ref, xt_ref, cache_refs, out_refs):
    cs_ref, c1_ref, c4_ref, c16_ref = cache_refs
    ns_ref, n1_ref, n4_ref, n16_ref, yt_ref = out_refs
    n_seq = xq_ref.shape[1]
    hd = HEAD_DIM

    for i in range(dec.seq_per_step):
        b = dec.seq_base + step * dec.seq_per_step + i
        onehot = (lax.broadcasted_iota(jnp.int32, (n_seq, BLOCK), 0) == b).astype(BF16)
        q_b = _dot(xq_ref[...], onehot)
        new_rot = pltpu.roll(xt_ref[...], BLOCK - 1 - b, 1)
        heads = [(h // SWA_REP, q_b[h * hd:(h + 1) * hd], jnp.full((1, 1), sink_ref[h], F32))
                 for h in range(SWA_Q_HEADS)]
        outs, _ = _decode_cache(cs_ref, ns_ref, i, heads, new_rot[XT_KB:XT_VB], new_rot[XT_VB:XT_KD], 1)
        yb_col = jnp.concatenate(outs, axis=0)

        q_col = q_b[MIX:2 * MIX]
        k_new = new_rot[XT_KD:XT_VD]
        v_new = new_rot[XT_VD:XT_ROWS]
        o_heads, l_heads = [], []
        for g, (c_ref, n_ref, dil) in enumerate(zip((c1_ref, c4_ref, c16_ref),
                                                    (n1_ref, n4_ref, n16_ref), DILATIONS)):
            gc = slice(2 * g * hd, 2 * (g + 1) * hd)
            heads = [(hh, q_col[(2 * g + hh) * hd:(2 * g + hh + 1) * hd], None) for hh in range(2)]
            o, lse = _decode_cache(c_ref, n_ref, i, heads, k_new[gc], v_new[gc], dil)
            o_heads += o
            l_heads += lse
        cols = []
        for hh in range(2):
            ls = [l_heads[2 * g + hh] for g in range(3)]
            lmax = jnp.maximum(jnp.maximum(ls[0], ls[1]), ls[2])
            es = [jnp.exp(l - lmax) for l in ls]
            esum = es[0] + es[1] + es[2]
            cols.append([o_heads[2 * g + hh] * (es[g] / esum) for g in range(3)])
        yd_col = jnp.concatenate([cols[hh][g] for g in range(3) for hh in range(2)], axis=0)
        y_col = jnp.concatenate([yb_col, yd_col], axis=0)
        lane_y = lax.broadcasted_iota(jnp.int32, (2 * MIX, n_seq), 1)
        yt_ref[...] = jnp.where(lane_y == b, y_col, yt_ref[...])


class _DecodeStatic(NamedTuple):
    seq_base: int
    seq_per_step: int
    n_alias: int


class _Decode:
    def __init__(self, layer, seq_base, seq_per_step, n_sub, sinks, xq, xt, caches, prev_outs):
        self.layer, self.seq_base, self.seq_per_step, self.n_sub = layer, seq_base, seq_per_step, n_sub
        self.sinks, self.xq, self.xt = sinks, xq, xt
        self.caches, self.prev_outs = tuple(caches), tuple(prev_outs)
        self.static = _DecodeStatic(seq_base, seq_per_step, len(self.prev_outs))

    def extend(self, in_specs, out_specs, out_shape, args):
        layer, block0, n_sub = self.layer, self.seq_base // self.seq_per_step, self.n_sub
        linear = (lambda i: i) if n_sub == 1 else (lambda i, j: i * n_sub + j)
        spec = lambda c: pl.BlockSpec((None, self.seq_per_step) + c.shape[2:],
                                      lambda *ids: (layer, block0 + linear(*ids), 0, 0))
        first_alias_in = 1 + len(args) + 2 + len(self.caches)
        aliases = {first_alias_in + i: len(out_specs) + i for i in range(len(self.prev_outs))}
        in_specs = ([pl.BlockSpec(memory_space=pltpu.SMEM)] + in_specs
                    + [_const_spec(self.xq.shape), _const_spec(self.xt.shape)]
                    + [spec(c) for c in self.caches]
                    + [pl.BlockSpec(memory_space=pl.ANY) for _ in self.prev_outs])
        args = [self.sinks] + args + [self.xq, self.xt, *self.caches, *self.prev_outs]
        yt_shape = (2 * MIX, self.xq.shape[1])
        out_specs = out_specs + [spec(c) for c in self.caches] + [_const_spec(yt_shape)]
        out_shape = (out_shape + [jax.ShapeDtypeStruct(c.shape, F32) for c in self.caches]
                     + [jax.ShapeDtypeStruct(yt_shape, F32)])
        rows = sum(c.shape[3] for c in self.caches)
        extra = 4 * self.seq_per_step * KV_CH * rows * 4 + 4 * (2 * MIX + XT_ROWS) * BLOCK * 4
        return in_specs, out_specs, out_shape, args, aliases, extra


def _rope_tables(pos):
    inv_freq = ROPE_THETA ** (-jnp.arange(HALF_HEAD, dtype=F32) / HALF_HEAD)
    ang = pos.astype(F32)[:, None] * inv_freq[None, :]
    cos, sin = jnp.cos(ang), jnp.sin(ang)
    return (jnp.concatenate([cos, cos, cos, cos], axis=1),
            jnp.concatenate([-sin, sin, -sin, sin], axis=1))


def _cache_view(c):
    d, b, n = c.shape[:3]
    return jnp.transpose(c, (0, 1, 3, 4, 5, 2)).reshape(d, b, KV_CH, n)


def _cache_unview(c):
    lead, n = c.shape[:-2], c.shape[-1]
    nl = len(lead)
    perm = tuple(range(nl)) + (nl + 3, nl, nl + 1, nl + 2)
    return jnp.transpose(c.reshape(lead + (2, 2, HEAD_DIM, n)), perm)


def kernel(x_prompt, x_sample, state_conv, cache_swa_kv, cache_dil1_kv, cache_dil4_kv, cache_dil16_kv,
           norm_g, w_in, conv_w, attn_sinks, v_ln_g, v_ln_b, w_spatial, b_spatial,
           w_branch, w_merge, w_out, final_norm_g):
    depth = w_in.shape[0]
    nb, seq, _ = x_prompt.shape
    ns, dec_seq, _ = x_sample.shape
    assert dec_seq == 1 and seq % SUPER == 0
    past_len = 16384
    assert cache_swa_kv.shape[2] == BLOCK and cache_dil16_kv.shape[2] == SUPER

    cos_p, sin_p = _rope_tables(jnp.arange(seq, dtype=jnp.int32))
    cos_s, sin_s = _rope_tables(past_len + jnp.arange(1, dtype=jnp.int32))
    w_in_bf = w_in.astype(BF16)
    w_merge_bf = w_merge.astype(BF16)
    w_branch_bf = w_branch.astype(BF16)
    w_out_bf = w_out.astype(BF16)
    caches = [_cache_view(c) for c in (cache_swa_kv, cache_dil1_kv, cache_dil4_kv, cache_dil16_kv)]
    fg = final_norm_g.reshape(1, D_MODEL)

    xp = x_prompt.reshape(nb * seq, D_MODEL)
    xs = x_sample.reshape(ns, D_MODEL)
    conv_s, chunk_v = [], []
    state_p = ()
    dil_p = ()
    new_s = ()
    for l in range(depth):
        g = norm_g[l].reshape(1, D_MODEL)
        final = l == depth - 1
        ln_g = v_ln_g[l].reshape(1, MIX)
        ln_b = v_ln_b[l].reshape(1, MIX)

        w_diag = jnp.repeat(w_spatial[l][:, 0, 0], HEAD_DIM).reshape(1, MIX)
        b_diag = jnp.repeat(b_spatial[l][:, 0], HEAD_DIM).reshape(1, MIX)
        za, zc, nstate, vn, xq, xt, gate_b, gate_d = _inproj_sample(
            xs, g, w_in_bf, l, state_conv[l].reshape(ns, 2 * MIX), conv_w[l], ln_g, ln_b, w_diag, b_diag,
            cos_s, sin_s)

        tm = 512
        tiles = nb * seq // tm
        half = tiles * N_SUB
        assert ns - half == 2 * tiles
        b_tile = jnp.repeat(b_spatial[l].T, HEAD_DIM, axis=1)
        dec = _Decode(l, 0, 1, N_SUB, attn_sinks[l], xq, xt, caches, new_s)
        ya, yb, yc, pd, *state_p, ns0, ns1, ns4, ns16, yt0 = _inproj_mix(
            xp, g, w_in_bf, l, cos_p, sin_p, conv_w[l], ln_g, ln_b, w_spatial[l], b_tile, tm, seq, dec,
            state_p)
        yd, *dil_p = _mix_d(pd.reshape(nb, seq, COLS_D), cos_p, sin_p, l, depth, dil_p)
        dec = _Decode(l, half, 2, 1, attn_sinks[l], xq, xt, caches, (ns0, ns1, ns4, ns16))
        xp, *new_s, yt1 = _outproj(xp, (ya, yb, yc, yd.reshape(nb * seq, MIX)), g, w_merge_bf,
                                   w_branch_bf, w_out_bf, fg, l, tm, final, dec)

        xs = _outproj_sample(xs, za, zc, yt0, yt1, gate_b, gate_d, g, w_merge_bf, w_branch_bf, w_out_bf, fg,
                             l, final, half)
        conv_s.append(nstate.reshape(ns, 2, MIX))
        chunk_v.append(vn.reshape(ns, 1, MIX))

    st = jnp.stack
    conv_p, swa_p = state_p
    new_p = [_cache_unview(c) for c in (swa_p, *dil_p)]
    new_s = [_cache_unview(c) for c in new_s]
    return (xp.reshape(nb, seq, D_MODEL), xs.reshape(ns, 1, D_MODEL),
            conv_p, st(conv_s),
            new_p[0], new_s[0], new_p[1], new_s[1],
            new_p[2], new_s[2], new_p[3], new_s[3],
            st(chunk_v))
```

```python
import functools
from typing import NamedTuple

import jax
import jax.numpy as jnp
from jax import lax
from jax.experimental import pallas as pl
from jax.experimental.pallas import tpu as pltpu

F32 = jnp.float32
BF16 = jnp.bfloat16

D_MODEL = 1024
HEAD_DIM = 64
HALF_HEAD = HEAD_DIM // 2
MIX = 384
BLOCK = 128
SWA_Q_HEADS = 6
SWA_KV_HEADS = 2
SWA_REP = SWA_Q_HEADS // SWA_KV_HEADS
DILATIONS = (1, 4, 16)
SUPER = BLOCK * DILATIONS[-1]
BLOCKS_PER_TRIP = (16, 16, 8)
ROW_CHUNK = 256
ROPE_THETA = 10000.0
EPS = 1e-6
NEG_INF = -1e30
SCALE = HEAD_DIM ** -0.5
LOG2E = 1.4426950408889634
SCALE_LOG2 = SCALE * LOG2E

COLS_A = 4 * MIX
COLS_B = MIX + 2 * BLOCK + MIX
COLS_C = 3 * MIX
COLS_D = 4 * MIX
OFF_A, OFF_B, OFF_C, OFF_D = 0, COLS_A, COLS_A + COLS_B, COLS_A + COLS_B + COLS_C
IN_COLS = OFF_D + COLS_D
KV_CH = 4 * HEAD_DIM

V7X_VMEM_BYTES = 64 * 1024 * 1024
MATMUL_COL_CHUNK = 512


def _vmem_limit(estimate_bytes):
    return int(min(V7X_VMEM_BYTES - 8 * 1024 * 1024, estimate_bytes + 16 * 1024 * 1024))


def _const_spec(shape, single=False):
    nd = len(shape)
    if single:
        return pl.BlockSpec(shape, lambda *_: (0,) * nd, pipeline_mode=pl.Buffered(1))
    return pl.BlockSpec(shape, lambda *_: (0,) * nd)


def _layer_spec(stacked_shape, layer):
    nd = len(stacked_shape) - 1
    return pl.BlockSpec((None,) + tuple(stacked_shape[1:]), lambda *_: (layer,) + (0,) * nd,
                        pipeline_mode=pl.Buffered(1))


def _rms(x, g):
    ms = jnp.mean(x * x, axis=-1, keepdims=True)
    return x * lax.rsqrt(ms + EPS) * g


def _sigmoid(x):
    return 0.5 * jnp.tanh(0.5 * x) + 0.5


def _silu(x):
    h = 0.5 * x
    return h * jnp.tanh(h) + h


def _rope(x, cos, sin):
    w = x.shape[-1]
    lane = lax.broadcasted_iota(jnp.int32, x.shape, 1)
    fwd = pltpu.roll(x, w - HALF_HEAD, 1)
    bwd = pltpu.roll(x, HALF_HEAD, 1)
    partner = jnp.where((lane & HALF_HEAD) == 0, fwd, bwd)
    return x * cos + partner * sin


def _tile_lanes(t, reps):
    return t if reps == 1 else jnp.concatenate([t] * reps, axis=1)


def _dot(a, b):
    return jnp.dot(a, b, preferred_element_type=F32)


def _dot_nt(a, b):
    return lax.dot_general(a, b, (((1,), (1,)), ((), ())), preferred_element_type=F32)


def _project(h, w_ref, off, width, store):
    for j in range(0, width, MATMUL_COL_CHUNK):
        cw = min(MATMUL_COL_CHUNK, width - j)
        store(j, cw, _dot(h, w_ref[:, off + j:off + j + cw]))


def _outproj_body(*refs, final, dec):
    sink_ref = refs[0]
    (x_ref, ya_ref, yb_ref, yc_ref, yd_ref, g_ref, wm_ref, wb_ref, wo_ref, fg_ref) = refs[1:11]
    xq_ref, xt_ref = refs[11:13]
    o_ref = refs[17 + dec.n_alias]
    step = pl.program_id(0)
    dec_outs = refs[18 + dec.n_alias:]

    @pl.when(step == 0)
    def _():
        dec_outs[4][...] = jnp.zeros_like(dec_outs[4])

    _decode_sequences(dec, step, sink_ref, xq_ref, xt_ref, refs[13:17], dec_outs)
    ys = (ya_ref[...], yb_ref[...], yc_ref[...], yd_ref[...])
    o_ref[...] = _merge_project(x_ref[...], ys, g_ref, wm_ref, wb_ref, wo_ref, fg_ref, final)


def _merge_project(x, ys, g_ref, wm_ref, wb_ref, wo_ref, fg_ref, final):
    h = _rms(x, g_ref[...]).astype(BF16)
    merged = None
    for n, y in enumerate(ys):
        gate = _sigmoid(_dot(h, wm_ref[:, n * D_MODEL:(n + 1) * D_MODEL]))
        term = gate * _dot(y.astype(BF16), wb_ref[n])
        merged = term if merged is None else merged + term
    out = x + _dot(merged.astype(BF16), wo_ref[...])
    return _rms(out, fg_ref[...]) if final else out


def _outproj_sample_body(x_ref, ya_ref, yc_ref, yt0_ref, yt1_ref, gb_ref, gd_ref,
                         g_ref, wm_ref, wb_ref, wo_ref, fg_ref, o_ref, *, final, split):
    lane = lax.broadcasted_iota(jnp.int32, yt0_ref.shape, 1)
    yt = jnp.where(lane < split, yt0_ref[...], yt1_ref[...])
    yb = yt[0:MIX, :].T * _silu(gb_ref[...])
    yd = yt[MIX:2 * MIX, :].T * _silu(gd_ref[...])
    ys = (ya_ref[...], yb, yc_ref[...], yd)
    o_ref[...] = _merge_project(x_ref[...], ys, g_ref, wm_ref, wb_ref, wo_ref, fg_ref, final)


def _outproj_sample(x, ya, yc, yt0, yt1, gb, gd, g, wm, wb, wo, fg, layer, final, split):
    ns = x.shape[0]
    est = (4 * D_MODEL * D_MODEL + 4 * MIX * D_MODEL + D_MODEL * D_MODEL) * 2 + 24 * ns * D_MODEL * 4
    whole = lambda a: _const_spec(a.shape)
    return pl.pallas_call(
        functools.partial(_outproj_sample_body, final=final, split=split),
        grid=(1,),
        in_specs=[whole(a) for a in (x, ya, yc, yt0, yt1, gb, gd, g)]
                 + [_layer_spec(wm.shape, layer), _layer_spec(wb.shape, layer), _layer_spec(wo.shape, layer),
                    whole(fg)],
        out_specs=_const_spec((ns, D_MODEL)),
        out_shape=jax.ShapeDtypeStruct((ns, D_MODEL), F32),
        compiler_params=pltpu.CompilerParams(dimension_semantics=("arbitrary",),
                                             vmem_limit_bytes=_vmem_limit(est)),
        name="outproj_sample",
    )(x, ya, yc, yt0, yt1, gb, gd, g, wm, wb, wo, fg)


def _outproj(x, ys, g, wm, wb, wo, fg, layer, tm, final, dec):
    t = x.shape[0]
    ysize = jnp.dtype(ys[0].dtype).itemsize
    est = (4 * tm * D_MODEL * 4 + 2 * 4 * tm * MIX * ysize
           + (4 * D_MODEL * D_MODEL + 4 * MIX * D_MODEL + D_MODEL * D_MODEL) * 2
           + 4 * tm * D_MODEL * 4)
    row = lambda i: (i, 0)
    in_specs = ([pl.BlockSpec((tm, D_MODEL), row)]
                + [pl.BlockSpec((tm, MIX), row)] * 4
                + [_const_spec((1, D_MODEL)),
                   _layer_spec(wm.shape, layer), _layer_spec(wb.shape, layer), _layer_spec(wo.shape, layer),
                   _const_spec((1, D_MODEL))])
    out_specs = [pl.BlockSpec((tm, D_MODEL), row)]
    out_shape = [jax.ShapeDtypeStruct((t, D_MODEL), F32)]
    args = [x, *ys, g, wm, wb, wo, fg]
    in_specs, out_specs, out_shape, args, aliases, extra = dec.extend(in_specs, out_specs, out_shape, args)
    return pl.pallas_call(
        functools.partial(_outproj_body, final=final, dec=dec.static),
        grid=(t // tm,),
        in_specs=in_specs, out_specs=out_specs, out_shape=out_shape,
        input_output_aliases=aliases,
        compiler_params=pltpu.CompilerParams(dimension_semantics=("arbitrary",),
                                             vmem_limit_bytes=_vmem_limit(est + extra)),
        name="outproj_dec",
    )(*args)


def _mix_a(pa_ref, cw_ref, ya_ref, prev2, prev1):
    row = lax.broadcasted_iota(jnp.int32, (BLOCK, MIX), 0)
    for ck in range(pa_ref.shape[0] // BLOCK):
        rows = slice(ck * BLOCK, (ck + 1) * BLOCK)
        a_b = pa_ref[rows, 0:MIX]
        z = pa_ref[rows, MIX:2 * MIX] * pa_ref[rows, 2 * MIX:3 * MIX]
        a_gate = pa_ref[rows, 3 * MIX:4 * MIX]
        z1 = jnp.where(row == 0, prev1, pltpu.roll(z, 1, 0))
        z2 = jnp.where(row == 0, prev2, jnp.where(row == 1, prev1, pltpu.roll(z, 2, 0)))
        conv = cw_ref[0:1, :] * z2 + cw_ref[1:2, :] * z1 + cw_ref[2:3, :] * z
        ya_ref[rows, :] = (a_b * conv * _silu(a_gate)).astype(ya_ref.dtype)
        prev2 = z[BLOCK - 2:BLOCK - 1, :]
        prev1 = z[BLOCK - 1:BLOCK, :]
    return prev2, prev1


def _mix_c(pc_ref, col0, lg_ref, lb_ref, ws_ref, bs_ref, yc_ref):
    r = lax.broadcasted_iota(jnp.int32, (BLOCK, BLOCK), 0)
    c = lax.broadcasted_iota(jnp.int32, (BLOCK, BLOCK), 1)
    n_groups = MIX // HEAD_DIM
    w_causal = [jnp.where(r >= c, ws_ref[g], 0.0).astype(BF16) for g in range(n_groups)]
    for ck in range(pc_ref.shape[0] // BLOCK):
        rows = slice(ck * BLOCK, (ck + 1) * BLOCK)
        c_v = pc_ref[rows, col0 + MIX:col0 + 2 * MIX]
        mu = jnp.mean(c_v, axis=-1, keepdims=True)
        cen = c_v - mu
        var = jnp.mean(cen * cen, axis=-1, keepdims=True)
        vn_c = (cen * lax.rsqrt(var + EPS) * lg_ref[...] + lb_ref[...]).astype(BF16)
        mixed = jnp.concatenate(
            [_dot(w_causal[g], vn_c[:, g * HEAD_DIM:(g + 1) * HEAD_DIM]) for g in range(n_groups)],
            axis=1) + bs_ref[...]
        c_u = pc_ref[rows, col0:col0 + MIX]
        c_gate = pc_ref[rows, col0 + 2 * MIX:col0 + 3 * MIX]
        yc_ref[rows, :] = (c_u * mixed * _silu(c_gate)).astype(yc_ref.dtype)


def _band_mask(first_block):
    i = lax.broadcasted_iota(jnp.int32, (BLOCK, 2 * BLOCK), 0)
    j = lax.broadcasted_iota(jnp.int32, (BLOCK, 2 * BLOCK), 1)
    band = (j >= i) & (j <= i + BLOCK)
    return band & jnp.logical_or(jnp.logical_not(first_block), j >= BLOCK)


def _mix_b_prepare(pb_ref, cos_ref, sin_ref, kv_ref, q_s, hist):
    k_s, v_s, ksw_s, vsw_s = hist
    ts = pb_ref.shape[0]
    for r0 in range(0, ts, ROW_CHUNK):
        rows = slice(r0, r0 + ROW_CHUNK)
        dst = slice(BLOCK + r0, BLOCK + r0 + ROW_CHUNK)
        cos = cos_ref[rows, :]
        sin = sin_ref[rows, :]
        q_s[rows, :] = (_rope(pb_ref[rows, 0:MIX], _tile_lanes(cos, 3), _tile_lanes(sin, 3)) * SCALE_LOG2
                        ).astype(BF16)
        k = _rope(pb_ref[rows, MIX:MIX + BLOCK], cos, sin)
        v = pb_ref[rows, MIX + BLOCK:MIX + 2 * BLOCK]
        k_s[dst, :] = k.astype(BF16)
        v_s[dst, :] = v.astype(BF16)
        ksw_s[dst, :] = pltpu.roll(k, HEAD_DIM, 1).astype(BF16)
        vsw_s[dst, :] = pltpu.roll(v, HEAD_DIM, 1).astype(BF16)
        if r0 + ROW_CHUNK == ts:
            kv_ref[0:BLOCK, :] = k[ROW_CHUNK - BLOCK:ROW_CHUNK, :].T
            kv_ref[BLOCK:2 * BLOCK, :] = v[ROW_CHUNK - BLOCK:ROW_CHUNK, :].T


def _mix_b_attend(sink_ref, pb_ref, first_tile, yb_ref, q_s, hist):
    k_s, v_s, ksw_s, vsw_s = hist
    ts = pb_ref.shape[0]
    half0 = lax.broadcasted_iota(jnp.int32, (BLOCK, BLOCK), 1) < HEAD_DIM
    bias_rest = jnp.where(_band_mask(False), 0.0, NEG_INF)
    bias_first = jnp.where(_band_mask(first_tile), 0.0, NEG_INF)
    for c in range(ts // BLOCK):
        rows = slice(c * BLOCK, (c + 1) * BLOCK)
        win = slice(c * BLOCK, (c + 2) * BLOCK)
        kk, vv, kk_sw, vv_sw = k_s[win, :], v_s[win, :], ksw_s[win, :], vsw_s[win, :]
        bias = bias_first if c == 0 else bias_rest
        outs, dens = [], []
        for h in range(SWA_Q_HEADS):
            g, chunk, half = h // SWA_REP, h // 2, h % 2
            qc = q_s[rows, chunk * BLOCK:(chunk + 1) * BLOCK]
            q_h = jnp.where(half0 if half == 0 else jnp.logical_not(half0), qc, jnp.zeros_like(qc))
            s = _dot_nt(q_h, kk if half == g else kk_sw) + bias
            sink = sink_ref[h] * LOG2E
            m = jnp.maximum(jnp.max(s, axis=-1, keepdims=True), sink)
            p = jnp.exp2(s - m)
            dens.append(jnp.sum(p, axis=-1, keepdims=True) + jnp.exp2(sink - m))
            outs.append(_dot(p.astype(BF16), vv if half == g else vv_sw))
        y = jnp.concatenate([jnp.where(half0, outs[2 * j], outs[2 * j + 1])
                             / jnp.where(half0, dens[2 * j], dens[2 * j + 1]) for j in range(3)], axis=1)
        gate = pb_ref[rows, MIX + 2 * BLOCK:COLS_B]
        yb_ref[rows, :] = (y * _silu(gate)).astype(yb_ref.dtype)
    for r in hist:
        r[0:BLOCK, :] = r[ts:ts + BLOCK, :]


N_SUB = 2
COLS_BC = COLS_B + COLS_C


def _inproj_mix_body(*refs, dec, tiles_per_seq, n_prev, n_cast):
    (sink_ref, x_ref, g_ref, w_ref, cos_ref, sin_ref, cw_ref, lg_ref, lb_ref, ws_ref, bs_ref) = refs[:11]
    refs = refs[11 + n_prev:]
    cast_in = refs[:n_cast]
    xq_ref, xt_ref = refs[n_cast:n_cast + 2]
    caches = refs[n_cast + 2:n_cast + 6]
    rest = refs[n_cast + 6 + dec.n_alias:]
    ya_ref, yb_ref, yc_ref, pd_ref, nc_ref, kv_ref = rest[:6]
    cast_out = rest[6:6 + n_cast]
    rest = rest[6 + n_cast:]
    dec_outs = rest[0:5]
    h_s, pa_s, pbc_s, tail_s, q_s = rest[5:10]
    hist = rest[10:14]
    tile = pl.program_id(0)
    sub = pl.program_id(1)
    step = tile * N_SUB + sub

    for src, dst in zip(cast_in, cast_out):
        dst[...] = src[...].astype(BF16)

    @pl.when(step == 0)
    def _():
        dec_outs[4][...] = jnp.zeros_like(dec_outs[4])
        tail_s[...] = jnp.zeros_like(tail_s)
        for r in hist:
            r[0:BLOCK, :] = jnp.zeros((BLOCK, BLOCK), BF16)

    @pl.when(sub == 0)
    def _():
        h = _rms(x_ref[...], g_ref[...]).astype(BF16)
        h_s[...] = h

        def store(j, cw, acc, base):
            pbc_s[:, base + j:base + j + cw] = acc

        def store_d(j, cw, acc):
            pd_ref[:, j:j + cw] = acc.astype(pd_ref.dtype)

        _project(h, w_ref, OFF_C, COLS_C, functools.partial(store, base=COLS_B))
        _project(h, w_ref, OFF_B, COLS_B, functools.partial(store, base=0))
        _project(h, w_ref, OFF_D, COLS_D, store_d)
        _mix_c(pbc_s, COLS_B, lg_ref, lb_ref, ws_ref, bs_ref, yc_ref)
        _mix_b_prepare(pbc_s, cos_ref, sin_ref, kv_ref, q_s, hist)
        _decode_sequences(dec, step, sink_ref, xq_ref, xt_ref, caches, dec_outs)

    @pl.when(sub == 1)
    def _():
        def store_a(j, cw, acc):
            pa_s[:, j:j + cw] = acc

        _project(h_s[...], w_ref, OFF_A, COLS_A, store_a)

        first_tile = (tile % tiles_per_seq) == 0
        prev2 = jnp.where(first_tile, 0.0, tail_s[0:1, :])
        prev1 = jnp.where(first_tile, 0.0, tail_s[1:2, :])
        prev2, prev1 = _mix_a(pa_s, cw_ref, ya_ref, prev2, prev1)
        last2 = jnp.concatenate([prev2, prev1], axis=0)
        tail_s[...] = last2
        nc_ref[...] = last2
        _mix_b_attend(sink_ref, pbc_s, first_tile, yb_ref, q_s, hist)
        _decode_sequences(dec, step, sink_ref, xq_ref, xt_ref, caches, dec_outs)


def _inproj_mix(x, g, w_bf16, w_layer, layer, depth, cos, sin, conv_w, ln_g, ln_b, w_s, b_tile, tm, seq, dec,
                prev_state, casts):
    t = x.shape[0]
    nb = t // seq
    tiles_per_seq = seq // tm
    steps = (t // tm) * N_SUB
    assert dec.n_sub == N_SUB
    row = lambda i, j: (i, 0)
    pos = lambda i, j: (i % tiles_per_seq, 0)
    per_seq = lambda i, j: (layer, i // tiles_per_seq, 0, 0)
    in_specs = ([pl.BlockSpec((tm, D_MODEL), row), _const_spec((1, D_MODEL)), _layer_spec(w_bf16.shape, w_layer),
                 pl.BlockSpec((tm, BLOCK), pos), pl.BlockSpec((tm, BLOCK), pos),
                 _const_spec((3, MIX)), _const_spec((1, MIX)), _const_spec((1, MIX)),
                 _const_spec((MIX // HEAD_DIM, BLOCK, BLOCK)), _const_spec((BLOCK, MIX))]
                + [pl.BlockSpec(memory_space=pl.ANY) for _ in prev_state])
    out_specs = ([pl.BlockSpec((tm, MIX), row)] * 3 + [pl.BlockSpec((tm, COLS_D), row),
                 pl.BlockSpec((None, None, 2, MIX), per_seq), pl.BlockSpec((None, None, KV_CH, BLOCK), per_seq)])
    out_shape = ([jax.ShapeDtypeStruct((t, MIX), BF16)] * 3 + [jax.ShapeDtypeStruct((t, COLS_D), BF16),
                 jax.ShapeDtypeStruct((depth, nb, 2, MIX), F32),
                 jax.ShapeDtypeStruct((depth, nb, KV_CH, BLOCK), F32)])
    args = [x, g, w_bf16, cos, sin, conv_w, ln_g, ln_b, w_s, b_tile, *prev_state]
    state_aliases = {1 + 10 + k: 4 + k for k in range(len(prev_state))}
    cast_bytes = 0
    for arr, n_rows, first_row in casts:
        slab = n_rows // steps
        assert slab * steps == n_rows and slab % 16 == 0 and first_row % slab == 0
        first = first_row // slab
        in_specs.append(pl.BlockSpec((slab, arr.shape[1]), lambda i, j, first=first: (first + i * N_SUB + j, 0)))
        out_specs.append(pl.BlockSpec((slab, arr.shape[1]), lambda i, j: (i * N_SUB + j, 0)))
        out_shape.append(jax.ShapeDtypeStruct((n_rows, arr.shape[1]), BF16))
        args.append(arr)
        cast_bytes += 2 * slab * arr.shape[1] * 6
    scratch = [pltpu.VMEM((tm, D_MODEL), BF16), pltpu.VMEM((tm, COLS_A), F32), pltpu.VMEM((tm, COLS_BC), F32),
               pltpu.VMEM((2, MIX), F32), pltpu.VMEM((tm, MIX), BF16)] + [pltpu.VMEM((BLOCK + tm, BLOCK), BF16)] * 4
    est = (2 * tm * D_MODEL * 4 + D_MODEL * IN_COLS * 2 + 2 * tm * (3 * MIX + COLS_D) * 2
           + tm * (D_MODEL * 2 + (COLS_A + COLS_BC) * 4 + MIX * 2) + 4 * (BLOCK + tm) * BLOCK * 2
           + 4 * tm * BLOCK * 4)
    in_specs, out_specs, out_shape, args, aliases, extra = dec.extend(in_specs, out_specs, out_shape, args)
    return pl.pallas_call(
        functools.partial(_inproj_mix_body, dec=dec.static, tiles_per_seq=tiles_per_seq,
                          n_prev=len(prev_state), n_cast=len(casts)),
        grid=(t // tm, N_SUB),
        in_specs=in_specs, out_specs=out_specs, out_shape=out_shape,
        scratch_shapes=scratch,
        input_output_aliases={**aliases, **state_aliases},
        compiler_params=pltpu.CompilerParams(dimension_semantics=("arbitrary", "arbitrary"),
                                             vmem_limit_bytes=_vmem_limit(est + extra + cast_bytes)),
        name="inproj_mix",
    )(*args)


def _mix_d_body(pd_ref, cos_ref, sin_ref, yd_ref, c1_ref, c4_ref, c16_ref,
                q_ref, k_ref, v_ref, o_ref, m_ref, d_ref):
    sb = pl.program_id(1)

    lw = 2 * HEAD_DIM
    n_groups = len(DILATIONS)

    @pl.when(sb == 0)
    def _():
        k_ref[:, 0:SUPER, :] = jnp.zeros((n_groups, SUPER, lw), F32)
        v_ref[:, 0:SUPER, :] = jnp.zeros((n_groups, SUPER, lw), F32)

    def prepare(c, carry):
        r0 = pl.multiple_of(c * ROW_CHUNK, ROW_CHUNK)
        rows = pl.ds(r0, ROW_CHUNK)
        new_rows = pl.ds(SUPER + r0, ROW_CHUNK)
        cos = cos_ref[rows, :]
        sin = sin_ref[rows, :]
        for g in range(n_groups):
            q_ref[g, rows, :] = _rope(pd_ref[rows, g * lw:(g + 1) * lw].astype(F32), cos, sin) * SCALE_LOG2
            k_ref[g, new_rows, :] = _rope(pd_ref[rows, MIX + g * lw:MIX + (g + 1) * lw].astype(F32), cos, sin)
            v_ref[g, new_rows, :] = pd_ref[rows, 2 * MIX + g * lw:2 * MIX + (g + 1) * lw].astype(F32)
        return carry

    lax.fori_loop(0, SUPER // ROW_CHUNK, prepare, 0)

    n_blocks = SUPER // BLOCK
    bias_rest = jnp.where(_band_mask(False), 0.0, NEG_INF)
    bias_first = jnp.where(_band_mask(True), 0.0, NEG_INF)
    for g, dil in enumerate(DILATIONS):

        def block(t, carry, dil=dil, g=g):
            r = t % dil
            j = t // dil
            base = j * (BLOCK * dil) + r
            qs = q_ref[g, pl.ds(base, BLOCK, stride=dil), :]
            kstart = SUPER + base - BLOCK * dil
            ks = k_ref[g, pl.ds(kstart, 2 * BLOCK, stride=dil), :]
            vs = v_ref[g, pl.ds(kstart, 2 * BLOCK, stride=dil), :]
            bias = jnp.where(jnp.logical_and(sb == 0, j == 0), bias_first, bias_rest)
            ks_bf = ks.astype(BF16)
            vs_bf = vs.astype(BF16)
            head0 = lax.broadcasted_iota(jnp.int32, (BLOCK, lw), 1) < HEAD_DIM
            o_h, m_h, d_h = [], [], []
            for hh in range(2):
                q_h = jnp.where(head0 if hh == 0 else jnp.logical_not(head0), qs, 0.0).astype(BF16)
                s = _dot_nt(q_h, ks_bf) + bias
                m = jnp.max(s, axis=-1, keepdims=True)
                p = jnp.exp2(s - m)
                o_h.append(_dot(p.astype(BF16), vs_bf))
                m_h.append(m)
                d_h.append(jnp.sum(p, axis=-1, keepdims=True))
            rows = pl.ds(base, BLOCK, stride=dil)
            o_ref[g, rows, :] = jnp.where(head0, o_h[0], o_h[1])
            m_ref[g, rows, :] = jnp.where(head0, m_h[0], m_h[1])
            d_ref[g, rows, :] = jnp.where(head0, d_h[0], d_h[1])
            return carry

        lax.fori_loop(0, n_blocks, block, 0, unroll=BLOCKS_PER_TRIP[g])

    def merge(c, carry):
        rows = pl.ds(pl.multiple_of(c * ROW_CHUNK, ROW_CHUNK), ROW_CHUNK)
        ms = [m_ref[g, rows, :] for g in range(n_groups)]
        mmax = jnp.maximum(jnp.maximum(ms[0], ms[1]), ms[2])
        es = [jnp.exp2(m - mmax) for m in ms]
        inv = 1.0 / (d_ref[0, rows, :] * es[0] + d_ref[1, rows, :] * es[1] + d_ref[2, rows, :] * es[2])
        for g in range(n_groups):
            gate = pd_ref[rows, 3 * MIX + g * lw:3 * MIX + (g + 1) * lw].astype(F32)
            y = o_ref[g, rows, :] * (es[g] * inv)
            yd_ref[rows, g * lw:(g + 1) * lw] = (y * _silu(gate)).astype(yd_ref.dtype)
        return carry

    lax.fori_loop(0, SUPER // ROW_CHUNK, merge, 0)

    for g, (c_ref, dil) in enumerate(zip((c1_ref, c4_ref, c16_ref), DILATIONS)):
        n = BLOCK * dil
        for j in range(0, n, ROW_CHUNK if n >= ROW_CHUNK else n):
            w = min(ROW_CHUNK, n)
            src = slice(2 * SUPER - n + j, 2 * SUPER - n + j + w)
            c_ref[0:lw, j:j + w] = k_ref[g, src, :].T
            c_ref[lw:2 * lw, j:j + w] = v_ref[g, src, :].T

    k_ref[:, 0:SUPER, :] = k_ref[:, SUPER:2 * SUPER, :]
    v_ref[:, 0:SUPER, :] = v_ref[:, SUPER:2 * SUPER, :]


def _mix_d_entry(pd_ref, cos_ref, sin_ref, *rest, n_prev):
    _mix_d_body(pd_ref, cos_ref, sin_ref, *rest[n_prev:])


def _mix_d(pd, cos, sin, layer, depth, prev_caches):
    b, s, _ = pd.shape
    est = (2 * SUPER * COLS_D * 2 + 4 * SUPER * BLOCK * 4 + 2 * SUPER * MIX * 2
           + 2 * KV_CH * (BLOCK + 4 * BLOCK + SUPER) * 4 + (4 + 2 * 2) * SUPER * MIX * 4)
    tile = lambda bi, si: (bi, si, 0)
    cache = lambda bi, si: (layer, bi, 0, 0)
    return pl.pallas_call(
        functools.partial(_mix_d_entry, n_prev=len(prev_caches)),
        grid=(b, s // SUPER),
        in_specs=[pl.BlockSpec((None, SUPER, COLS_D), tile),
                  pl.BlockSpec((SUPER, BLOCK), lambda bi, si: (si, 0)),
                  pl.BlockSpec((SUPER, BLOCK), lambda bi, si: (si, 0))]
                 + [pl.BlockSpec(memory_space=pl.ANY) for _ in prev_caches],
        out_specs=[pl.BlockSpec((None, SUPER, MIX), tile)]
                  + [pl.BlockSpec((None, None, KV_CH, BLOCK * d), cache) for d in DILATIONS],
        out_shape=[jax.ShapeDtypeStruct((b, s, MIX), BF16)]
                  + [jax.ShapeDtypeStruct((depth, b, KV_CH, BLOCK * d), F32) for d in DILATIONS],
        input_output_aliases={3 + k: 1 + k for k in range(len(prev_caches))},
        scratch_shapes=[pltpu.VMEM((3, SUPER, 2 * HEAD_DIM), F32),
                        pltpu.VMEM((3, 2 * SUPER, 2 * HEAD_DIM), F32),
                        pltpu.VMEM((3, 2 * SUPER, 2 * HEAD_DIM), F32),
                        pltpu.VMEM((3, SUPER, 2 * HEAD_DIM), F32),
                        pltpu.VMEM((3, SUPER, 2 * HEAD_DIM), F32),
                        pltpu.VMEM((3, SUPER, 2 * HEAD_DIM), F32)],
        compiler_params=pltpu.CompilerParams(dimension_semantics=("arbitrary", "arbitrary"),
                                             vmem_limit_bytes=_vmem_limit(est)),
        name="mix_d",
    )(pd, cos, sin, *prev_caches)


def _shift_pass(c_ref, o_ref, i, ch0, new_b, per_tile):
    n_tiles = c_ref.shape[2] // BLOCK
    chans = pl.ds(ch0, 2 * HEAD_DIM)
    lane = lax.broadcasted_iota(jnp.int32, (2 * HEAD_DIM, BLOCK), 1)
    prev = None
    for t in range(n_tiles):
        x = c_ref[i, chans, t * BLOCK:(t + 1) * BLOCK]
        per_tile(t, x)
        r = pltpu.roll(x, BLOCK - 1, 1)
        if t > 0:
            o_ref[i, chans, (t - 1) * BLOCK:t * BLOCK] = jnp.where(lane < BLOCK - 1, prev, r)
        prev = r
    o_ref[i, chans, (n_tiles - 1) * BLOCK:n_tiles * BLOCK] = jnp.where(lane < BLOCK - 1, prev, new_b)


def _decode_cache(c_ref, o_ref, i, heads, k_new, v_new, dil):
    hd = HEAD_DIM
    n = c_ref.shape[2]
    n_tiles = n // BLOCK
    s_tiles = [[None] * (n_tiles + 1) for _ in heads]

    def scores(t, x):
        for qi, (g, q_b, _) in enumerate(heads):
            s_tiles[qi][t] = jnp.sum(x[g * hd:(g + 1) * hd, :] * q_b, axis=0, keepdims=True)

    _shift_pass(c_ref, o_ref, i, 0, k_new, scores)
    scores(n_tiles, k_new)

    lane = lax.broadcasted_iota(jnp.int32, (1, n + BLOCK), 1)
    valid = jnp.logical_or(jnp.logical_and(lane < n, (lane & (dil - 1)) == 0), lane == n + BLOCK - 1)
    probs, dens, lses = [], [], []
    for qi, (g, q_b, sink) in enumerate(heads):
        s = jnp.where(valid, jnp.concatenate(s_tiles[qi], axis=1), NEG_INF)
        m = jnp.max(s, axis=1, keepdims=True)
        if sink is not None:
            m = jnp.maximum(m, sink)
        p = jnp.exp(s - m)
        den = jnp.sum(p, axis=1, keepdims=True)
        if sink is not None:
            den = den + jnp.exp(sink - m)
        probs.append(p)
        dens.append(den)
        lses.append(m + jnp.log(den))

    accs = [jnp.zeros((hd, BLOCK), F32) for _ in heads]

    def weighted(t, x):
        for qi, (g, _, _) in enumerate(heads):
            accs[qi] = accs[qi] + x[g * hd:(g + 1) * hd, :] * probs[qi][:, t * BLOCK:(t + 1) * BLOCK]

    _shift_pass(c_ref, o_ref, i, 2 * hd, v_new, weighted)
    weighted(n_tiles, v_new)

    outs = [jnp.sum(accs[qi], axis=1, keepdims=True) / dens[qi] for qi in range(len(heads))]
    return outs, lses


XT_KB, XT_VB, XT_KD, XT_VD = 0, BLOCK, 2 * BLOCK, 2 * BLOCK + MIX
XT_ROWS = 2 * BLOCK + 2 * MIX


def _decode_prep_body(pa_ref, pb_ref, pc_ref, pd_ref, st_ref, cw_ref, lg_ref, lb_ref,
                      wd_ref, bd_ref, cos_ref, sin_ref,
                      ya_ref, yc_ref, nst_ref, vn_ref, xq_ref, xt_ref):
    a_b = pa_ref[:, 0:MIX]
    z = pa_ref[:, MIX:2 * MIX] * pa_ref[:, 2 * MIX:3 * MIX]
    prev0 = st_ref[:, 0:MIX]
    prev1 = st_ref[:, MIX:2 * MIX]
    conv = cw_ref[0:1, :] * prev0 + cw_ref[1:2, :] * prev1 + cw_ref[2:3, :] * z
    ya_ref[...] = a_b * conv * _silu(pa_ref[:, 3 * MIX:4 * MIX])
    nst_ref[:, 0:MIX] = prev1
    nst_ref[:, MIX:2 * MIX] = z

    c_v = pc_ref[:, MIX:2 * MIX]
    mu = jnp.mean(c_v, axis=-1, keepdims=True)
    cen = c_v - mu
    var = jnp.mean(cen * cen, axis=-1, keepdims=True)
    vn = cen * lax.rsqrt(var + EPS) * lg_ref[...] + lb_ref[...]
    vn_ref[...] = vn
    mixed = wd_ref[...] * vn + bd_ref[...]
    yc_ref[...] = pc_ref[:, 0:MIX] * mixed * _silu(pc_ref[:, 2 * MIX:3 * MIX])

    cos = cos_ref[...]
    sin = sin_ref[...]
    cos3, sin3 = _tile_lanes(cos, 3), _tile_lanes(sin, 3)
    xq_ref[0:MIX, :] = (_rope(pb_ref[:, 0:MIX], cos3, sin3) * SCALE).T.astype(BF16)
    xq_ref[MIX:2 * MIX, :] = (_rope(pd_ref[:, 0:MIX], cos3, sin3) * SCALE).T.astype(BF16)
    xt_ref[XT_KB:XT_VB, :] = _rope(pb_ref[:, MIX:MIX + BLOCK], cos, sin).T
    xt_ref[XT_VB:XT_KD, :] = pb_ref[:, MIX + BLOCK:MIX + 2 * BLOCK].T
    xt_ref[XT_KD:XT_VD, :] = _rope(pd_ref[:, MIX:2 * MIX], cos3, sin3).T
    xt_ref[XT_VD:XT_ROWS, :] = pd_ref[:, 2 * MIX:3 * MIX].T


def _inproj_sample_body(x_ref, g_ref, w_ref, st_ref, cw_ref, lg_ref, lb_ref, wd_ref, bd_ref, cos_ref, sin_ref,
                        ya_ref, yc_ref, nst_ref, vn_ref, xq_ref, xt_ref, gb_ref, gd_ref, *rest, cast):
    if cast:
        wbf_ref = rest[0]
        rest = rest[1:]
        for j in range(0, IN_COLS, MATMUL_COL_CHUNK):
            cw = min(MATMUL_COL_CHUNK, IN_COLS - j)
            wbf_ref[:, j:j + cw] = w_ref[:, j:j + cw].astype(BF16)
        w_ref = wbf_ref
    pa_s, pb_s, pc_s, pd_s = rest
    h = _rms(x_ref[...], g_ref[...]).astype(BF16)
    for p_s, off in zip((pa_s, pb_s, pc_s, pd_s), (OFF_A, OFF_B, OFF_C, OFF_D)):
        def store(j, cw, acc, p_s=p_s):
            p_s[:, j:j + cw] = acc
        _project(h, w_ref, off, p_s.shape[-1], store)
    _decode_prep_body(pa_s, pb_s, pc_s, pd_s, st_ref, cw_ref, lg_ref, lb_ref, wd_ref, bd_ref, cos_ref, sin_ref,
                      ya_ref, yc_ref, nst_ref, vn_ref, xq_ref, xt_ref)
    gb_ref[...] = pb_s[:, MIX + 2 * BLOCK:COLS_B]
    gd_ref[...] = pd_s[:, 3 * MIX:4 * MIX]


def _inproj_sample(x, g, w, layer, state, conv_w, ln_g, ln_b, w_diag, b_diag, cos, sin):
    ns = x.shape[0]
    assert ns == BLOCK
    cast = w.dtype != BF16
    shapes = [((ns, MIX), F32), ((ns, MIX), F32), ((ns, 2 * MIX), F32), ((ns, MIX), F32),
              ((2 * MIX, ns), BF16), ((XT_ROWS, ns), F32), ((ns, MIX), F32), ((ns, MIX), F32)]
    out_specs = [_const_spec(s) for s, _ in shapes]
    if cast:
        shapes.append(((1, D_MODEL, IN_COLS), BF16))
        out_specs.append(pl.BlockSpec((None, D_MODEL, IN_COLS), lambda i: (0, 0, 0)))
    small = (state, conv_w, ln_g, ln_b, w_diag, b_diag, cos, sin)
    est = D_MODEL * IN_COLS * (2 + 4 * cast) * 2 + 8 * ns * IN_COLS * 4
    return pl.pallas_call(
        functools.partial(_inproj_sample_body, cast=cast),
        grid=(1,),
        in_specs=[_const_spec(x.shape), _const_spec(g.shape), _layer_spec(w.shape, layer)]
                 + [_const_spec(a.shape) for a in small],
        out_specs=out_specs,
        out_shape=[jax.ShapeDtypeStruct(s, d) for s, d in shapes],
        scratch_shapes=[pltpu.VMEM((ns, wd), F32) for wd in (COLS_A, COLS_B, COLS_C, COLS_D)],
        compiler_params=pltpu.CompilerParams(dimension_semantics=("arbitrary",),
                                             vmem_limit_bytes=_vmem_limit(est)),
        name="inproj_sample",
    )(x, g, w, *small)


def _decode_sequences(dec, step, sink_ref, xq_ref, xt_ref, cache_refs, out_refs):
    cs_ref, c1_ref, c4_ref, c16_ref = cache_refs
    ns_ref, n1_ref, n4_ref, n16_ref, yt_ref = out_refs
    n_seq = xq_ref.shape[1]
    hd = HEAD_DIM

    for i in range(dec.seq_per_step):
        b = dec.seq_base + step * dec.seq_per_step + i
        onehot = (lax.broadcasted_iota(jnp.int32, (n_seq, BLOCK), 0) == b).astype(BF16)
        q_b = _dot(xq_ref[...], onehot)
        new_rot = pltpu.roll(xt_ref[...], BLOCK - 1 - b, 1)
        heads = [(h // SWA_REP, q_b[h * hd:(h + 1) * hd], jnp.full((1, 1), sink_ref[h], F32))
                 for h in range(SWA_Q_HEADS)]
        outs, _ = _decode_cache(cs_ref, ns_ref, i, heads, new_rot[XT_KB:XT_VB], new_rot[XT_VB:XT_KD], 1)
        yb_col = jnp.concatenate(outs, axis=0)

        q_col = q_b[MIX:2 * MIX]
        k_new = new_rot[XT_KD:XT_VD]
        v_new = new_rot[XT_VD:XT_ROWS]
        o_heads, l_heads = [], []
        for g, (c_ref, n_ref, dil) in enumerate(zip((c1_ref, c4_ref, c16_ref),
                                                    (n1_ref, n4_ref, n16_ref), DILATIONS)):
            gc = slice(2 * g * hd, 2 * (g + 1) * hd)
            heads = [(hh, q_col[(2 * g + hh) * hd:(2 * g + hh + 1) * hd], None) for hh in range(2)]
            o, lse = _decode_cache(c_ref, n_ref, i, heads, k_new[gc], v_new[gc], dil)
            o_heads += o
            l_heads += lse
        cols = []
        for hh in range(2):
            ls = [l_heads[2 * g + hh] for g in range(3)]
            lmax = jnp.maximum(jnp.maximum(ls[0], ls[1]), ls[2])
            es = [jnp.exp(l - lmax) for l in ls]
            esum = es[0] + es[1] + es[2]
            cols.append([o_heads[2 * g + hh] * (es[g] / esum) for g in range(3)])
        yd_col = jnp.concatenate([cols[hh][g] for g in range(3) for hh in range(2)], axis=0)
        y_col = jnp.concatenate([yb_col, yd_col], axis=0)
        lane_y = lax.broadcasted_iota(jnp.int32, (2 * MIX, n_seq), 1)
        yt_ref[...] = jnp.where(lane_y == b, y_col, yt_ref[...])


class _DecodeStatic(NamedTuple):
    seq_base: int
    seq_per_step: int
    n_alias: int


class _Decode:
    def __init__(self, layer, seq_base, seq_per_step, n_sub, sinks, xq, xt, caches, prev_outs):
        self.layer, self.seq_base, self.seq_per_step, self.n_sub = layer, seq_base, seq_per_step, n_sub
        self.sinks, self.xq, self.xt = sinks, xq, xt
        self.caches, self.prev_outs = tuple(caches), tuple(prev_outs)
        self.static = _DecodeStatic(seq_base, seq_per_step, len(self.prev_outs))

    def extend(self, in_specs, out_specs, out_shape, args):
        layer, block0, n_sub = self.layer, self.seq_base // self.seq_per_step, self.n_sub
        linear = (lambda i: i) if n_sub == 1 else (lambda i, j: i * n_sub + j)
        spec = lambda c: pl.BlockSpec((None, self.seq_per_step) + c.shape[2:],
                                      lambda *ids: (layer, block0 + linear(*ids), 0, 0))
        first_alias_in = 1 + len(args) + 2 + len(self.caches)
        aliases = {first_alias_in + i: len(out_specs) + i for i in range(len(self.prev_outs))}
        in_specs = ([pl.BlockSpec(memory_space=pltpu.SMEM)] + in_specs
                    + [_const_spec(self.xq.shape), _const_spec(self.xt.shape)]
                    + [spec(c) for c in self.caches]
                    + [pl.BlockSpec(memory_space=pl.ANY) for _ in self.prev_outs])
        args = [self.sinks] + args + [self.xq, self.xt, *self.caches, *self.prev_outs]
        yt_shape = (2 * MIX, self.xq.shape[1])
        out_specs = out_specs + [spec(c) for c in self.caches] + [_const_spec(yt_shape)]
        out_shape = (out_shape + [jax.ShapeDtypeStruct(c.shape, F32) for c in self.caches]
                     + [jax.ShapeDtypeStruct(yt_shape, F32)])
        rows = sum(c.shape[3] for c in self.caches)
        extra = 4 * self.seq_per_step * KV_CH * rows * 4 + 4 * (2 * MIX + XT_ROWS) * BLOCK * 4
        return in_specs, out_specs, out_shape, args, aliases, extra


def _rope_tables(pos):
    inv_freq = ROPE_THETA ** (-jnp.arange(HALF_HEAD, dtype=F32) / HALF_HEAD)
    ang = pos.astype(F32)[:, None] * inv_freq[None, :]
    cos, sin = jnp.cos(ang), jnp.sin(ang)
    return (jnp.concatenate([cos, cos, cos, cos], axis=1),
            jnp.concatenate([-sin, sin, -sin, sin], axis=1))


def _cache_view(c):
    d, b, n = c.shape[:3]
    return jnp.transpose(c, (0, 1, 3, 4, 5, 2)).reshape(d, b, KV_CH, n)


def _cache_unview(c):
    lead, n = c.shape[:-2], c.shape[-1]
    nl = len(lead)
    perm = tuple(range(nl)) + (nl + 3, nl, nl + 1, nl + 2)
    return jnp.transpose(c.reshape(lead + (2, 2, HEAD_DIM, n)), perm)


def kernel(x_prompt, x_sample, state_conv, cache_swa_kv, cache_dil1_kv, cache_dil4_kv, cache_dil16_kv,
           norm_g, w_in, conv_w, attn_sinks, v_ln_g, v_ln_b, w_spatial, b_spatial,
           w_branch, w_merge, w_out, final_norm_g):
    depth = w_in.shape[0]
    nb, seq, _ = x_prompt.shape
    ns, dec_seq, _ = x_sample.shape
    assert dec_seq == 1 and seq % SUPER == 0
    past_len = 16384
    assert cache_swa_kv.shape[2] == BLOCK and cache_dil16_kv.shape[2] == SUPER

    cos_p, sin_p = _rope_tables(jnp.arange(seq, dtype=jnp.int32))
    cos_s, sin_s = _rope_tables(past_len + jnp.arange(1, dtype=jnp.int32))
    assert depth == 2
    casts = ((w_in.reshape(depth * D_MODEL, IN_COLS), D_MODEL, D_MODEL),
             (w_merge.reshape(depth * D_MODEL, 4 * D_MODEL), depth * D_MODEL, 0),
             (w_branch.reshape(depth * 4 * MIX, D_MODEL), depth * 4 * MIX, 0),
             (w_out.reshape(depth * D_MODEL, D_MODEL), depth * D_MODEL, 0))
    w_in_l = w_in
    caches = [_cache_view(c) for c in (cache_swa_kv, cache_dil1_kv, cache_dil4_kv, cache_dil16_kv)]
    fg = final_norm_g.reshape(1, D_MODEL)

    xp = x_prompt.reshape(nb * seq, D_MODEL)
    xs = x_sample.reshape(ns, D_MODEL)
    conv_s, chunk_v = [], []
    state_p = ()
    dil_p = ()
    new_s = ()
    for l in range(depth):
        g = norm_g[l].reshape(1, D_MODEL)
        final = l == depth - 1
        ln_g = v_ln_g[l].reshape(1, MIX)
        ln_b = v_ln_b[l].reshape(1, MIX)

        w_diag = jnp.repeat(w_spatial[l][:, 0, 0], HEAD_DIM).reshape(1, MIX)
        b_diag = jnp.repeat(b_spatial[l][:, 0], HEAD_DIM).reshape(1, MIX)
        za, zc, nstate, vn, xq, xt, gate_b, gate_d, *w_new = _inproj_sample(
            xs, g, w_in_l, 0, state_conv[l].reshape(ns, 2 * MIX), conv_w[l], ln_g, ln_b, w_diag, b_diag,
            cos_s, sin_s)
        if w_new:
            w_in_l, = w_new

        tm = 512
        tiles = nb * seq // tm
        half = tiles * N_SUB
        assert ns - half == 2 * tiles
        b_tile = jnp.repeat(b_spatial[l].T, HEAD_DIM, axis=1)
        dec = _Decode(l, 0, 1, N_SUB, attn_sinks[l], xq, xt, caches, new_s)
        ya, yb, yc, pd, conv_p, swa_p, *rest = _inproj_mix(
            xp, g, w_in_l, 0, l, depth, cos_p, sin_p, conv_w[l], ln_g, ln_b, w_spatial[l], b_tile, tm, seq,
            dec, state_p, casts)
        state_p = (conv_p, swa_p)
        if casts:
            w_next, w_merge_bf, w_branch_bf, w_out_bf = rest[:4]
            w_in_next = w_next.reshape(1, D_MODEL, IN_COLS)
            w_merge_bf = w_merge_bf.reshape(depth, D_MODEL, 4 * D_MODEL)
            w_branch_bf = w_branch_bf.reshape(depth, 4, MIX, D_MODEL)
            w_out_bf = w_out_bf.reshape(depth, D_MODEL, D_MODEL)
            rest = rest[4:]
            casts = ()
        ns0, ns1, ns4, ns16, yt0 = rest
        yd, *dil_p = _mix_d(pd.reshape(nb, seq, COLS_D), cos_p, sin_p, l, depth, dil_p)
        dec = _Decode(l, half, 2, 1, attn_sinks[l], xq, xt, caches, (ns0, ns1, ns4, ns16))
        xp, *new_s, yt1 = _outproj(xp, (ya, yb, yc, yd.reshape(nb * seq, MIX)), g, w_merge_bf,
                                   w_branch_bf, w_out_bf, fg, l, tm, final, dec)

        xs = _outproj_sample(xs, za, zc, yt0, yt1, gate_b, gate_d, g, w_merge_bf, w_branch_bf, w_out_bf, fg,
                             l, final, half)
        conv_s.append(nstate.reshape(ns, 2, MIX))
        chunk_v.append(vn.reshape(ns, 1, MIX))
        w_in_l = w_in_next

    st = jnp.stack
    conv_p, swa_p = state_p
    new_p = [_cache_unview(c) for c in (swa_p, *dil_p)]
    new_s = [_cache_unview(c) for c in new_s]
    return (xp.reshape(nb, seq, D_MODEL), xs.reshape(ns, 1, D_MODEL),
            conv_p, st(conv_s),
            new_p[0], new_s[0], new_p[1], new_s[1],
            new_p[2], new_s[2], new_p[3], new_s[3],
            st(chunk_v))
```

```python
import functools
from typing import NamedTuple

import jax
import jax.numpy as jnp
from jax import lax
from jax.experimental import pallas as pl
from jax.experimental.pallas import tpu as pltpu

F32 = jnp.float32
BF16 = jnp.bfloat16

D_MODEL = 1024
HEAD_DIM = 64
HALF_HEAD = HEAD_DIM // 2
MIX = 384
BLOCK = 128
SWA_Q_HEADS = 6
SWA_KV_HEADS = 2
SWA_REP = SWA_Q_HEADS // SWA_KV_HEADS
DILATIONS = (1, 4, 16)
SUPER = BLOCK * DILATIONS[-1]
ROW_CHUNK = 256
ROPE_THETA = 10000.0
EPS = 1e-6
NEG_INF = -1e30
SCALE = HEAD_DIM ** -0.5
LOG2E = 1.4426950408889634
SCALE_LOG2 = SCALE * LOG2E

COLS_A = 4 * MIX
COLS_B = MIX + 2 * BLOCK + MIX
COLS_C = 3 * MIX
COLS_D = 4 * MIX
OFF_A, OFF_B, OFF_C, OFF_D = 0, COLS_A, COLS_A + COLS_B, COLS_A + COLS_B + COLS_C
IN_COLS = OFF_D + COLS_D
KV_CH = 4 * HEAD_DIM

V7X_VMEM_BYTES = 64 * 1024 * 1024
MATMUL_COL_CHUNK = 512


def _vmem_limit(estimate_bytes):
    return int(min(V7X_VMEM_BYTES - 8 * 1024 * 1024, estimate_bytes + 16 * 1024 * 1024))


def _const_spec(shape, single=False):
    nd = len(shape)
    if single:
        return pl.BlockSpec(shape, lambda *_: (0,) * nd, pipeline_mode=pl.Buffered(1))
    return pl.BlockSpec(shape, lambda *_: (0,) * nd)


def _layer_spec(stacked_shape, layer):
    nd = len(stacked_shape) - 1
    return pl.BlockSpec((None,) + tuple(stacked_shape[1:]), lambda *_: (layer,) + (0,) * nd,
                        pipeline_mode=pl.Buffered(1))


def _rms(x, g):
    ms = jnp.mean(x * x, axis=-1, keepdims=True)
    return x * lax.rsqrt(ms + EPS) * g


def _sigmoid(x):
    return 0.5 * jnp.tanh(0.5 * x) + 0.5


def _silu(x):
    h = 0.5 * x
    return h * jnp.tanh(h) + h


def _rope(x, cos, sin):
    w = x.shape[-1]
    lane = lax.broadcasted_iota(jnp.int32, x.shape, 1)
    fwd = pltpu.roll(x, w - HALF_HEAD, 1)
    bwd = pltpu.roll(x, HALF_HEAD, 1)
    partner = jnp.where((lane & HALF_HEAD) == 0, fwd, bwd)
    return x * cos + partner * sin


def _tile_lanes(t, reps):
    return t if reps == 1 else jnp.concatenate([t] * reps, axis=1)


def _dot(a, b):
    return jnp.dot(a, b, preferred_element_type=F32)


def _dot_nt(a, b):
    return lax.dot_general(a, b, (((1,), (1,)), ((), ())), preferred_element_type=F32)


def _project(h, w_ref, off, width, store):
    for j in range(0, width, MATMUL_COL_CHUNK):
        cw = min(MATMUL_COL_CHUNK, width - j)
        store(j, cw, _dot(h, w_ref[:, off + j:off + j + cw]))


def _outproj_body(*refs, final, dec):
    sink_ref = refs[0]
    (x_ref, ya_ref, yb_ref, yc_ref, yd_ref, g_ref, wm_ref, wb_ref, wo_ref, fg_ref) = refs[1:11]
    xq_ref, xt_ref = refs[11:13]
    o_ref = refs[17 + dec.n_alias]
    step = pl.program_id(0)
    dec_outs = refs[18 + dec.n_alias:]

    @pl.when(step == 0)
    def _():
        dec_outs[4][...] = jnp.zeros_like(dec_outs[4])

    _decode_sequences(dec, step, sink_ref, xq_ref, xt_ref, refs[13:17], dec_outs)
    ys = (ya_ref[...], yb_ref[...], yc_ref[...], yd_ref[...])
    o_ref[...] = _merge_project(x_ref[...], ys, g_ref, wm_ref, wb_ref, wo_ref, fg_ref, final)


def _merge_project(x, ys, g_ref, wm_ref, wb_ref, wo_ref, fg_ref, final):
    h = _rms(x, g_ref[...]).astype(BF16)
    merged = None
    for n, y in enumerate(ys):
        gate = _sigmoid(_dot(h, wm_ref[:, n * D_MODEL:(n + 1) * D_MODEL]))
        term = gate * _dot(y.astype(BF16), wb_ref[n])
        merged = term if merged is None else merged + term
    out = x + _dot(merged.astype(BF16), wo_ref[...])
    return _rms(out, fg_ref[...]) if final else out


def _outproj_sample_body(x_ref, ya_ref, yc_ref, yt0_ref, yt1_ref, gb_ref, gd_ref,
                         g_ref, wm_ref, wb_ref, wo_ref, fg_ref, o_ref, *, final, split):
    lane = lax.broadcasted_iota(jnp.int32, yt0_ref.shape, 1)
    yt = jnp.where(lane < split, yt0_ref[...], yt1_ref[...])
    yb = yt[0:MIX, :].T * _silu(gb_ref[...])
    yd = yt[MIX:2 * MIX, :].T * _silu(gd_ref[...])
    ys = (ya_ref[...], yb, yc_ref[...], yd)
    o_ref[...] = _merge_project(x_ref[...], ys, g_ref, wm_ref, wb_ref, wo_ref, fg_ref, final)


def _outproj_sample(x, ya, yc, yt0, yt1, gb, gd, g, wm, wb, wo, fg, layer, final, split):
    ns = x.shape[0]
    est = (4 * D_MODEL * D_MODEL + 4 * MIX * D_MODEL + D_MODEL * D_MODEL) * 2 + 24 * ns * D_MODEL * 4
    whole = lambda a: _const_spec(a.shape)
    return pl.pallas_call(
        functools.partial(_outproj_sample_body, final=final, split=split),
        grid=(1,),
        in_specs=[whole(a) for a in (x, ya, yc, yt0, yt1, gb, gd, g)]
                 + [_layer_spec(wm.shape, layer), _layer_spec(wb.shape, layer), _layer_spec(wo.shape, layer),
                    whole(fg)],
        out_specs=_const_spec((ns, D_MODEL)),
        out_shape=jax.ShapeDtypeStruct((ns, D_MODEL), F32),
        compiler_params=pltpu.CompilerParams(dimension_semantics=("arbitrary",),
                                             vmem_limit_bytes=_vmem_limit(est)),
        name="outproj_sample",
    )(x, ya, yc, yt0, yt1, gb, gd, g, wm, wb, wo, fg)


def _outproj(x, ys, g, wm, wb, wo, fg, layer, tm, final, dec):
    t = x.shape[0]
    ysize = jnp.dtype(ys[0].dtype).itemsize
    est = (4 * tm * D_MODEL * 4 + 2 * 4 * tm * MIX * ysize
           + (4 * D_MODEL * D_MODEL + 4 * MIX * D_MODEL + D_MODEL * D_MODEL) * 2
           + 4 * tm * D_MODEL * 4)
    row = lambda i: (i, 0)
    in_specs = ([pl.BlockSpec((tm, D_MODEL), row)]
                + [pl.BlockSpec((tm, MIX), row)] * 4
                + [_const_spec((1, D_MODEL)),
                   _layer_spec(wm.shape, layer), _layer_spec(wb.shape, layer), _layer_spec(wo.shape, layer),
                   _const_spec((1, D_MODEL))])
    out_specs = [pl.BlockSpec((tm, D_MODEL), row)]
    out_shape = [jax.ShapeDtypeStruct((t, D_MODEL), F32)]
    args = [x, *ys, g, wm, wb, wo, fg]
    in_specs, out_specs, out_shape, args, aliases, extra = dec.extend(in_specs, out_specs, out_shape, args)
    return pl.pallas_call(
        functools.partial(_outproj_body, final=final, dec=dec.static),
        grid=(t // tm,),
        in_specs=in_specs, out_specs=out_specs, out_shape=out_shape,
        input_output_aliases=aliases,
        compiler_params=pltpu.CompilerParams(dimension_semantics=("arbitrary",),
                                             vmem_limit_bytes=_vmem_limit(est + extra)),
        name="outproj_dec",
    )(*args)


def _mix_a(pa_ref, cw_ref, ya_ref, prev2, prev1):
    row = lax.broadcasted_iota(jnp.int32, (BLOCK, MIX), 0)
    for ck in range(pa_ref.shape[0] // BLOCK):
        rows = slice(ck * BLOCK, (ck + 1) * BLOCK)
        a_b = pa_ref[rows, 0:MIX]
        z = pa_ref[rows, MIX:2 * MIX] * pa_ref[rows, 2 * MIX:3 * MIX]
        a_gate = pa_ref[rows, 3 * MIX:4 * MIX]
        z1 = jnp.where(row == 0, prev1, pltpu.roll(z, 1, 0))
        z2 = jnp.where(row == 0, prev2, jnp.where(row == 1, prev1, pltpu.roll(z, 2, 0)))
        conv = cw_ref[0:1, :] * z2 + cw_ref[1:2, :] * z1 + cw_ref[2:3, :] * z
        ya_ref[rows, :] = (a_b * conv * _silu(a_gate)).astype(ya_ref.dtype)
        prev2 = z[BLOCK - 2:BLOCK - 1, :]
        prev1 = z[BLOCK - 1:BLOCK, :]
    return prev2, prev1


def _mix_c(pc_ref, col0, lg_ref, lb_ref, ws_ref, bs_ref, yc_ref):
    r = lax.broadcasted_iota(jnp.int32, (BLOCK, BLOCK), 0)
    c = lax.broadcasted_iota(jnp.int32, (BLOCK, BLOCK), 1)
    n_groups = MIX // HEAD_DIM
    w_causal = [jnp.where(r >= c, ws_ref[g], 0.0).astype(BF16) for g in range(n_groups)]
    for ck in range(pc_ref.shape[0] // BLOCK):
        rows = slice(ck * BLOCK, (ck + 1) * BLOCK)
        c_v = pc_ref[rows, col0 + MIX:col0 + 2 * MIX]
        mu = jnp.mean(c_v, axis=-1, keepdims=True)
        cen = c_v - mu
        var = jnp.mean(cen * cen, axis=-1, keepdims=True)
        vn_c = (cen * lax.rsqrt(var + EPS) * lg_ref[...] + lb_ref[...]).astype(BF16)
        mixed = jnp.concatenate(
            [_dot(w_causal[g], vn_c[:, g * HEAD_DIM:(g + 1) * HEAD_DIM]) for g in range(n_groups)],
            axis=1) + bs_ref[...]
        c_u = pc_ref[rows, col0:col0 + MIX]
        c_gate = pc_ref[rows, col0 + 2 * MIX:col0 + 3 * MIX]
        yc_ref[rows, :] = (c_u * mixed * _silu(c_gate)).astype(yc_ref.dtype)


def _band_mask(first_block):
    i = lax.broadcasted_iota(jnp.int32, (BLOCK, 2 * BLOCK), 0)
    j = lax.broadcasted_iota(jnp.int32, (BLOCK, 2 * BLOCK), 1)
    band = (j >= i) & (j <= i + BLOCK)
    return band & jnp.logical_or(jnp.logical_not(first_block), j >= BLOCK)


def _mix_b_prepare(pb_ref, cos_ref, sin_ref, kv_ref, q_s, hist):
    k_s, v_s, ksw_s, vsw_s = hist
    ts = pb_ref.shape[0]
    for r0 in range(0, ts, ROW_CHUNK):
        rows = slice(r0, r0 + ROW_CHUNK)
        dst = slice(BLOCK + r0, BLOCK + r0 + ROW_CHUNK)
        cos = cos_ref[rows, :]
        sin = sin_ref[rows, :]
        q_s[rows, :] = (_rope(pb_ref[rows, 0:MIX], _tile_lanes(cos, 3), _tile_lanes(sin, 3)) * SCALE_LOG2
                        ).astype(BF16)
        k = _rope(pb_ref[rows, MIX:MIX + BLOCK], cos, sin)
        v = pb_ref[rows, MIX + BLOCK:MIX + 2 * BLOCK]
        k_s[dst, :] = k.astype(BF16)
        v_s[dst, :] = v.astype(BF16)
        ksw_s[dst, :] = pltpu.roll(k, HEAD_DIM, 1).astype(BF16)
        vsw_s[dst, :] = pltpu.roll(v, HEAD_DIM, 1).astype(BF16)
        if r0 + ROW_CHUNK == ts:
            kv_ref[0:BLOCK, :] = k[ROW_CHUNK - BLOCK:ROW_CHUNK, :].T
            kv_ref[BLOCK:2 * BLOCK, :] = v[ROW_CHUNK - BLOCK:ROW_CHUNK, :].T


def _mix_b_attend(sink_ref, pb_ref, first_tile, yb_ref, q_s, hist):
    k_s, v_s, ksw_s, vsw_s = hist
    ts = pb_ref.shape[0]
    half0 = lax.broadcasted_iota(jnp.int32, (BLOCK, BLOCK), 1) < HEAD_DIM
    bias_rest = jnp.where(_band_mask(False), 0.0, NEG_INF)
    bias_first = jnp.where(_band_mask(first_tile), 0.0, NEG_INF)
    for c in range(ts // BLOCK):
        rows = slice(c * BLOCK, (c + 1) * BLOCK)
        win = slice(c * BLOCK, (c + 2) * BLOCK)
        kk, vv, kk_sw, vv_sw = k_s[win, :], v_s[win, :], ksw_s[win, :], vsw_s[win, :]
        bias = bias_first if c == 0 else bias_rest
        outs, dens = [], []
        for h in range(SWA_Q_HEADS):
            g, chunk, half = h // SWA_REP, h // 2, h % 2
            qc = q_s[rows, chunk * BLOCK:(chunk + 1) * BLOCK]
            q_h = jnp.where(half0 if half == 0 else jnp.logical_not(half0), qc, jnp.zeros_like(qc))
            s = _dot_nt(q_h, kk if half == g else kk_sw) + bias
            sink = sink_ref[h] * LOG2E
            m = jnp.maximum(jnp.max(s, axis=-1, keepdims=True), sink)
            p = jnp.exp2(s - m)
            dens.append(jnp.sum(p, axis=-1, keepdims=True) + jnp.exp2(sink - m))
            outs.append(_dot(p.astype(BF16), vv if half == g else vv_sw))
        y = jnp.concatenate([jnp.where(half0, outs[2 * j], outs[2 * j + 1])
                             / jnp.where(half0, dens[2 * j], dens[2 * j + 1]) for j in range(3)], axis=1)
        gate = pb_ref[rows, MIX + 2 * BLOCK:COLS_B]
        yb_ref[rows, :] = (y * _silu(gate)).astype(yb_ref.dtype)
    for r in hist:
        r[0:BLOCK, :] = r[ts:ts + BLOCK, :]


N_SUB = 2
COLS_BC = COLS_B + COLS_C


def _inproj_mix_body(*refs, dec, tiles_per_seq, n_prev, n_cast):
    (sink_ref, x_ref, g_ref, w_ref, cos_ref, sin_ref, cw_ref, lg_ref, lb_ref, ws_ref, bs_ref) = refs[:11]
    refs = refs[11 + n_prev:]
    cast_in = refs[:n_cast]
    xq_ref, xt_ref = refs[n_cast:n_cast + 2]
    caches = refs[n_cast + 2:n_cast + 6]
    rest = refs[n_cast + 6 + dec.n_alias:]
    ya_ref, yb_ref, yc_ref, pd_ref, nc_ref, kv_ref = rest[:6]
    cast_out = rest[6:6 + n_cast]
    rest = rest[6 + n_cast:]
    dec_outs = rest[0:5]
    h_s, pa_s, pbc_s, tail_s, q_s = rest[5:10]
    hist = rest[10:14]
    tile = pl.program_id(0)
    sub = pl.program_id(1)
    step = tile * N_SUB + sub

    for src, dst in zip(cast_in, cast_out):
        dst[...] = src[...].astype(BF16)

    @pl.when(step == 0)
    def _():
        dec_outs[4][...] = jnp.zeros_like(dec_outs[4])
        tail_s[...] = jnp.zeros_like(tail_s)
        for r in hist:
            r[0:BLOCK, :] = jnp.zeros((BLOCK, BLOCK), BF16)

    first_tile = (tile % tiles_per_seq) == 0

    @pl.when(sub == 0)
    def _():
        h = _rms(x_ref[...], g_ref[...]).astype(BF16)
        h_s[...] = h

        def store(j, cw, acc, base):
            pbc_s[:, base + j:base + j + cw] = acc

        def store_d(j, cw, acc):
            pd_ref[:, j:j + cw] = acc.astype(pd_ref.dtype)

        _project(h, w_ref, OFF_C, COLS_C, functools.partial(store, base=COLS_B))
        _project(h, w_ref, OFF_B, COLS_B, functools.partial(store, base=0))
        _project(h, w_ref, OFF_D, COLS_D, store_d)
        _mix_c(pbc_s, COLS_B, lg_ref, lb_ref, ws_ref, bs_ref, yc_ref)
        _mix_b_prepare(pbc_s, cos_ref, sin_ref, kv_ref, q_s, hist)
        _decode_sequences(dec, step, sink_ref, xq_ref, xt_ref, caches, dec_outs)

    @pl.when(sub == 1)
    def _():
        def store_a(j, cw, acc):
            pa_s[:, j:j + cw] = acc

        _project(h_s[...], w_ref, OFF_A, COLS_A, store_a)

        prev2 = jnp.where(first_tile, 0.0, tail_s[0:1, :])
        prev1 = jnp.where(first_tile, 0.0, tail_s[1:2, :])
        prev2, prev1 = _mix_a(pa_s, cw_ref, ya_ref, prev2, prev1)
        last2 = jnp.concatenate([prev2, prev1], axis=0)
        tail_s[...] = last2
        nc_ref[...] = last2
        _mix_b_attend(sink_ref, pbc_s, first_tile, yb_ref, q_s, hist)
        _decode_sequences(dec, step, sink_ref, xq_ref, xt_ref, caches, dec_outs)


def _inproj_mix(x, g, w_bf16, w_layer, layer, depth, cos, sin, conv_w, ln_g, ln_b, w_s, b_tile, tm, seq, dec,
                prev_state, casts):
    t = x.shape[0]
    nb = t // seq
    tiles_per_seq = seq // tm
    steps = (t // tm) * N_SUB
    assert dec.n_sub == N_SUB
    row = lambda i, j: (i, 0)
    pos = lambda i, j: (i % tiles_per_seq, 0)
    per_seq = lambda i, j: (layer, i // tiles_per_seq, 0, 0)
    in_specs = ([pl.BlockSpec((tm, D_MODEL), row), _const_spec((1, D_MODEL)), _layer_spec(w_bf16.shape, w_layer),
                 pl.BlockSpec((tm, BLOCK), pos), pl.BlockSpec((tm, BLOCK), pos),
                 _const_spec((3, MIX)), _const_spec((1, MIX)), _const_spec((1, MIX)),
                 _const_spec((MIX // HEAD_DIM, BLOCK, BLOCK)), _const_spec((BLOCK, MIX))]
                + [pl.BlockSpec(memory_space=pl.ANY) for _ in prev_state])
    out_specs = ([pl.BlockSpec((tm, MIX), row)] * 3 + [pl.BlockSpec((tm, COLS_D), row),
                 pl.BlockSpec((None, None, 2, MIX), per_seq), pl.BlockSpec((None, None, KV_CH, BLOCK), per_seq)])
    out_shape = ([jax.ShapeDtypeStruct((t, MIX), BF16)] * 3 + [jax.ShapeDtypeStruct((t, COLS_D), BF16),
                 jax.ShapeDtypeStruct((depth, nb, 2, MIX), F32),
                 jax.ShapeDtypeStruct((depth, nb, KV_CH, BLOCK), F32)])
    args = [x, g, w_bf16, cos, sin, conv_w, ln_g, ln_b, w_s, b_tile, *prev_state]
    state_aliases = {1 + 10 + k: 4 + k for k in range(len(prev_state))}
    cast_bytes = 0
    for arr, n_rows, first_row in casts:
        slab = n_rows // steps
        assert slab * steps == n_rows and slab % 16 == 0 and first_row % slab == 0
        first = first_row // slab
        in_specs.append(pl.BlockSpec((slab, arr.shape[1]), lambda i, j, first=first: (first + i * N_SUB + j, 0)))
        out_specs.append(pl.BlockSpec((slab, arr.shape[1]), lambda i, j: (i * N_SUB + j, 0)))
        out_shape.append(jax.ShapeDtypeStruct((n_rows, arr.shape[1]), BF16))
        args.append(arr)
        cast_bytes += 2 * slab * arr.shape[1] * 6
    scratch = [pltpu.VMEM((tm, D_MODEL), BF16), pltpu.VMEM((tm, COLS_A), F32), pltpu.VMEM((tm, COLS_BC), F32),
               pltpu.VMEM((2, MIX), F32), pltpu.VMEM((tm, MIX), BF16)] + [pltpu.VMEM((BLOCK + tm, BLOCK), BF16)] * 4
    est = (2 * tm * D_MODEL * 4 + D_MODEL * IN_COLS * 2 + 2 * tm * (3 * MIX + COLS_D) * 2
           + tm * (D_MODEL * 2 + (COLS_A + COLS_BC) * 4 + MIX * 2) + 4 * (BLOCK + tm) * BLOCK * 2
           + 4 * tm * BLOCK * 4)
    in_specs, out_specs, out_shape, args, aliases, extra = dec.extend(in_specs, out_specs, out_shape, args)
    return pl.pallas_call(
        functools.partial(_inproj_mix_body, dec=dec.static, tiles_per_seq=tiles_per_seq,
                          n_prev=len(prev_state), n_cast=len(casts)),
        grid=(t // tm, N_SUB),
        in_specs=in_specs, out_specs=out_specs, out_shape=out_shape,
        scratch_shapes=scratch,
        input_output_aliases={**aliases, **state_aliases},
        compiler_params=pltpu.CompilerParams(dimension_semantics=("arbitrary", "arbitrary"),
                                             vmem_limit_bytes=_vmem_limit(est + extra + cast_bytes)),
        name="inproj_mix",
    )(*args)


def _mix_d_body(pd_ref, cos_ref, sin_ref, yd_ref, c1_ref, c4_ref, c16_ref,
                q_ref, k_ref, v_ref, o_ref, m_ref, d_ref):
    sb = pl.program_id(1)

    lw = 2 * HEAD_DIM
    n_groups = len(DILATIONS)

    @pl.when(jnp.logical_and(pl.program_id(0) == 0, sb == 0))
    def _():
        k_ref[:, 0:SUPER, :] = jnp.zeros((n_groups, SUPER, lw), F32)
        v_ref[:, 0:SUPER, :] = jnp.zeros((n_groups, SUPER, lw), F32)

    def prepare(c, carry):
        r0 = pl.multiple_of(c * ROW_CHUNK, ROW_CHUNK)
        rows = pl.ds(r0, ROW_CHUNK)
        new_rows = pl.ds(SUPER + r0, ROW_CHUNK)
        cos = cos_ref[rows, :]
        sin = sin_ref[rows, :]
        for g in range(n_groups):
            q_ref[g, rows, :] = _rope(pd_ref[rows, g * lw:(g + 1) * lw].astype(F32), cos, sin) * SCALE_LOG2
            k_ref[g, new_rows, :] = _rope(pd_ref[rows, MIX + g * lw:MIX + (g + 1) * lw].astype(F32), cos, sin)
            v_ref[g, new_rows, :] = pd_ref[rows, 2 * MIX + g * lw:2 * MIX + (g + 1) * lw].astype(F32)
        return carry

    lax.fori_loop(0, SUPER // ROW_CHUNK, prepare, 0)

    n_blocks = SUPER // BLOCK
    bias_rest = jnp.where(_band_mask(False), 0.0, NEG_INF)
    bias_first = jnp.where(_band_mask(sb == 0), 0.0, NEG_INF)
    for g, dil in enumerate(DILATIONS):
        for t in range(n_blocks):
            r = t % dil
            j = t // dil
            base = j * (BLOCK * dil) + r
            qs = q_ref[g, pl.ds(base, BLOCK, stride=dil), :]
            kstart = SUPER + base - BLOCK * dil
            ks = k_ref[g, pl.ds(kstart, 2 * BLOCK, stride=dil), :]
            vs = v_ref[g, pl.ds(kstart, 2 * BLOCK, stride=dil), :]
            bias = bias_first if j == 0 else bias_rest
            ks_bf = ks.astype(BF16)
            vs_bf = vs.astype(BF16)
            head0 = lax.broadcasted_iota(jnp.int32, (BLOCK, lw), 1) < HEAD_DIM
            o_h, m_h, d_h = [], [], []
            for hh in range(2):
                q_h = jnp.where(head0 if hh == 0 else jnp.logical_not(head0), qs, 0.0).astype(BF16)
                s = _dot_nt(q_h, ks_bf) + bias
                m = jnp.max(s, axis=-1, keepdims=True)
                p = jnp.exp2(s - m)
                o_h.append(_dot(p.astype(BF16), vs_bf))
                m_h.append(m)
                d_h.append(jnp.sum(p, axis=-1, keepdims=True))
            rows = pl.ds(base, BLOCK, stride=dil)
            o_ref[g, rows, :] = jnp.where(head0, o_h[0], o_h[1])
            m_ref[g, rows, :] = jnp.where(head0, m_h[0], m_h[1])
            d_ref[g, rows, :] = jnp.where(head0, d_h[0], d_h[1])

    def merge(c, carry):
        rows = pl.ds(pl.multiple_of(c * ROW_CHUNK, ROW_CHUNK), ROW_CHUNK)
        ms = [m_ref[g, rows, :] for g in range(n_groups)]
        mmax = jnp.maximum(jnp.maximum(ms[0], ms[1]), ms[2])
        es = [jnp.exp2(m - mmax) for m in ms]
        inv = 1.0 / (d_ref[0, rows, :] * es[0] + d_ref[1, rows, :] * es[1] + d_ref[2, rows, :] * es[2])
        for g in range(n_groups):
            gate = pd_ref[rows, 3 * MIX + g * lw:3 * MIX + (g + 1) * lw].astype(F32)
            y = o_ref[g, rows, :] * (es[g] * inv)
            yd_ref[rows, g * lw:(g + 1) * lw] = (y * _silu(gate)).astype(yd_ref.dtype)
        return carry

    lax.fori_loop(0, SUPER // ROW_CHUNK, merge, 0)

    last = pl.num_programs(1) - 1

    @pl.when(sb == last)
    def _():
        for g, (c_ref, dil) in enumerate(zip((c1_ref, c4_ref, c16_ref), DILATIONS)):
            n = BLOCK * dil
            for j in range(0, n, ROW_CHUNK if n >= ROW_CHUNK else n):
                w = min(ROW_CHUNK, n)
                src = slice(2 * SUPER - n + j, 2 * SUPER - n + j + w)
                c_ref[0:lw, j:j + w] = k_ref[g, src, :].T
                c_ref[lw:2 * lw, j:j + w] = v_ref[g, src, :].T

    @pl.when(sb < last)
    def _():
        k_ref[:, 0:SUPER, :] = k_ref[:, SUPER:2 * SUPER, :]
        v_ref[:, 0:SUPER, :] = v_ref[:, SUPER:2 * SUPER, :]


def _mix_d_entry(pd_ref, cos_ref, sin_ref, *rest, n_prev):
    _mix_d_body(pd_ref, cos_ref, sin_ref, *rest[n_prev:])


def _mix_d(pd, cos, sin, layer, depth, prev_caches):
    b, s, _ = pd.shape
    est = (2 * SUPER * COLS_D * 2 + 4 * SUPER * BLOCK * 4 + 2 * SUPER * MIX * 2
           + 2 * KV_CH * (BLOCK + 4 * BLOCK + SUPER) * 4 + (4 + 2 * 2) * SUPER * MIX * 4)
    tile = lambda bi, si: (bi, si, 0)
    cache = lambda bi, si: (layer, bi, 0, 0)
    return pl.pallas_call(
        functools.partial(_mix_d_entry, n_prev=len(prev_caches)),
        grid=(b, s // SUPER),
        in_specs=[pl.BlockSpec((None, SUPER, COLS_D), tile),
                  pl.BlockSpec((SUPER, BLOCK), lambda bi, si: (si, 0)),
                  pl.BlockSpec((SUPER, BLOCK), lambda bi, si: (si, 0))]
                 + [pl.BlockSpec(memory_space=pl.ANY) for _ in prev_caches],
        out_specs=[pl.BlockSpec((None, SUPER, MIX), tile)]
                  + [pl.BlockSpec((None, None, KV_CH, BLOCK * d), cache) for d in DILATIONS],
        out_shape=[jax.ShapeDtypeStruct((b, s, MIX), BF16)]
                  + [jax.ShapeDtypeStruct((depth, b, KV_CH, BLOCK * d), F32) for d in DILATIONS],
        input_output_aliases={3 + k: 1 + k for k in range(len(prev_caches))},
        scratch_shapes=[pltpu.VMEM((3, SUPER, 2 * HEAD_DIM), F32),
                        pltpu.VMEM((3, 2 * SUPER, 2 * HEAD_DIM), F32),
                        pltpu.VMEM((3, 2 * SUPER, 2 * HEAD_DIM), F32),
                        pltpu.VMEM((3, SUPER, 2 * HEAD_DIM), F32),
                        pltpu.VMEM((3, SUPER, 2 * HEAD_DIM), F32),
                        pltpu.VMEM((3, SUPER, 2 * HEAD_DIM), F32)],
        compiler_params=pltpu.CompilerParams(dimension_semantics=("arbitrary", "arbitrary"),
                                             vmem_limit_bytes=_vmem_limit(est)),
        name="mix_d",
    )(pd, cos, sin, *prev_caches)


def _shift_pass(c_ref, o_ref, i, ch0, new_b, per_tile):
    n_tiles = c_ref.shape[2] // BLOCK
    chans = pl.ds(ch0, 2 * HEAD_DIM)
    lane = lax.broadcasted_iota(jnp.int32, (2 * HEAD_DIM, BLOCK), 1)
    prev = None
    for t in range(n_tiles):
        x = c_ref[i, chans, t * BLOCK:(t + 1) * BLOCK]
        per_tile(t, x)
        r = pltpu.roll(x, BLOCK - 1, 1)
        if t > 0:
            o_ref[i, chans, (t - 1) * BLOCK:t * BLOCK] = jnp.where(lane < BLOCK - 1, prev, r)
        prev = r
    o_ref[i, chans, (n_tiles - 1) * BLOCK:n_tiles * BLOCK] = jnp.where(lane < BLOCK - 1, prev, new_b)


def _decode_cache(c_ref, o_ref, i, heads, k_new, v_new, dil):
    hd = HEAD_DIM
    n = c_ref.shape[2]
    n_tiles = n // BLOCK
    s_tiles = [[None] * (n_tiles + 1) for _ in heads]

    def scores(t, x):
        for qi, (g, q_b, _) in enumerate(heads):
            s_tiles[qi][t] = jnp.sum(x[g * hd:(g + 1) * hd, :] * q_b, axis=0, keepdims=True)

    _shift_pass(c_ref, o_ref, i, 0, k_new, scores)
    scores(n_tiles, k_new)

    lane = lax.broadcasted_iota(jnp.int32, (1, n + BLOCK), 1)
    valid = jnp.logical_or(jnp.logical_and(lane < n, (lane & (dil - 1)) == 0), lane == n + BLOCK - 1)
    probs, dens, lses = [], [], []
    for qi, (g, q_b, sink) in enumerate(heads):
        s = jnp.where(valid, jnp.concatenate(s_tiles[qi], axis=1), NEG_INF)
        m = jnp.max(s, axis=1, keepdims=True)
        if sink is not None:
            m = jnp.maximum(m, sink)
        p = jnp.exp(s - m)
        den = jnp.sum(p, axis=1, keepdims=True)
        if sink is not None:
            den = den + jnp.exp(sink - m)
        probs.append(p)
        dens.append(den)
        lses.append(m + jnp.log(den))

    accs = [jnp.zeros((hd, BLOCK), F32) for _ in heads]

    def weighted(t, x):
        for qi, (g, _, _) in enumerate(heads):
            accs[qi] = accs[qi] + x[g * hd:(g + 1) * hd, :] * probs[qi][:, t * BLOCK:(t + 1) * BLOCK]

    _shift_pass(c_ref, o_ref, i, 2 * hd, v_new, weighted)
    weighted(n_tiles, v_new)

    outs = [jnp.sum(accs[qi], axis=1, keepdims=True) / dens[qi] for qi in range(len(heads))]
    return outs, lses


XT_KB, XT_VB, XT_KD, XT_VD = 0, BLOCK, 2 * BLOCK, 2 * BLOCK + MIX
XT_ROWS = 2 * BLOCK + 2 * MIX


def _decode_prep_body(pa_ref, pb_ref, pc_ref, pd_ref, st_ref, cw_ref, lg_ref, lb_ref,
                      wd_ref, bd_ref, cos_ref, sin_ref,
                      ya_ref, yc_ref, nst_ref, vn_ref, xq_ref, xt_ref):
    a_b = pa_ref[:, 0:MIX]
    z = pa_ref[:, MIX:2 * MIX] * pa_ref[:, 2 * MIX:3 * MIX]
    prev0 = st_ref[:, 0:MIX]
    prev1 = st_ref[:, MIX:2 * MIX]
    conv = cw_ref[0:1, :] * prev0 + cw_ref[1:2, :] * prev1 + cw_ref[2:3, :] * z
    ya_ref[...] = a_b * conv * _silu(pa_ref[:, 3 * MIX:4 * MIX])
    nst_ref[:, 0:MIX] = prev1
    nst_ref[:, MIX:2 * MIX] = z

    c_v = pc_ref[:, MIX:2 * MIX]
    mu = jnp.mean(c_v, axis=-1, keepdims=True)
    cen = c_v - mu
    var = jnp.mean(cen * cen, axis=-1, keepdims=True)
    vn = cen * lax.rsqrt(var + EPS) * lg_ref[...] + lb_ref[...]
    vn_ref[...] = vn
    mixed = wd_ref[...] * vn + bd_ref[...]
    yc_ref[...] = pc_ref[:, 0:MIX] * mixed * _silu(pc_ref[:, 2 * MIX:3 * MIX])

    cos = cos_ref[...]
    sin = sin_ref[...]
    cos3, sin3 = _tile_lanes(cos, 3), _tile_lanes(sin, 3)
    xq_ref[0:MIX, :] = (_rope(pb_ref[:, 0:MIX], cos3, sin3) * SCALE).T.astype(BF16)
    xq_ref[MIX:2 * MIX, :] = (_rope(pd_ref[:, 0:MIX], cos3, sin3) * SCALE).T.astype(BF16)
    xt_ref[XT_KB:XT_VB, :] = _rope(pb_ref[:, MIX:MIX + BLOCK], cos, sin).T
    xt_ref[XT_VB:XT_KD, :] = pb_ref[:, MIX + BLOCK:MIX + 2 * BLOCK].T
    xt_ref[XT_KD:XT_VD, :] = _rope(pd_ref[:, MIX:2 * MIX], cos3, sin3).T
    xt_ref[XT_VD:XT_ROWS, :] = pd_ref[:, 2 * MIX:3 * MIX].T


def _inproj_sample_body(x_ref, g_ref, w_ref, st_ref, cw_ref, lg_ref, lb_ref, wd_ref, bd_ref, cos_ref, sin_ref,
                        ya_ref, yc_ref, nst_ref, vn_ref, xq_ref, xt_ref, gb_ref, gd_ref, *rest, cast):
    if cast:
        wbf_ref = rest[0]
        rest = rest[1:]
        for j in range(0, IN_COLS, MATMUL_COL_CHUNK):
            cw = min(MATMUL_COL_CHUNK, IN_COLS - j)
            wbf_ref[:, j:j + cw] = w_ref[:, j:j + cw].astype(BF16)
        w_ref = wbf_ref
    pa_s, pb_s, pc_s, pd_s = rest
    h = _rms(x_ref[...], g_ref[...]).astype(BF16)
    for p_s, off in zip((pa_s, pb_s, pc_s, pd_s), (OFF_A, OFF_B, OFF_C, OFF_D)):
        def store(j, cw, acc, p_s=p_s):
            p_s[:, j:j + cw] = acc
        _project(h, w_ref, off, p_s.shape[-1], store)
    _decode_prep_body(pa_s, pb_s, pc_s, pd_s, st_ref, cw_ref, lg_ref, lb_ref, wd_ref, bd_ref, cos_ref, sin_ref,
                      ya_ref, yc_ref, nst_ref, vn_ref, xq_ref, xt_ref)
    gb_ref[...] = pb_s[:, MIX + 2 * BLOCK:COLS_B]
    gd_ref[...] = pd_s[:, 3 * MIX:4 * MIX]


def _inproj_sample(x, g, w, layer, state, conv_w, ln_g, ln_b, w_diag, b_diag, cos, sin):
    ns = x.shape[0]
    assert ns == BLOCK
    cast = w.dtype != BF16
    shapes = [((ns, MIX), F32), ((ns, MIX), F32), ((ns, 2 * MIX), F32), ((ns, MIX), F32),
              ((2 * MIX, ns), BF16), ((XT_ROWS, ns), F32), ((ns, MIX), F32), ((ns, MIX), F32)]
    out_specs = [_const_spec(s) for s, _ in shapes]
    if cast:
        shapes.append(((1, D_MODEL, IN_COLS), BF16))
        out_specs.append(pl.BlockSpec((None, D_MODEL, IN_COLS), lambda i: (0, 0, 0)))
    small = (state, conv_w, ln_g, ln_b, w_diag, b_diag, cos, sin)
    est = D_MODEL * IN_COLS * (2 + 4 * cast) * 2 + 8 * ns * IN_COLS * 4
    return pl.pallas_call(
        functools.partial(_inproj_sample_body, cast=cast),
        grid=(1,),
        in_specs=[_const_spec(x.shape), _const_spec(g.shape), _layer_spec(w.shape, layer)]
                 + [_const_spec(a.shape) for a in small],
        out_specs=out_specs,
        out_shape=[jax.ShapeDtypeStruct(s, d) for s, d in shapes],
        scratch_shapes=[pltpu.VMEM((ns, wd), F32) for wd in (COLS_A, COLS_B, COLS_C, COLS_D)],
        compiler_params=pltpu.CompilerParams(dimension_semantics=("arbitrary",),
                                             vmem_limit_bytes=_vmem_limit(est)),
        name="inproj_sample",
    )(x, g, w, *small)


def _decode_sequences(dec, step, sink_ref, xq_ref, xt_ref, cache_refs, out_refs):
    cs_ref, c1_ref, c4_ref, c16_ref = cache_refs
    ns_ref, n1_ref, n4_ref, n16_ref, yt_ref = out_refs
    n_seq = xq_ref.shape[1]
    hd = HEAD_DIM

    for i in range(dec.seq_per_step):
        b = dec.seq_base + step * dec.seq_per_step + i
        onehot = (lax.broadcasted_iota(jnp.int32, (n_seq, BLOCK), 0) == b).astype(BF16)
        q_b = _dot(xq_ref[...], onehot)
        new_rot = pltpu.roll(xt_ref[...], BLOCK - 1 - b, 1)
        heads = [(h // SWA_REP, q_b[h * hd:(h + 1) * hd], jnp.full((1, 1), sink_ref[h], F32))
                 for h in range(SWA_Q_HEADS)]
        outs, _ = _decode_cache(cs_ref, ns_ref, i, heads, new_rot[XT_KB:XT_VB], new_rot[XT_VB:XT_KD], 1)
        yb_col = jnp.concatenate(outs, axis=0)

        q_col = q_b[MIX:2 * MIX]
        k_new = new_rot[XT_KD:XT_VD]
        v_new = new_rot[XT_VD:XT_ROWS]
        o_heads, l_heads = [], []
        for g, (c_ref, n_ref, dil) in enumerate(zip((c1_ref, c4_ref, c16_ref),
                                                    (n1_ref, n4_ref, n16_ref), DILATIONS)):
            gc = slice(2 * g * hd, 2 * (g + 1) * hd)
            heads = [(hh, q_col[(2 * g + hh) * hd:(2 * g + hh + 1) * hd], None) for hh in range(2)]
            o, lse = _decode_cache(c_ref, n_ref, i, heads, k_new[gc], v_new[gc], dil)
            o_heads += o
            l_heads += lse
        cols = []
        for hh in range(2):
            ls = [l_heads[2 * g + hh] for g in range(3)]
            lmax = jnp.maximum(jnp.maximum(ls[0], ls[1]), ls[2])
            es = [jnp.exp(l - lmax) for l in ls]
            esum = es[0] + es[1] + es[2]
            cols.append([o_heads[2 * g + hh] * (es[g] / esum) for g in range(3)])
        yd_col = jnp.concatenate([cols[hh][g] for g in range(3) for hh in range(2)], axis=0)
        y_col = jnp.concatenate([yb_col, yd_col], axis=0)
        lane_y = lax.broadcasted_iota(jnp.int32, (2 * MIX, n_seq), 1)
        yt_ref[...] = jnp.where(lane_y == b, y_col, yt_ref[...])


class _DecodeStatic(NamedTuple):
    seq_base: int
    seq_per_step: int
    n_alias: int


class _Decode:
    def __init__(self, layer, seq_base, seq_per_step, n_sub, sinks, xq, xt, caches, prev_outs):
        self.layer, self.seq_base, self.seq_per_step, self.n_sub = layer, seq_base, seq_per_step, n_sub
        self.sinks, self.xq, self.xt = sinks, xq, xt
        self.caches, self.prev_outs = tuple(caches), tuple(prev_outs)
        self.static = _DecodeStatic(seq_base, seq_per_step, len(self.prev_outs))

    def extend(self, in_specs, out_specs, out_shape, args):
        layer, block0, n_sub = self.layer, self.seq_base // self.seq_per_step, self.n_sub
        linear = (lambda i: i) if n_sub == 1 else (lambda i, j: i * n_sub + j)
        spec = lambda c: pl.BlockSpec((None, self.seq_per_step) + c.shape[2:],
                                      lambda *ids: (layer, block0 + linear(*ids), 0, 0))
        first_alias_in = 1 + len(args) + 2 + len(self.caches)
        aliases = {first_alias_in + i: len(out_specs) + i for i in range(len(self.prev_outs))}
        in_specs = ([pl.BlockSpec(memory_space=pltpu.SMEM)] + in_specs
                    + [_const_spec(self.xq.shape), _const_spec(self.xt.shape)]
                    + [spec(c) for c in self.caches]
                    + [pl.BlockSpec(memory_space=pl.ANY) for _ in self.prev_outs])
        args = [self.sinks] + args + [self.xq, self.xt, *self.caches, *self.prev_outs]
        yt_shape = (2 * MIX, self.xq.shape[1])
        out_specs = out_specs + [spec(c) for c in self.caches] + [_const_spec(yt_shape)]
        out_shape = (out_shape + [jax.ShapeDtypeStruct(c.shape, F32) for c in self.caches]
                     + [jax.ShapeDtypeStruct(yt_shape, F32)])
        rows = sum(c.shape[3] for c in self.caches)
        extra = 4 * self.seq_per_step * KV_CH * rows * 4 + 4 * (2 * MIX + XT_ROWS) * BLOCK * 4
        return in_specs, out_specs, out_shape, args, aliases, extra


def _rope_tables(pos):
    inv_freq = ROPE_THETA ** (-jnp.arange(HALF_HEAD, dtype=F32) / HALF_HEAD)
    ang = pos.astype(F32)[:, None] * inv_freq[None, :]
    cos, sin = jnp.cos(ang), jnp.sin(ang)
    return (jnp.concatenate([cos, cos, cos, cos], axis=1),
            jnp.concatenate([-sin, sin, -sin, sin], axis=1))


def _cache_view(c):
    d, b, n = c.shape[:3]
    return jnp.transpose(c, (0, 1, 3, 4, 5, 2)).reshape(d, b, KV_CH, n)


def _cache_unview(c):
    lead, n = c.shape[:-2], c.shape[-1]
    nl = len(lead)
    perm = tuple(range(nl)) + (nl + 3, nl, nl + 1, nl + 2)
    return jnp.transpose(c.reshape(lead + (2, 2, HEAD_DIM, n)), perm)


def kernel(x_prompt, x_sample, state_conv, cache_swa_kv, cache_dil1_kv, cache_dil4_kv, cache_dil16_kv,
           norm_g, w_in, conv_w, attn_sinks, v_ln_g, v_ln_b, w_spatial, b_spatial,
           w_branch, w_merge, w_out, final_norm_g):
    depth = w_in.shape[0]
    nb, seq, _ = x_prompt.shape
    ns, dec_seq, _ = x_sample.shape
    assert dec_seq == 1 and seq % SUPER == 0
    past_len = 16384
    assert cache_swa_kv.shape[2] == BLOCK and cache_dil16_kv.shape[2] == SUPER

    cos_p, sin_p = _rope_tables(jnp.arange(seq, dtype=jnp.int32))
    cos_s, sin_s = _rope_tables(past_len + jnp.arange(1, dtype=jnp.int32))
    assert depth == 2
    casts = ((w_in.reshape(depth * D_MODEL, IN_COLS), D_MODEL, D_MODEL),
             (w_merge.reshape(depth * D_MODEL, 4 * D_MODEL), depth * D_MODEL, 0),
             (w_branch.reshape(depth * 4 * MIX, D_MODEL), depth * 4 * MIX, 0),
             (w_out.reshape(depth * D_MODEL, D_MODEL), depth * D_MODEL, 0))
    w_in_l = w_in
    caches = [_cache_view(c) for c in (cache_swa_kv, cache_dil1_kv, cache_dil4_kv, cache_dil16_kv)]
    fg = final_norm_g.reshape(1, D_MODEL)

    xp = x_prompt.reshape(nb * seq, D_MODEL)
    xs = x_sample.reshape(ns, D_MODEL)
    conv_s, chunk_v = [], []
    state_p = ()
    dil_p = ()
    new_s = ()
    for l in range(depth):
        g = norm_g[l].reshape(1, D_MODEL)
        final = l == depth - 1
        ln_g = v_ln_g[l].reshape(1, MIX)
        ln_b = v_ln_b[l].reshape(1, MIX)

        w_diag = jnp.repeat(w_spatial[l][:, 0, 0], HEAD_DIM).reshape(1, MIX)
        b_diag = jnp.repeat(b_spatial[l][:, 0], HEAD_DIM).reshape(1, MIX)
        za, zc, nstate, vn, xq, xt, gate_b, gate_d, *w_new = _inproj_sample(
            xs, g, w_in_l, 0, state_conv[l].reshape(ns, 2 * MIX), conv_w[l], ln_g, ln_b, w_diag, b_diag,
            cos_s, sin_s)
        if w_new:
            w_in_l, = w_new

        tm = 512
        tiles = nb * seq // tm
        half = tiles * N_SUB
        assert ns - half == 2 * tiles
        b_tile = jnp.repeat(b_spatial[l].T, HEAD_DIM, axis=1)
        dec = _Decode(l, 0, 1, N_SUB, attn_sinks[l], xq, xt, caches, new_s)
        ya, yb, yc, pd, conv_p, swa_p, *rest = _inproj_mix(
            xp, g, w_in_l, 0, l, depth, cos_p, sin_p, conv_w[l], ln_g, ln_b, w_spatial[l], b_tile, tm, seq,
            dec, state_p, casts)
        state_p = (conv_p, swa_p)
        if casts:
            w_next, w_merge_bf, w_branch_bf, w_out_bf = rest[:4]
            w_in_next = w_next.reshape(1, D_MODEL, IN_COLS)
            w_merge_bf = w_merge_bf.reshape(depth, D_MODEL, 4 * D_MODEL)
            w_branch_bf = w_branch_bf.reshape(depth, 4, MIX, D_MODEL)
            w_out_bf = w_out_bf.reshape(depth, D_MODEL, D_MODEL)
            rest = rest[4:]
            casts = ()
        ns0, ns1, ns4, ns16, yt0 = rest
        yd, *dil_p = _mix_d(pd.reshape(nb, seq, COLS_D), cos_p, sin_p, l, depth, dil_p)
        dec = _Decode(l, half, 2, 1, attn_sinks[l], xq, xt, caches, (ns0, ns1, ns4, ns16))
        xp, *new_s, yt1 = _outproj(xp, (ya, yb, yc, yd.reshape(nb * seq, MIX)), g, w_merge_bf,
                                   w_branch_bf, w_out_bf, fg, l, tm, final, dec)

        xs = _outproj_sample(xs, za, zc, yt0, yt1, gate_b, gate_d, g, w_merge_bf, w_branch_bf, w_out_bf, fg,
                             l, final, half)
        conv_s.append(nstate.reshape(ns, 2, MIX))
        chunk_v.append(vn.reshape(ns, 1, MIX))
        w_in_l = w_in_next

    st = jnp.stack
    conv_p, swa_p = state_p
    new_p = [_cache_unview(c) for c in (swa_p, *dil_p)]
    new_s = [_cache_unview(c) for c in new_s]
    return (xp.reshape(nb, seq, D_MODEL), xs.reshape(ns, 1, D_MODEL),
            conv_p, st(conv_s),
            new_p[0], new_s[0], new_p[1], new_s[1],
            new_p[2], new_s[2], new_p[3], new_s[3],
            st(chunk_v))
```

```python
import functools
from typing import NamedTuple

import jax
import jax.numpy as jnp
from jax import lax
from jax.experimental import pallas as pl
from jax.experimental.pallas import tpu as pltpu

F32 = jnp.float32
BF16 = jnp.bfloat16

D_MODEL = 1024
HEAD_DIM = 64
HALF_HEAD = HEAD_DIM // 2
MIX = 384
BLOCK = 128
SWA_Q_HEADS = 6
SWA_KV_HEADS = 2
SWA_REP = SWA_Q_HEADS // SWA_KV_HEADS
DILATIONS = (1, 4, 16)
SUPER = BLOCK * DILATIONS[-1]
ROW_CHUNK = 256
ROPE_THETA = 10000.0
EPS = 1e-6
NEG_INF = -1e30
SCALE = HEAD_DIM ** -0.5
LOG2E = 1.4426950408889634
SCALE_LOG2 = SCALE * LOG2E

COLS_A = 4 * MIX
COLS_B = MIX + 2 * BLOCK + MIX
COLS_C = 3 * MIX
COLS_D = 4 * MIX
OFF_A, OFF_B, OFF_C, OFF_D = 0, COLS_A, COLS_A + COLS_B, COLS_A + COLS_B + COLS_C
IN_COLS = OFF_D + COLS_D
KV_CH = 4 * HEAD_DIM

V7X_VMEM_BYTES = 64 * 1024 * 1024
MATMUL_COL_CHUNK = 512


def _vmem_limit(estimate_bytes):
    return int(min(V7X_VMEM_BYTES - 8 * 1024 * 1024, estimate_bytes + 16 * 1024 * 1024))


def _const_spec(shape, single=False):
    nd = len(shape)
    if single:
        return pl.BlockSpec(shape, lambda *_: (0,) * nd, pipeline_mode=pl.Buffered(1))
    return pl.BlockSpec(shape, lambda *_: (0,) * nd)


def _layer_spec(stacked_shape, layer):
    nd = len(stacked_shape) - 1
    return pl.BlockSpec((None,) + tuple(stacked_shape[1:]), lambda *_: (layer,) + (0,) * nd,
                        pipeline_mode=pl.Buffered(1))


def _rms(x, g):
    ms = jnp.mean(x * x, axis=-1, keepdims=True)
    return x * lax.rsqrt(ms + EPS) * g


def _sigmoid(x):
    return 0.5 * jnp.tanh(0.5 * x) + 0.5


def _silu(x):
    h = 0.5 * x
    return h * jnp.tanh(h) + h


def _rope(x, cos, sin):
    w = x.shape[-1]
    lane = lax.broadcasted_iota(jnp.int32, x.shape, 1)
    fwd = pltpu.roll(x, w - HALF_HEAD, 1)
    bwd = pltpu.roll(x, HALF_HEAD, 1)
    partner = jnp.where((lane & HALF_HEAD) == 0, fwd, bwd)
    return x * cos + partner * sin


def _tile_lanes(t, reps):
    return t if reps == 1 else jnp.concatenate([t] * reps, axis=1)


def _dot(a, b):
    return jnp.dot(a, b, preferred_element_type=F32)


def _dot_nt(a, b):
    return lax.dot_general(a, b, (((1,), (1,)), ((), ())), preferred_element_type=F32)


def _project(h, w_ref, off, width, store):
    for j in range(0, width, MATMUL_COL_CHUNK):
        cw = min(MATMUL_COL_CHUNK, width - j)
        store(j, cw, _dot(h, w_ref[:, off + j:off + j + cw]))


def _outproj_body(*refs, final, dec):
    sink_ref = refs[0]
    (x_ref, ya_ref, yb_ref, yc_ref, yd_ref, g_ref, wm_ref, wb_ref, wo_ref, fg_ref) = refs[1:11]
    xq_ref, xt_ref = refs[11:13]
    o_ref = refs[17 + dec.n_alias]
    step = pl.program_id(0)
    dec_outs = refs[18 + dec.n_alias:]

    @pl.when(step == 0)
    def _():
        dec_outs[4][...] = jnp.zeros_like(dec_outs[4])

    _decode_sequences(dec, step, sink_ref, xq_ref, xt_ref, refs[13:17], dec_outs)
    ys = (ya_ref[...], yb_ref[...], yc_ref[...], yd_ref[...])
    o_ref[...] = _merge_project(x_ref[...], ys, g_ref, wm_ref, wb_ref, wo_ref, fg_ref, final)


def _merge_project(x, ys, g_ref, wm_ref, wb_ref, wo_ref, fg_ref, final):
    h = _rms(x, g_ref[...]).astype(BF16)
    merged = None
    for n, y in enumerate(ys):
        gate = _sigmoid(_dot(h, wm_ref[:, n * D_MODEL:(n + 1) * D_MODEL]))
        term = gate * _dot(y.astype(BF16), wb_ref[n])
        merged = term if merged is None else merged + term
    out = x + _dot(merged.astype(BF16), wo_ref[...])
    return _rms(out, fg_ref[...]) if final else out


def _outproj_sample_body(x_ref, ya_ref, yc_ref, yt0_ref, yt1_ref, gb_ref, gd_ref,
                         g_ref, wm_ref, wb_ref, wo_ref, fg_ref, o_ref, *, final, split):
    lane = lax.broadcasted_iota(jnp.int32, yt0_ref.shape, 1)
    yt = jnp.where(lane < split, yt0_ref[...], yt1_ref[...])
    yb = yt[0:MIX, :].T * _silu(gb_ref[...])
    yd = yt[MIX:2 * MIX, :].T * _silu(gd_ref[...])
    ys = (ya_ref[...], yb, yc_ref[...], yd)
    o_ref[...] = _merge_project(x_ref[...], ys, g_ref, wm_ref, wb_ref, wo_ref, fg_ref, final)


def _outproj_sample(x, ya, yc, yt0, yt1, gb, gd, g, wm, wb, wo, fg, layer, final, split):
    ns = x.shape[0]
    est = (4 * D_MODEL * D_MODEL + 4 * MIX * D_MODEL + D_MODEL * D_MODEL) * 2 + 24 * ns * D_MODEL * 4
    whole = lambda a: _const_spec(a.shape)
    return pl.pallas_call(
        functools.partial(_outproj_sample_body, final=final, split=split),
        grid=(1,),
        in_specs=[whole(a) for a in (x, ya, yc, yt0, yt1, gb, gd, g)]
                 + [_layer_spec(wm.shape, layer), _layer_spec(wb.shape, layer), _layer_spec(wo.shape, layer),
                    whole(fg)],
        out_specs=_const_spec((ns, D_MODEL)),
        out_shape=jax.ShapeDtypeStruct((ns, D_MODEL), F32),
        compiler_params=pltpu.CompilerParams(dimension_semantics=("arbitrary",),
                                             vmem_limit_bytes=_vmem_limit(est)),
        name="outproj_sample",
    )(x, ya, yc, yt0, yt1, gb, gd, g, wm, wb, wo, fg)


def _outproj(x, ys, g, wm, wb, wo, fg, layer, tm, final, dec):
    t = x.shape[0]
    ysize = jnp.dtype(ys[0].dtype).itemsize
    est = (4 * tm * D_MODEL * 4 + 2 * 4 * tm * MIX * ysize
           + (4 * D_MODEL * D_MODEL + 4 * MIX * D_MODEL + D_MODEL * D_MODEL) * 2
           + 4 * tm * D_MODEL * 4)
    row = lambda i: (i, 0)
    in_specs = ([pl.BlockSpec((tm, D_MODEL), row)]
                + [pl.BlockSpec((tm, MIX), row)] * 4
                + [_const_spec((1, D_MODEL)),
                   _layer_spec(wm.shape, layer), _layer_spec(wb.shape, layer), _layer_spec(wo.shape, layer),
                   _const_spec((1, D_MODEL))])
    out_specs = [pl.BlockSpec((tm, D_MODEL), row)]
    out_shape = [jax.ShapeDtypeStruct((t, D_MODEL), F32)]
    args = [x, *ys, g, wm, wb, wo, fg]
    in_specs, out_specs, out_shape, args, aliases, extra = dec.extend(in_specs, out_specs, out_shape, args)
    return pl.pallas_call(
        functools.partial(_outproj_body, final=final, dec=dec.static),
        grid=(t // tm,),
        in_specs=in_specs, out_specs=out_specs, out_shape=out_shape,
        input_output_aliases=aliases,
        compiler_params=pltpu.CompilerParams(dimension_semantics=("arbitrary",),
                                             vmem_limit_bytes=_vmem_limit(est + extra)),
        name="outproj_dec",
    )(*args)


def _mix_a(pa_ref, cw_ref, ya_ref, prev2, prev1):
    row = lax.broadcasted_iota(jnp.int32, (BLOCK, MIX), 0)
    for ck in range(pa_ref.shape[0] // BLOCK):
        rows = slice(ck * BLOCK, (ck + 1) * BLOCK)
        a_b = pa_ref[rows, 0:MIX]
        z = pa_ref[rows, MIX:2 * MIX] * pa_ref[rows, 2 * MIX:3 * MIX]
        a_gate = pa_ref[rows, 3 * MIX:4 * MIX]
        z1 = jnp.where(row == 0, prev1, pltpu.roll(z, 1, 0))
        z2 = jnp.where(row == 0, prev2, jnp.where(row == 1, prev1, pltpu.roll(z, 2, 0)))
        conv = cw_ref[0:1, :] * z2 + cw_ref[1:2, :] * z1 + cw_ref[2:3, :] * z
        ya_ref[rows, :] = (a_b * conv * _silu(a_gate)).astype(ya_ref.dtype)
        prev2 = z[BLOCK - 2:BLOCK - 1, :]
        prev1 = z[BLOCK - 1:BLOCK, :]
    return prev2, prev1


def _mix_c(pc_ref, col0, lg_ref, lb_ref, ws_ref, bs_ref, yc_ref):
    r = lax.broadcasted_iota(jnp.int32, (BLOCK, BLOCK), 0)
    c = lax.broadcasted_iota(jnp.int32, (BLOCK, BLOCK), 1)
    n_groups = MIX // HEAD_DIM
    w_causal = [jnp.where(r >= c, ws_ref[g], 0.0).astype(BF16) for g in range(n_groups)]
    for ck in range(pc_ref.shape[0] // BLOCK):
        rows = slice(ck * BLOCK, (ck + 1) * BLOCK)
        c_v = pc_ref[rows, col0 + MIX:col0 + 2 * MIX]
        mu = jnp.mean(c_v, axis=-1, keepdims=True)
        cen = c_v - mu
        var = jnp.mean(cen * cen, axis=-1, keepdims=True)
        vn_c = (cen * lax.rsqrt(var + EPS) * lg_ref[...] + lb_ref[...]).astype(BF16)
        mixed = jnp.concatenate(
            [_dot(w_causal[g], vn_c[:, g * HEAD_DIM:(g + 1) * HEAD_DIM]) for g in range(n_groups)],
            axis=1) + bs_ref[...]
        c_u = pc_ref[rows, col0:col0 + MIX]
        c_gate = pc_ref[rows, col0 + 2 * MIX:col0 + 3 * MIX]
        yc_ref[rows, :] = (c_u * mixed * _silu(c_gate)).astype(yc_ref.dtype)


def _band_mask(first_block):
    i = lax.broadcasted_iota(jnp.int32, (BLOCK, 2 * BLOCK), 0)
    j = lax.broadcasted_iota(jnp.int32, (BLOCK, 2 * BLOCK), 1)
    band = (j >= i) & (j <= i + BLOCK)
    return band & jnp.logical_or(jnp.logical_not(first_block), j >= BLOCK)


def _mix_b_prepare(pb_ref, cos_ref, sin_ref, kv_ref, q_s, hist):
    k_s, v_s, ksw_s, vsw_s = hist
    ts = pb_ref.shape[0]
    for r0 in range(0, ts, ROW_CHUNK):
        rows = slice(r0, r0 + ROW_CHUNK)
        dst = slice(BLOCK + r0, BLOCK + r0 + ROW_CHUNK)
        cos = cos_ref[rows, :]
        sin = sin_ref[rows, :]
        q_s[rows, :] = (_rope(pb_ref[rows, 0:MIX], _tile_lanes(cos, 3), _tile_lanes(sin, 3)) * SCALE_LOG2
                        ).astype(BF16)
        k = _rope(pb_ref[rows, MIX:MIX + BLOCK], cos, sin)
        v = pb_ref[rows, MIX + BLOCK:MIX + 2 * BLOCK]
        k_s[dst, :] = k.astype(BF16)
        v_s[dst, :] = v.astype(BF16)
        ksw_s[dst, :] = pltpu.roll(k, HEAD_DIM, 1).astype(BF16)
        vsw_s[dst, :] = pltpu.roll(v, HEAD_DIM, 1).astype(BF16)
        if r0 + ROW_CHUNK == ts:
            kv_ref[0:BLOCK, :] = k[ROW_CHUNK - BLOCK:ROW_CHUNK, :].T
            kv_ref[BLOCK:2 * BLOCK, :] = v[ROW_CHUNK - BLOCK:ROW_CHUNK, :].T


def _mix_b_attend(sink_ref, pb_ref, first_tile, yb_ref, q_s, hist):
    k_s, v_s, ksw_s, vsw_s = hist
    ts = pb_ref.shape[0]
    half0 = lax.broadcasted_iota(jnp.int32, (BLOCK, BLOCK), 1) < HEAD_DIM
    bias_rest = jnp.where(_band_mask(False), 0.0, NEG_INF)
    bias_first = jnp.where(_band_mask(first_tile), 0.0, NEG_INF)
    for c in range(ts // BLOCK):
        rows = slice(c * BLOCK, (c + 1) * BLOCK)
        win = slice(c * BLOCK, (c + 2) * BLOCK)
        kk, vv, kk_sw, vv_sw = k_s[win, :], v_s[win, :], ksw_s[win, :], vsw_s[win, :]
        bias = bias_first if c == 0 else bias_rest
        outs, dens = [], []
        for h in range(SWA_Q_HEADS):
            g, chunk, half = h // SWA_REP, h // 2, h % 2
            qc = q_s[rows, chunk * BLOCK:(chunk + 1) * BLOCK]
            q_h = jnp.where(half0 if half == 0 else jnp.logical_not(half0), qc, jnp.zeros_like(qc))
            s = _dot_nt(q_h, kk if half == g else kk_sw) + bias
            sink = sink_ref[h] * LOG2E
            m = jnp.maximum(jnp.max(s, axis=-1, keepdims=True), sink)
            p = jnp.exp2(s - m)
            dens.append(jnp.sum(p, axis=-1, keepdims=True) + jnp.exp2(sink - m))
            outs.append(_dot(p.astype(BF16), vv if half == g else vv_sw))
        y = jnp.concatenate([jnp.where(half0, outs[2 * j], outs[2 * j + 1])
                             / jnp.where(half0, dens[2 * j], dens[2 * j + 1]) for j in range(3)], axis=1)
        gate = pb_ref[rows, MIX + 2 * BLOCK:COLS_B]
        yb_ref[rows, :] = (y * _silu(gate)).astype(yb_ref.dtype)
    for r in hist:
        r[0:BLOCK, :] = r[ts:ts + BLOCK, :]


N_SUB = 2
COLS_BC = COLS_B + COLS_C
N_MIX_OUT = 9
D_GROUPS = len(DILATIONS)
D_LANES = 2 * HEAD_DIM


def _mix_d_prepare(pd_ref, cos_ref, sin_ref, qd_ref, kd_ref, vd_ref, gd_ref):
    for r0 in range(0, pd_ref.shape[0], ROW_CHUNK):
        rows = slice(r0, r0 + ROW_CHUNK)
        cos = cos_ref[rows, :]
        sin = sin_ref[rows, :]
        for g in range(D_GROUPS):
            lanes = slice(g * D_LANES, (g + 1) * D_LANES)
            qd_ref[g, rows, :] = _rope(pd_ref[rows, lanes], cos, sin) * SCALE_LOG2
            kd_ref[g, rows, :] = _rope(pd_ref[rows, MIX + g * D_LANES:MIX + (g + 1) * D_LANES], cos, sin)
            vd_ref[g, rows, :] = pd_ref[rows, 2 * MIX + g * D_LANES:2 * MIX + (g + 1) * D_LANES]
        gd_ref[rows, :] = pd_ref[rows, 3 * MIX:4 * MIX].astype(gd_ref.dtype)


def _inproj_mix_body(*refs, dec, tiles_per_seq, n_prev, n_cast):
    (sink_ref, x_ref, g_ref, w_ref, cos_ref, sin_ref, cw_ref, lg_ref, lb_ref, ws_ref, bs_ref) = refs[:11]
    refs = refs[11 + n_prev:]
    cast_in = refs[:n_cast]
    xq_ref, xt_ref = refs[n_cast:n_cast + 2]
    caches = refs[n_cast + 2:n_cast + 6]
    rest = refs[n_cast + 6 + dec.n_alias:]
    ya_ref, yb_ref, yc_ref, qd_ref, kd_ref, vd_ref, gd_ref, nc_ref, kv_ref = rest[:N_MIX_OUT]
    cast_out = rest[N_MIX_OUT:N_MIX_OUT + n_cast]
    rest = rest[N_MIX_OUT + n_cast:]
    dec_outs = rest[0:5]
    h_s, pa_s, pbc_s, pd_s, tail_s, q_s = rest[5:11]
    hist = rest[11:15]
    tile = pl.program_id(0)
    sub = pl.program_id(1)
    step = tile * N_SUB + sub

    for src, dst in zip(cast_in, cast_out):
        dst[...] = src[...].astype(BF16)

    @pl.when(step == 0)
    def _():
        dec_outs[4][...] = jnp.zeros_like(dec_outs[4])
        tail_s[...] = jnp.zeros_like(tail_s)
        for r in hist:
            r[0:BLOCK, :] = jnp.zeros((BLOCK, BLOCK), BF16)

    first_tile = (tile % tiles_per_seq) == 0

    @pl.when(sub == 0)
    def _():
        h = _rms(x_ref[...], g_ref[...]).astype(BF16)
        h_s[...] = h

        def store(j, cw, acc, base):
            pbc_s[:, base + j:base + j + cw] = acc

        def store_d(j, cw, acc):
            pd_s[:, j:j + cw] = acc

        _project(h, w_ref, OFF_C, COLS_C, functools.partial(store, base=COLS_B))
        _project(h, w_ref, OFF_B, COLS_B, functools.partial(store, base=0))
        _project(h, w_ref, OFF_D, COLS_D, store_d)
        _mix_c(pbc_s, COLS_B, lg_ref, lb_ref, ws_ref, bs_ref, yc_ref)
        _mix_b_prepare(pbc_s, cos_ref, sin_ref, kv_ref, q_s, hist)
        _mix_d_prepare(pd_s, cos_ref, sin_ref, qd_ref, kd_ref, vd_ref, gd_ref)
        _decode_sequences(dec, step, sink_ref, xq_ref, xt_ref, caches, dec_outs)

    @pl.when(sub == 1)
    def _():
        def store_a(j, cw, acc):
            pa_s[:, j:j + cw] = acc

        _project(h_s[...], w_ref, OFF_A, COLS_A, store_a)

        prev2 = jnp.where(first_tile, 0.0, tail_s[0:1, :])
        prev1 = jnp.where(first_tile, 0.0, tail_s[1:2, :])
        prev2, prev1 = _mix_a(pa_s, cw_ref, ya_ref, prev2, prev1)
        last2 = jnp.concatenate([prev2, prev1], axis=0)
        tail_s[...] = last2
        nc_ref[...] = last2
        _mix_b_attend(sink_ref, pbc_s, first_tile, yb_ref, q_s, hist)
        _decode_sequences(dec, step, sink_ref, xq_ref, xt_ref, caches, dec_outs)


def _inproj_mix(x, g, w_bf16, w_layer, layer, depth, cos, sin, conv_w, ln_g, ln_b, w_s, b_tile, tm, seq, dec,
                prev_state, casts):
    t = x.shape[0]
    nb = t // seq
    tiles_per_seq = seq // tm
    steps = (t // tm) * N_SUB
    assert dec.n_sub == N_SUB
    row = lambda i, j: (i, 0)
    pos = lambda i, j: (i % tiles_per_seq, 0)
    per_seq = lambda i, j: (layer, i // tiles_per_seq, 0, 0)
    in_specs = ([pl.BlockSpec((tm, D_MODEL), row), _const_spec((1, D_MODEL)), _layer_spec(w_bf16.shape, w_layer),
                 pl.BlockSpec((tm, BLOCK), pos), pl.BlockSpec((tm, BLOCK), pos),
                 _const_spec((3, MIX)), _const_spec((1, MIX)), _const_spec((1, MIX)),
                 _const_spec((MIX // HEAD_DIM, BLOCK, BLOCK)), _const_spec((BLOCK, MIX))]
                + [pl.BlockSpec(memory_space=pl.ANY) for _ in prev_state])
    grouped = pl.BlockSpec((D_GROUPS, tm, D_LANES), lambda i, j: (0, i, 0))
    out_specs = ([pl.BlockSpec((tm, MIX), row)] * 3 + [grouped] * 3 + [pl.BlockSpec((tm, MIX), row),
                 pl.BlockSpec((None, None, 2, MIX), per_seq), pl.BlockSpec((None, None, KV_CH, BLOCK), per_seq)])
    out_shape = ([jax.ShapeDtypeStruct((t, MIX), BF16)] * 3
                 + [jax.ShapeDtypeStruct((D_GROUPS, t, D_LANES), F32)] * 3
                 + [jax.ShapeDtypeStruct((t, MIX), BF16),
                    jax.ShapeDtypeStruct((depth, nb, 2, MIX), F32),
                    jax.ShapeDtypeStruct((depth, nb, KV_CH, BLOCK), F32)])
    assert len(out_specs) == N_MIX_OUT
    args = [x, g, w_bf16, cos, sin, conv_w, ln_g, ln_b, w_s, b_tile, *prev_state]
    state_aliases = {1 + 10 + k: N_MIX_OUT - 2 + k for k in range(len(prev_state))}
    cast_bytes = 0
    for arr, n_rows, first_row in casts:
        slab = n_rows // steps
        assert slab * steps == n_rows and slab % 16 == 0 and first_row % slab == 0
        first = first_row // slab
        in_specs.append(pl.BlockSpec((slab, arr.shape[1]), lambda i, j, first=first: (first + i * N_SUB + j, 0)))
        out_specs.append(pl.BlockSpec((slab, arr.shape[1]), lambda i, j: (i * N_SUB + j, 0)))
        out_shape.append(jax.ShapeDtypeStruct((n_rows, arr.shape[1]), BF16))
        args.append(arr)
        cast_bytes += 2 * slab * arr.shape[1] * 6
    scratch = [pltpu.VMEM((tm, D_MODEL), BF16), pltpu.VMEM((tm, COLS_A), F32), pltpu.VMEM((tm, COLS_BC), F32),
               pltpu.VMEM((tm, COLS_D), F32), pltpu.VMEM((2, MIX), F32), pltpu.VMEM((tm, MIX), BF16)
               ] + [pltpu.VMEM((BLOCK + tm, BLOCK), BF16)] * 4
    est = (2 * tm * D_MODEL * 4 + D_MODEL * IN_COLS * 2 + 2 * tm * (4 * MIX * 2 + 3 * MIX * 4)
           + tm * (D_MODEL * 2 + IN_COLS * 4 + MIX * 2) + 4 * (BLOCK + tm) * BLOCK * 2
           + 4 * tm * BLOCK * 4)
    in_specs, out_specs, out_shape, args, aliases, extra = dec.extend(in_specs, out_specs, out_shape, args)
    return pl.pallas_call(
        functools.partial(_inproj_mix_body, dec=dec.static, tiles_per_seq=tiles_per_seq,
                          n_prev=len(prev_state), n_cast=len(casts)),
        grid=(t // tm, N_SUB),
        in_specs=in_specs, out_specs=out_specs, out_shape=out_shape,
        scratch_shapes=scratch,
        input_output_aliases={**aliases, **state_aliases},
        compiler_params=pltpu.CompilerParams(dimension_semantics=("arbitrary", "arbitrary"),
                                             vmem_limit_bytes=_vmem_limit(est + extra + cast_bytes)),
        name="inproj_mix",
    )(*args)


def _mix_d_body(q_ref, kc_ref, kp_ref, vc_ref, vp_ref, gd_ref, yd_ref, c1_ref, c4_ref, c16_ref,
                o_ref, m_ref, d_ref):
    sb = pl.program_id(1)
    lw = D_LANES
    n_groups = D_GROUPS

    def window(cur, prev, g, base, dil, j):
        if j > 0:
            return cur[g, pl.ds(base - BLOCK * dil, 2 * BLOCK, stride=dil), :]
        return jnp.concatenate([prev[g, pl.ds(SUPER - BLOCK * dil + base, BLOCK, stride=dil), :],
                                cur[g, pl.ds(base, BLOCK, stride=dil), :]], axis=0)

    n_blocks = SUPER // BLOCK
    bias_rest = jnp.where(_band_mask(False), 0.0, NEG_INF)
    bias_first = jnp.where(_band_mask(sb == 0), 0.0, NEG_INF)
    for g, dil in enumerate(DILATIONS):
        for t in range(n_blocks):
            r = t % dil
            j = t // dil
            base = j * (BLOCK * dil) + r
            qs = q_ref[g, pl.ds(base, BLOCK, stride=dil), :]
            bias = bias_first if j == 0 else bias_rest
            ks_bf = window(kc_ref, kp_ref, g, base, dil, j).astype(BF16)
            vs_bf = window(vc_ref, vp_ref, g, base, dil, j).astype(BF16)
            head0 = lax.broadcasted_iota(jnp.int32, (BLOCK, lw), 1) < HEAD_DIM
            o_h, m_h, d_h = [], [], []
            for hh in range(2):
                q_h = jnp.where(head0 if hh == 0 else jnp.logical_not(head0), qs, 0.0).astype(BF16)
                s = _dot_nt(q_h, ks_bf) + bias
                m = jnp.max(s, axis=-1, keepdims=True)
                p = jnp.exp2(s - m)
                o_h.append(_dot(p.astype(BF16), vs_bf))
                m_h.append(m)
                d_h.append(jnp.sum(p, axis=-1, keepdims=True))
            rows = pl.ds(base, BLOCK, stride=dil)
            o_ref[g, rows, :] = jnp.where(head0, o_h[0], o_h[1])
            m_ref[g, rows, :] = jnp.where(head0, m_h[0], m_h[1])
            d_ref[g, rows, :] = jnp.where(head0, d_h[0], d_h[1])

    def merge(c, carry):
        rows = pl.ds(pl.multiple_of(c * ROW_CHUNK, ROW_CHUNK), ROW_CHUNK)
        ms = [m_ref[g, rows, :] for g in range(n_groups)]
        mmax = jnp.maximum(jnp.maximum(ms[0], ms[1]), ms[2])
        es = [jnp.exp2(m - mmax) for m in ms]
        inv = 1.0 / (d_ref[0, rows, :] * es[0] + d_ref[1, rows, :] * es[1] + d_ref[2, rows, :] * es[2])
        for g in range(n_groups):
            gate = gd_ref[rows, g * lw:(g + 1) * lw].astype(F32)
            y = o_ref[g, rows, :] * (es[g] * inv)
            yd_ref[rows, g * lw:(g + 1) * lw] = (y * _silu(gate)).astype(yd_ref.dtype)
        return carry

    lax.fori_loop(0, SUPER // ROW_CHUNK, merge, 0)

    @pl.when(sb == pl.num_programs(1) - 1)
    def _():
        for g, (c_ref, dil) in enumerate(zip((c1_ref, c4_ref, c16_ref), DILATIONS)):
            n = BLOCK * dil
            for j in range(0, n, ROW_CHUNK if n >= ROW_CHUNK else n):
                w = min(ROW_CHUNK, n)
                src = slice(SUPER - n + j, SUPER - n + j + w)
                c_ref[0:lw, j:j + w] = kc_ref[g, src, :].T
                c_ref[lw:2 * lw, j:j + w] = vc_ref[g, src, :].T


def _mix_d_entry(q_ref, kc_ref, kp_ref, vc_ref, vp_ref, gd_ref, *rest, n_prev):
    _mix_d_body(q_ref, kc_ref, kp_ref, vc_ref, vp_ref, gd_ref, *rest[n_prev:])


def _mix_d(qd, kd, vd, gd, layer, depth, prev_caches):
    _, b, s, _ = qd.shape
    slab = D_GROUPS * SUPER * D_LANES * 4
    est = (2 * 5 * slab + 3 * slab + 2 * 2 * SUPER * MIX * 2
           + 2 * KV_CH * (BLOCK + 4 * BLOCK + SUPER) * 4)
    cur = pl.BlockSpec((D_GROUPS, None, SUPER, D_LANES), lambda bi, si: (0, bi, si, 0))
    prev = pl.BlockSpec((D_GROUPS, None, SUPER, D_LANES), lambda bi, si: (0, bi, jnp.maximum(si - 1, 0), 0))
    tile = lambda bi, si: (bi, si, 0)
    cache = lambda bi, si: (layer, bi, 0, 0)
    return pl.pallas_call(
        functools.partial(_mix_d_entry, n_prev=len(prev_caches)),
        grid=(b, s // SUPER),
        in_specs=[cur, cur, prev, cur, prev, pl.BlockSpec((None, SUPER, MIX), tile)]
                 + [pl.BlockSpec(memory_space=pl.ANY) for _ in prev_caches],
        out_specs=[pl.BlockSpec((None, SUPER, MIX), tile)]
                  + [pl.BlockSpec((None, None, KV_CH, BLOCK * d), cache) for d in DILATIONS],
        out_shape=[jax.ShapeDtypeStruct((b, s, MIX), BF16)]
                  + [jax.ShapeDtypeStruct((depth, b, KV_CH, BLOCK * d), F32) for d in DILATIONS],
        input_output_aliases={6 + k: 1 + k for k in range(len(prev_caches))},
        scratch_shapes=[pltpu.VMEM((D_GROUPS, SUPER, D_LANES), F32)] * 3,
        compiler_params=pltpu.CompilerParams(dimension_semantics=("arbitrary", "arbitrary"),
                                             vmem_limit_bytes=_vmem_limit(est)),
        name="mix_d",
    )(qd, kd, kd, vd, vd, gd, *prev_caches)


def _shift_pass(c_ref, o_ref, i, ch0, new_b, per_tile):
    n_tiles = c_ref.shape[2] // BLOCK
    chans = pl.ds(ch0, 2 * HEAD_DIM)
    lane = lax.broadcasted_iota(jnp.int32, (2 * HEAD_DIM, BLOCK), 1)
    prev = None
    for t in range(n_tiles):
        x = c_ref[i, chans, t * BLOCK:(t + 1) * BLOCK]
        per_tile(t, x)
        r = pltpu.roll(x, BLOCK - 1, 1)
        if t > 0:
            o_ref[i, chans, (t - 1) * BLOCK:t * BLOCK] = jnp.where(lane < BLOCK - 1, prev, r)
        prev = r
    o_ref[i, chans, (n_tiles - 1) * BLOCK:n_tiles * BLOCK] = jnp.where(lane < BLOCK - 1, prev, new_b)


def _decode_cache(c_ref, o_ref, i, heads, k_new, v_new, dil):
    hd = HEAD_DIM
    n = c_ref.shape[2]
    n_tiles = n // BLOCK
    s_tiles = [[None] * (n_tiles + 1) for _ in heads]

    def scores(t, x):
        for qi, (g, q_b, _) in enumerate(heads):
            s_tiles[qi][t] = jnp.sum(x[g * hd:(g + 1) * hd, :] * q_b, axis=0, keepdims=True)

    _shift_pass(c_ref, o_ref, i, 0, k_new, scores)
    scores(n_tiles, k_new)

    lane = lax.broadcasted_iota(jnp.int32, (1, n + BLOCK), 1)
    valid = jnp.logical_or(jnp.logical_and(lane < n, (lane & (dil - 1)) == 0), lane == n + BLOCK - 1)
    probs, dens, lses = [], [], []
    for qi, (g, q_b, sink) in enumerate(heads):
        s = jnp.where(valid, jnp.concatenate(s_tiles[qi], axis=1), NEG_INF)
        m = jnp.max(s, axis=1, keepdims=True)
        if sink is not None:
            m = jnp.maximum(m, sink)
        p = jnp.exp(s - m)
        den = jnp.sum(p, axis=1, keepdims=True)
        if sink is not None:
            den = den + jnp.exp(sink - m)
        probs.append(p)
        dens.append(den)
        lses.append(m + jnp.log(den))

    accs = [jnp.zeros((hd, BLOCK), F32) for _ in heads]

    def weighted(t, x):
        for qi, (g, _, _) in enumerate(heads):
            accs[qi] = accs[qi] + x[g * hd:(g + 1) * hd, :] * probs[qi][:, t * BLOCK:(t + 1) * BLOCK]

    _shift_pass(c_ref, o_ref, i, 2 * hd, v_new, weighted)
    weighted(n_tiles, v_new)

    outs = [jnp.sum(accs[qi], axis=1, keepdims=True) / dens[qi] for qi in range(len(heads))]
    return outs, lses


XT_KB, XT_VB, XT_KD, XT_VD = 0, BLOCK, 2 * BLOCK, 2 * BLOCK + MIX
XT_ROWS = 2 * BLOCK + 2 * MIX


def _decode_prep_body(pa_ref, pb_ref, pc_ref, pd_ref, st_ref, cw_ref, lg_ref, lb_ref,
                      wd_ref, bd_ref, cos_ref, sin_ref,
                      ya_ref, yc_ref, nst_ref, vn_ref, xq_ref, xt_ref):
    a_b = pa_ref[:, 0:MIX]
    z = pa_ref[:, MIX:2 * MIX] * pa_ref[:, 2 * MIX:3 * MIX]
    prev0 = st_ref[:, 0:MIX]
    prev1 = st_ref[:, MIX:2 * MIX]
    conv = cw_ref[0:1, :] * prev0 + cw_ref[1:2, :] * prev1 + cw_ref[2:3, :] * z
    ya_ref[...] = a_b * conv * _silu(pa_ref[:, 3 * MIX:4 * MIX])
    nst_ref[:, 0:MIX] = prev1
    nst_ref[:, MIX:2 * MIX] = z

    c_v = pc_ref[:, MIX:2 * MIX]
    mu = jnp.mean(c_v, axis=-1, keepdims=True)
    cen = c_v - mu
    var = jnp.mean(cen * cen, axis=-1, keepdims=True)
    vn = cen * lax.rsqrt(var + EPS) * lg_ref[...] + lb_ref[...]
    vn_ref[...] = vn
    mixed = wd_ref[...] * vn + bd_ref[...]
    yc_ref[...] = pc_ref[:, 0:MIX] * mixed * _silu(pc_ref[:, 2 * MIX:3 * MIX])

    cos = cos_ref[...]
    sin = sin_ref[...]
    cos3, sin3 = _tile_lanes(cos, 3), _tile_lanes(sin, 3)
    xq_ref[0:MIX, :] = (_rope(pb_ref[:, 0:MIX], cos3, sin3) * SCALE).T.astype(BF16)
    xq_ref[MIX:2 * MIX, :] = (_rope(pd_ref[:, 0:MIX], cos3, sin3) * SCALE).T.astype(BF16)
    xt_ref[XT_KB:XT_VB, :] = _rope(pb_ref[:, MIX:MIX + BLOCK], cos, sin).T
    xt_ref[XT_VB:XT_KD, :] = pb_ref[:, MIX + BLOCK:MIX + 2 * BLOCK].T
    xt_ref[XT_KD:XT_VD, :] = _rope(pd_ref[:, MIX:2 * MIX], cos3, sin3).T
    xt_ref[XT_VD:XT_ROWS, :] = pd_ref[:, 2 * MIX:3 * MIX].T


def _inproj_sample_body(x_ref, g_ref, w_ref, st_ref, cw_ref, lg_ref, lb_ref, wd_ref, bd_ref, cos_ref, sin_ref,
                        ya_ref, yc_ref, nst_ref, vn_ref, xq_ref, xt_ref, gb_ref, gd_ref, *rest, cast):
    if cast:
        wbf_ref = rest[0]
        rest = rest[1:]
        for j in range(0, IN_COLS, MATMUL_COL_CHUNK):
            cw = min(MATMUL_COL_CHUNK, IN_COLS - j)
            wbf_ref[:, j:j + cw] = w_ref[:, j:j + cw].astype(BF16)
        w_ref = wbf_ref
    pa_s, pb_s, pc_s, pd_s = rest
    h = _rms(x_ref[...], g_ref[...]).astype(BF16)
    for p_s, off in zip((pa_s, pb_s, pc_s, pd_s), (OFF_A, OFF_B, OFF_C, OFF_D)):
        def store(j, cw, acc, p_s=p_s):
            p_s[:, j:j + cw] = acc
        _project(h, w_ref, off, p_s.shape[-1], store)
    _decode_prep_body(pa_s, pb_s, pc_s, pd_s, st_ref, cw_ref, lg_ref, lb_ref, wd_ref, bd_ref, cos_ref, sin_ref,
                      ya_ref, yc_ref, nst_ref, vn_ref, xq_ref, xt_ref)
    gb_ref[...] = pb_s[:, MIX + 2 * BLOCK:COLS_B]
    gd_ref[...] = pd_s[:, 3 * MIX:4 * MIX]


def _inproj_sample(x, g, w, layer, state, conv_w, ln_g, ln_b, w_diag, b_diag, cos, sin):
    ns = x.shape[0]
    assert ns == BLOCK
    cast = w.dtype != BF16
    shapes = [((ns, MIX), F32), ((ns, MIX), F32), ((ns, 2 * MIX), F32), ((ns, MIX), F32),
              ((2 * MIX, ns), BF16), ((XT_ROWS, ns), F32), ((ns, MIX), F32), ((ns, MIX), F32)]
    out_specs = [_const_spec(s) for s, _ in shapes]
    if cast:
        shapes.append(((1, D_MODEL, IN_COLS), BF16))
        out_specs.append(pl.BlockSpec((None, D_MODEL, IN_COLS), lambda i: (0, 0, 0)))
    small = (state, conv_w, ln_g, ln_b, w_diag, b_diag, cos, sin)
    est = D_MODEL * IN_COLS * (2 + 4 * cast) * 2 + 8 * ns * IN_COLS * 4
    return pl.pallas_call(
        functools.partial(_inproj_sample_body, cast=cast),
        grid=(1,),
        in_specs=[_const_spec(x.shape), _const_spec(g.shape), _layer_spec(w.shape, layer)]
                 + [_const_spec(a.shape) for a in small],
        out_specs=out_specs,
        out_shape=[jax.ShapeDtypeStruct(s, d) for s, d in shapes],
        scratch_shapes=[pltpu.VMEM((ns, wd), F32) for wd in (COLS_A, COLS_B, COLS_C, COLS_D)],
        compiler_params=pltpu.CompilerParams(dimension_semantics=("arbitrary",),
                                             vmem_limit_bytes=_vmem_limit(est)),
        name="inproj_sample",
    )(x, g, w, *small)


def _decode_sequences(dec, step, sink_ref, xq_ref, xt_ref, cache_refs, out_refs):
    cs_ref, c1_ref, c4_ref, c16_ref = cache_refs
    ns_ref, n1_ref, n4_ref, n16_ref, yt_ref = out_refs
    n_seq = xq_ref.shape[1]
    hd = HEAD_DIM

    for i in range(dec.seq_per_step):
        b = dec.seq_base + step * dec.seq_per_step + i
        onehot = (lax.broadcasted_iota(jnp.int32, (n_seq, BLOCK), 0) == b).astype(BF16)
        q_b = _dot(xq_ref[...], onehot)
        new_rot = pltpu.roll(xt_ref[...], BLOCK - 1 - b, 1)
        heads = [(h // SWA_REP, q_b[h * hd:(h + 1) * hd], jnp.full((1, 1), sink_ref[h], F32))
                 for h in range(SWA_Q_HEADS)]
        outs, _ = _decode_cache(cs_ref, ns_ref, i, heads, new_rot[XT_KB:XT_VB], new_rot[XT_VB:XT_KD], 1)
        yb_col = jnp.concatenate(outs, axis=0)

        q_col = q_b[MIX:2 * MIX]
        k_new = new_rot[XT_KD:XT_VD]
        v_new = new_rot[XT_VD:XT_ROWS]
        o_heads, l_heads = [], []
        for g, (c_ref, n_ref, dil) in enumerate(zip((c1_ref, c4_ref, c16_ref),
                                                    (n1_ref, n4_ref, n16_ref), DILATIONS)):
            gc = slice(2 * g * hd, 2 * (g + 1) * hd)
            heads = [(hh, q_col[(2 * g + hh) * hd:(2 * g + hh + 1) * hd], None) for hh in range(2)]
            o, lse = _decode_cache(c_ref, n_ref, i, heads, k_new[gc], v_new[gc], dil)
            o_heads += o
            l_heads += lse
        cols = []
        for hh in range(2):
            ls = [l_heads[2 * g + hh] for g in range(3)]
            lmax = jnp.maximum(jnp.maximum(ls[0], ls[1]), ls[2])
            es = [jnp.exp(l - lmax) for l in ls]
            esum = es[0] + es[1] + es[2]
            cols.append([o_heads[2 * g + hh] * (es[g] / esum) for g in range(3)])
        yd_col = jnp.concatenate([cols[hh][g] for g in range(3) for hh in range(2)], axis=0)
        y_col = jnp.concatenate([yb_col, yd_col], axis=0)
        lane_y = lax.broadcasted_iota(jnp.int32, (2 * MIX, n_seq), 1)
        yt_ref[...] = jnp.where(lane_y == b, y_col, yt_ref[...])


class _DecodeStatic(NamedTuple):
    seq_base: int
    seq_per_step: int
    n_alias: int


class _Decode:
    def __init__(self, layer, seq_base, seq_per_step, n_sub, sinks, xq, xt, caches, prev_outs):
        self.layer, self.seq_base, self.seq_per_step, self.n_sub = layer, seq_base, seq_per_step, n_sub
        self.sinks, self.xq, self.xt = sinks, xq, xt
        self.caches, self.prev_outs = tuple(caches), tuple(prev_outs)
        self.static = _DecodeStatic(seq_base, seq_per_step, len(self.prev_outs))

    def extend(self, in_specs, out_specs, out_shape, args):
        layer, block0, n_sub = self.layer, self.seq_base // self.seq_per_step, self.n_sub
        linear = (lambda i: i) if n_sub == 1 else (lambda i, j: i * n_sub + j)
        spec = lambda c: pl.BlockSpec((None, self.seq_per_step) + c.shape[2:],
                                      lambda *ids: (layer, block0 + linear(*ids), 0, 0))
        first_alias_in = 1 + len(args) + 2 + len(self.caches)
        aliases = {first_alias_in + i: len(out_specs) + i for i in range(len(self.prev_outs))}
        in_specs = ([pl.BlockSpec(memory_space=pltpu.SMEM)] + in_specs
                    + [_const_spec(self.xq.shape), _const_spec(self.xt.shape)]
                    + [spec(c) for c in self.caches]
                    + [pl.BlockSpec(memory_space=pl.ANY) for _ in self.prev_outs])
        args = [self.sinks] + args + [self.xq, self.xt, *self.caches, *self.prev_outs]
        yt_shape = (2 * MIX, self.xq.shape[1])
        out_specs = out_specs + [spec(c) for c in self.caches] + [_const_spec(yt_shape)]
        out_shape = (out_shape + [jax.ShapeDtypeStruct(c.shape, F32) for c in self.caches]
                     + [jax.ShapeDtypeStruct(yt_shape, F32)])
        rows = sum(c.shape[3] for c in self.caches)
        extra = 4 * self.seq_per_step * KV_CH * rows * 4 + 4 * (2 * MIX + XT_ROWS) * BLOCK * 4
        return in_specs, out_specs, out_shape, args, aliases, extra


def _rope_tables(pos):
    inv_freq = ROPE_THETA ** (-jnp.arange(HALF_HEAD, dtype=F32) / HALF_HEAD)
    ang = pos.astype(F32)[:, None] * inv_freq[None, :]
    cos, sin = jnp.cos(ang), jnp.sin(ang)
    return (jnp.concatenate([cos, cos, cos, cos], axis=1),
            jnp.concatenate([-sin, sin, -sin, sin], axis=1))


def _cache_view(c):
    d, b, n = c.shape[:3]
    return jnp.transpose(c, (0, 1, 3, 4, 5, 2)).reshape(d, b, KV_CH, n)


def _cache_unview(c):
    lead, n = c.shape[:-2], c.shape[-1]
    nl = len(lead)
    perm = tuple(range(nl)) + (nl + 3, nl, nl + 1, nl + 2)
    return jnp.transpose(c.reshape(lead + (2, 2, HEAD_DIM, n)), perm)


def kernel(x_prompt, x_sample, state_conv, cache_swa_kv, cache_dil1_kv, cache_dil4_kv, cache_dil16_kv,
           norm_g, w_in, conv_w, attn_sinks, v_ln_g, v_ln_b, w_spatial, b_spatial,
           w_branch, w_merge, w_out, final_norm_g):
    depth = w_in.shape[0]
    nb, seq, _ = x_prompt.shape
    ns, dec_seq, _ = x_sample.shape
    assert dec_seq == 1 and seq % SUPER == 0
    past_len = 16384
    assert cache_swa_kv.shape[2] == BLOCK and cache_dil16_kv.shape[2] == SUPER

    cos_p, sin_p = _rope_tables(jnp.arange(seq, dtype=jnp.int32))
    cos_s, sin_s = _rope_tables(past_len + jnp.arange(1, dtype=jnp.int32))
    assert depth == 2
    casts = ((w_in.reshape(depth * D_MODEL, IN_COLS), D_MODEL, D_MODEL),
             (w_merge.reshape(depth * D_MODEL, 4 * D_MODEL), depth * D_MODEL, 0),
             (w_branch.reshape(depth * 4 * MIX, D_MODEL), depth * 4 * MIX, 0),
             (w_out.reshape(depth * D_MODEL, D_MODEL), depth * D_MODEL, 0))
    w_in_l = w_in
    caches = [_cache_view(c) for c in (cache_swa_kv, cache_dil1_kv, cache_dil4_kv, cache_dil16_kv)]
    fg = final_norm_g.reshape(1, D_MODEL)

    xp = x_prompt.reshape(nb * seq, D_MODEL)
    xs = x_sample.reshape(ns, D_MODEL)
    conv_s, chunk_v = [], []
    state_p = ()
    dil_p = ()
    new_s = ()
    for l in range(depth):
        g = norm_g[l].reshape(1, D_MODEL)
        final = l == depth - 1
        ln_g = v_ln_g[l].reshape(1, MIX)
        ln_b = v_ln_b[l].reshape(1, MIX)

        w_diag = jnp.repeat(w_spatial[l][:, 0, 0], HEAD_DIM).reshape(1, MIX)
        b_diag = jnp.repeat(b_spatial[l][:, 0], HEAD_DIM).reshape(1, MIX)
        za, zc, nstate, vn, xq, xt, gate_b, gate_d, *w_new = _inproj_sample(
            xs, g, w_in_l, 0, state_conv[l].reshape(ns, 2 * MIX), conv_w[l], ln_g, ln_b, w_diag, b_diag,
            cos_s, sin_s)
        if w_new:
            w_in_l, = w_new

        tm = 512
        tiles = nb * seq // tm
        half = tiles * N_SUB
        assert ns - half == 2 * tiles
        b_tile = jnp.repeat(b_spatial[l].T, HEAD_DIM, axis=1)
        dec = _Decode(l, 0, 1, N_SUB, attn_sinks[l], xq, xt, caches, new_s)
        ya, yb, yc, qd, kd, vd, gd, conv_p, swa_p, *rest = _inproj_mix(
            xp, g, w_in_l, 0, l, depth, cos_p, sin_p, conv_w[l], ln_g, ln_b, w_spatial[l], b_tile, tm, seq,
            dec, state_p, casts)
        state_p = (conv_p, swa_p)
        if casts:
            w_next, w_merge_bf, w_branch_bf, w_out_bf = rest[:4]
            w_in_next = w_next.reshape(1, D_MODEL, IN_COLS)
            w_merge_bf = w_merge_bf.reshape(depth, D_MODEL, 4 * D_MODEL)
            w_branch_bf = w_branch_bf.reshape(depth, 4, MIX, D_MODEL)
            w_out_bf = w_out_bf.reshape(depth, D_MODEL, D_MODEL)
            rest = rest[4:]
            casts = ()
        ns0, ns1, ns4, ns16, yt0 = rest
        per_seq = lambda a: a.reshape(D_GROUPS, nb, seq, D_LANES)
        yd, *dil_p = _mix_d(per_seq(qd), per_seq(kd), per_seq(vd), gd.reshape(nb, seq, MIX), l, depth, dil_p)
        dec = _Decode(l, half, 2, 1, attn_sinks[l], xq, xt, caches, (ns0, ns1, ns4, ns16))
        xp, *new_s, yt1 = _outproj(xp, (ya, yb, yc, yd.reshape(nb * seq, MIX)), g, w_merge_bf,
                                   w_branch_bf, w_out_bf, fg, l, tm, final, dec)

        xs = _outproj_sample(xs, za, zc, yt0, yt1, gate_b, gate_d, g, w_merge_bf, w_branch_bf, w_out_bf, fg,
                             l, final, half)
        conv_s.append(nstate.reshape(ns, 2, MIX))
        chunk_v.append(vn.reshape(ns, 1, MIX))
        w_in_l = w_in_next

    st = jnp.stack
    conv_p, swa_p = state_p
    new_p = [_cache_unview(c) for c in (swa_p, *dil_p)]
    new_s = [_cache_unview(c) for c in new_s]
    return (xp.reshape(nb, seq, D_MODEL), xs.reshape(ns, 1, D_MODEL),
            conv_p, st(conv_s),
            new_p[0], new_s[0], new_p[1], new_s[1],
            new_p[2], new_s[2], new_p[3], new_s[3],
            st(chunk_v))
```

```python
import functools
from typing import NamedTuple

import jax
import jax.numpy as jnp
from jax import lax
from jax.experimental import pallas as pl
from jax.experimental.pallas import tpu as pltpu

F32 = jnp.float32
BF16 = jnp.bfloat16

D_MODEL = 1024
HEAD_DIM = 64
HALF_HEAD = HEAD_DIM // 2
MIX = 384
BLOCK = 128
SWA_Q_HEADS = 6
SWA_KV_HEADS = 2
SWA_REP = SWA_Q_HEADS // SWA_KV_HEADS
DILATIONS = (1, 4, 16)
SUPER = BLOCK * DILATIONS[-1]
ROW_CHUNK = 256
ROPE_THETA = 10000.0
PAST_LEN = 16384
EPS = 1e-6
NEG_INF = -1e30
SCALE = HEAD_DIM ** -0.5
LOG2E = 1.4426950408889634
SCALE_LOG2 = SCALE * LOG2E

COLS_A = 4 * MIX
COLS_B = MIX + 2 * BLOCK + MIX
COLS_C = 3 * MIX
COLS_D = 4 * MIX
OFF_A, OFF_B, OFF_C, OFF_D = 0, COLS_A, COLS_A + COLS_B, COLS_A + COLS_B + COLS_C
IN_COLS = OFF_D + COLS_D
KV_CH = 4 * HEAD_DIM

MIB = 1024 * 1024
V7X_VMEM_BYTES = 64 * MIB
VMEM_KEPT_FREE = 8 * MIB
VMEM_TEMPORARIES = 16 * MIB
MATMUL_COL_CHUNK = 512


def _vmem_limit(estimate_bytes):
    return int(min(V7X_VMEM_BYTES - VMEM_KEPT_FREE, estimate_bytes + VMEM_TEMPORARIES))


def _const_spec(shape):
    nd = len(shape)
    return pl.BlockSpec(shape, lambda *_: (0,) * nd)


def _layer_spec(stacked_shape, layer):
    nd = len(stacked_shape) - 1
    return pl.BlockSpec((None,) + tuple(stacked_shape[1:]), lambda *_: (layer,) + (0,) * nd,
                        pipeline_mode=pl.Buffered(1))


def _rms(x, g):
    ms = jnp.mean(x * x, axis=-1, keepdims=True)
    return x * lax.rsqrt(ms + EPS) * g


def _sigmoid(x):
    return 0.5 * jnp.tanh(0.5 * x) + 0.5


def _silu(x):
    h = 0.5 * x
    return h * jnp.tanh(h) + h


def _rope(x, cos, sin):
    w = x.shape[-1]
    lane = lax.broadcasted_iota(jnp.int32, x.shape, 1)
    fwd = pltpu.roll(x, w - HALF_HEAD, 1)
    bwd = pltpu.roll(x, HALF_HEAD, 1)
    partner = jnp.where((lane & HALF_HEAD) == 0, fwd, bwd)
    return x * cos + partner * sin


def _tile_lanes(t, reps):
    return t if reps == 1 else jnp.concatenate([t] * reps, axis=1)


def _dot(a, b):
    return jnp.dot(a, b, preferred_element_type=F32)


def _dot_nt(a, b):
    return lax.dot_general(a, b, (((1,), (1,)), ((), ())), preferred_element_type=F32)


def _project(h, w_ref, off, width, store):
    for j in range(0, width, MATMUL_COL_CHUNK):
        cw = min(MATMUL_COL_CHUNK, width - j)
        store(j, cw, _dot(h, w_ref[:, off + j:off + j + cw]))


def _outproj_body(*refs, final, dec):
    sink_ref = refs[0]
    (x_ref, ya_ref, yb_ref, yc_ref, yd_ref, g_ref, wm_ref, wb_ref, wo_ref, fg_ref) = refs[1:11]
    xq_ref, xt_ref = refs[11:13]
    o_ref = refs[17 + dec.n_alias]
    step = pl.program_id(0)
    dec_outs = refs[18 + dec.n_alias:]

    @pl.when(step == 0)
    def _():
        dec_outs[4][...] = jnp.zeros_like(dec_outs[4])

    _decode_sequences(dec, step, sink_ref, xq_ref, xt_ref, refs[13:17], dec_outs)
    ys = (ya_ref[...], yb_ref[...], yc_ref[...], yd_ref[...])
    o_ref[...] = _merge_project(x_ref[...], ys, g_ref, wm_ref, wb_ref, wo_ref, fg_ref, final)


def _merge_project(x, ys, g_ref, wm_ref, wb_ref, wo_ref, fg_ref, final):
    h = _rms(x, g_ref[...]).astype(BF16)
    merged = None
    for n, y in enumerate(ys):
        gate = _sigmoid(_dot(h, wm_ref[:, n * D_MODEL:(n + 1) * D_MODEL]))
        term = gate * _dot(y.astype(BF16), wb_ref[n])
        merged = term if merged is None else merged + term
    out = x + _dot(merged.astype(BF16), wo_ref[...])
    return _rms(out, fg_ref[...]) if final else out


def _outproj_sample_body(x_ref, ya_ref, yc_ref, yt0_ref, yt1_ref, gb_ref, gd_ref,
                         g_ref, wm_ref, wb_ref, wo_ref, fg_ref, o_ref, *, final, split):
    lane = lax.broadcasted_iota(jnp.int32, yt0_ref.shape, 1)
    yt = jnp.where(lane < split, yt0_ref[...], yt1_ref[...])
    yb = yt[0:MIX, :].T * _silu(gb_ref[...])
    yd = yt[MIX:2 * MIX, :].T * _silu(gd_ref[...])
    ys = (ya_ref[...], yb, yc_ref[...], yd)
    o_ref[...] = _merge_project(x_ref[...], ys, g_ref, wm_ref, wb_ref, wo_ref, fg_ref, final)


def _outproj_sample(x, ya, yc, yt0, yt1, gb, gd, g, wm, wb, wo, fg, layer, final, split):
    ns = x.shape[0]
    est = (4 * D_MODEL * D_MODEL + 4 * MIX * D_MODEL + D_MODEL * D_MODEL) * 2 + 24 * ns * D_MODEL * 4
    whole = lambda a: _const_spec(a.shape)
    return pl.pallas_call(
        functools.partial(_outproj_sample_body, final=final, split=split),
        grid=(1,),
        in_specs=[whole(a) for a in (x, ya, yc, yt0, yt1, gb, gd, g)]
                 + [_layer_spec(wm.shape, layer), _layer_spec(wb.shape, layer), _layer_spec(wo.shape, layer),
                    whole(fg)],
        out_specs=_const_spec((ns, D_MODEL)),
        out_shape=jax.ShapeDtypeStruct((ns, D_MODEL), F32),
        compiler_params=pltpu.CompilerParams(dimension_semantics=("arbitrary",),
                                             vmem_limit_bytes=_vmem_limit(est)),
        name="outproj_sample",
    )(x, ya, yc, yt0, yt1, gb, gd, g, wm, wb, wo, fg)


def _outproj(x, ys, g, wm, wb, wo, fg, layer, tm, final, dec):
    t = x.shape[0]
    ysize = jnp.dtype(ys[0].dtype).itemsize
    est = (4 * tm * D_MODEL * 4 + 2 * 4 * tm * MIX * ysize
           + (4 * D_MODEL * D_MODEL + 4 * MIX * D_MODEL + D_MODEL * D_MODEL) * 2
           + 4 * tm * D_MODEL * 4)
    row = lambda i: (i, 0)
    in_specs = ([pl.BlockSpec((tm, D_MODEL), row)]
                + [pl.BlockSpec((tm, MIX), row)] * 4
                + [_const_spec((1, D_MODEL)),
                   _layer_spec(wm.shape, layer), _layer_spec(wb.shape, layer), _layer_spec(wo.shape, layer),
                   _const_spec((1, D_MODEL))])
    out_specs = [pl.BlockSpec((tm, D_MODEL), row)]
    out_shape = [jax.ShapeDtypeStruct((t, D_MODEL), F32)]
    args = [x, *ys, g, wm, wb, wo, fg]
    in_specs, out_specs, out_shape, args, aliases, extra = dec.extend(in_specs, out_specs, out_shape, args)
    return pl.pallas_call(
        functools.partial(_outproj_body, final=final, dec=dec.static),
        grid=(t // tm,),
        in_specs=in_specs, out_specs=out_specs, out_shape=out_shape,
        input_output_aliases=aliases,
        compiler_params=pltpu.CompilerParams(dimension_semantics=("arbitrary",),
                                             vmem_limit_bytes=_vmem_limit(est + extra)),
        name="outproj_dec",
    )(*args)


def _mix_a(pa_ref, cw_ref, ya_ref, prev2, prev1):
    row = lax.broadcasted_iota(jnp.int32, (BLOCK, MIX), 0)
    for ck in range(pa_ref.shape[0] // BLOCK):
        rows = slice(ck * BLOCK, (ck + 1) * BLOCK)
        a_b = pa_ref[rows, 0:MIX]
        z = pa_ref[rows, MIX:2 * MIX] * pa_ref[rows, 2 * MIX:3 * MIX]
        a_gate = pa_ref[rows, 3 * MIX:4 * MIX]
        z1 = jnp.where(row == 0, prev1, pltpu.roll(z, 1, 0))
        z2 = jnp.where(row == 0, prev2, jnp.where(row == 1, prev1, pltpu.roll(z, 2, 0)))
        conv = cw_ref[0:1, :] * z2 + cw_ref[1:2, :] * z1 + cw_ref[2:3, :] * z
        ya_ref[rows, :] = (a_b * conv * _silu(a_gate)).astype(ya_ref.dtype)
        prev2 = z[BLOCK - 2:BLOCK - 1, :]
        prev1 = z[BLOCK - 1:BLOCK, :]
    return prev2, prev1


def _mix_c(pc_ref, col0, lg_ref, lb_ref, ws_ref, bs_ref, yc_ref):
    r = lax.broadcasted_iota(jnp.int32, (BLOCK, BLOCK), 0)
    c = lax.broadcasted_iota(jnp.int32, (BLOCK, BLOCK), 1)
    n_groups = MIX // HEAD_DIM
    w_causal = [jnp.where(r >= c, ws_ref[g], 0.0).astype(BF16) for g in range(n_groups)]
    for ck in range(pc_ref.shape[0] // BLOCK):
        rows = slice(ck * BLOCK, (ck + 1) * BLOCK)
        c_v = pc_ref[rows, col0 + MIX:col0 + 2 * MIX]
        mu = jnp.mean(c_v, axis=-1, keepdims=True)
        cen = c_v - mu
        var = jnp.mean(cen * cen, axis=-1, keepdims=True)
        vn_c = (cen * lax.rsqrt(var + EPS) * lg_ref[...] + lb_ref[...]).astype(BF16)
        mixed = jnp.concatenate(
            [_dot(w_causal[g], vn_c[:, g * HEAD_DIM:(g + 1) * HEAD_DIM]) for g in range(n_groups)],
            axis=1) + bs_ref[...]
        c_u = pc_ref[rows, col0:col0 + MIX]
        c_gate = pc_ref[rows, col0 + 2 * MIX:col0 + 3 * MIX]
        yc_ref[rows, :] = (c_u * mixed * _silu(c_gate)).astype(yc_ref.dtype)


def _band_mask(first_block):
    i = lax.broadcasted_iota(jnp.int32, (BLOCK, 2 * BLOCK), 0)
    j = lax.broadcasted_iota(jnp.int32, (BLOCK, 2 * BLOCK), 1)
    band = (j >= i) & (j <= i + BLOCK)
    return band & jnp.logical_or(jnp.logical_not(first_block), j >= BLOCK)


def _mix_b_prepare(pb_ref, cos_ref, sin_ref, kv_ref, q_s, hist):
    k_s, v_s, ksw_s, vsw_s = hist
    ts = pb_ref.shape[0]
    for r0 in range(0, ts, ROW_CHUNK):
        rows = slice(r0, r0 + ROW_CHUNK)
        dst = slice(BLOCK + r0, BLOCK + r0 + ROW_CHUNK)
        cos = cos_ref[rows, :]
        sin = sin_ref[rows, :]
        q_s[rows, :] = (_rope(pb_ref[rows, 0:MIX], _tile_lanes(cos, 3), _tile_lanes(sin, 3)) * SCALE_LOG2
                        ).astype(BF16)
        k = _rope(pb_ref[rows, MIX:MIX + BLOCK], cos, sin)
        v = pb_ref[rows, MIX + BLOCK:MIX + 2 * BLOCK]
        k_s[dst, :] = k.astype(BF16)
        v_s[dst, :] = v.astype(BF16)
        ksw_s[dst, :] = pltpu.roll(k, HEAD_DIM, 1).astype(BF16)
        vsw_s[dst, :] = pltpu.roll(v, HEAD_DIM, 1).astype(BF16)
        if r0 + ROW_CHUNK == ts:
            kv_ref[0:BLOCK, :] = k[ROW_CHUNK - BLOCK:ROW_CHUNK, :].T
            kv_ref[BLOCK:2 * BLOCK, :] = v[ROW_CHUNK - BLOCK:ROW_CHUNK, :].T


def _mix_b_attend(sink_ref, pb_ref, first_tile, yb_ref, q_s, hist):
    k_s, v_s, ksw_s, vsw_s = hist
    ts = pb_ref.shape[0]
    half0 = lax.broadcasted_iota(jnp.int32, (BLOCK, BLOCK), 1) < HEAD_DIM
    bias_rest = jnp.where(_band_mask(False), 0.0, NEG_INF)
    bias_first = jnp.where(_band_mask(first_tile), 0.0, NEG_INF)
    for c in range(ts // BLOCK):
        rows = slice(c * BLOCK, (c + 1) * BLOCK)
        win = slice(c * BLOCK, (c + 2) * BLOCK)
        kk, vv, kk_sw, vv_sw = k_s[win, :], v_s[win, :], ksw_s[win, :], vsw_s[win, :]
        bias = bias_first if c == 0 else bias_rest
        outs, dens = [], []
        for h in range(SWA_Q_HEADS):
            g, chunk, half = h // SWA_REP, h // 2, h % 2
            qc = q_s[rows, chunk * BLOCK:(chunk + 1) * BLOCK]
            q_h = jnp.where(half0 if half == 0 else jnp.logical_not(half0), qc, jnp.zeros_like(qc))
            s = _dot_nt(q_h, kk if half == g else kk_sw) + bias
            sink = sink_ref[h] * LOG2E
            m = jnp.maximum(jnp.max(s, axis=-1, keepdims=True), sink)
            p = jnp.exp2(s - m)
            dens.append(jnp.sum(p, axis=-1, keepdims=True) + jnp.exp2(sink - m))
            outs.append(_dot(p.astype(BF16), vv if half == g else vv_sw))
        y = jnp.concatenate([jnp.where(half0, outs[2 * j], outs[2 * j + 1])
                             / jnp.where(half0, dens[2 * j], dens[2 * j + 1]) for j in range(3)], axis=1)
        gate = pb_ref[rows, MIX + 2 * BLOCK:COLS_B]
        yb_ref[rows, :] = (y * _silu(gate)).astype(yb_ref.dtype)
    for r in hist:
        r[0:BLOCK, :] = r[ts:ts + BLOCK, :]


N_SUB = 2
COLS_BC = COLS_B + COLS_C
N_MIX_OUT = 9
D_GROUPS = len(DILATIONS)
D_LANES = 2 * HEAD_DIM


def _mix_d_prepare(pd_ref, cos_ref, sin_ref, qd_ref, kd_ref, vd_ref, gd_ref):
    for r0 in range(0, pd_ref.shape[0], ROW_CHUNK):
        rows = slice(r0, r0 + ROW_CHUNK)
        cos = cos_ref[rows, :]
        sin = sin_ref[rows, :]
        for g in range(D_GROUPS):
            lanes = slice(g * D_LANES, (g + 1) * D_LANES)
            qd_ref[g, rows, :] = _rope(pd_ref[rows, lanes], cos, sin) * SCALE_LOG2
            kd_ref[g, rows, :] = _rope(pd_ref[rows, MIX + g * D_LANES:MIX + (g + 1) * D_LANES], cos, sin)
            vd_ref[g, rows, :] = pd_ref[rows, 2 * MIX + g * D_LANES:2 * MIX + (g + 1) * D_LANES]
        gd_ref[rows, :] = pd_ref[rows, 3 * MIX:4 * MIX].astype(gd_ref.dtype)


def _inproj_mix_body(*refs, dec, tiles_per_seq, n_prev, n_cast):
    (sink_ref, x_ref, g_ref, w_ref, cos_ref, sin_ref, cw_ref, lg_ref, lb_ref, ws_ref, bs_ref) = refs[:11]
    refs = refs[11 + n_prev:]
    cast_in = refs[:n_cast]
    xq_ref, xt_ref = refs[n_cast:n_cast + 2]
    caches = refs[n_cast + 2:n_cast + 6]
    rest = refs[n_cast + 6 + dec.n_alias:]
    ya_ref, yb_ref, yc_ref, qd_ref, kd_ref, vd_ref, gd_ref, nc_ref, kv_ref = rest[:N_MIX_OUT]
    cast_out = rest[N_MIX_OUT:N_MIX_OUT + n_cast]
    rest = rest[N_MIX_OUT + n_cast:]
    dec_outs = rest[0:5]
    h_s, pa_s, pbc_s, pd_s, tail_s, q_s = rest[5:11]
    hist = rest[11:15]
    tile = pl.program_id(0)
    sub = pl.program_id(1)
    step = tile * N_SUB + sub

    def cast_slabs():
        for src, dst in zip(cast_in, cast_out):
            dst[...] = src[...].astype(BF16)

    @pl.when(step == 0)
    def _():
        dec_outs[4][...] = jnp.zeros_like(dec_outs[4])
        tail_s[...] = jnp.zeros_like(tail_s)
        for r in hist:
            r[0:BLOCK, :] = jnp.zeros((BLOCK, BLOCK), BF16)

    first_tile = (tile % tiles_per_seq) == 0

    @pl.when(sub == 0)
    def _():
        h = _rms(x_ref[...], g_ref[...]).astype(BF16)
        h_s[...] = h

        def store(j, cw, acc, base):
            pbc_s[:, base + j:base + j + cw] = acc

        def store_d(j, cw, acc):
            pd_s[:, j:j + cw] = acc

        _project(h, w_ref, OFF_C, COLS_C, functools.partial(store, base=COLS_B))
        _project(h, w_ref, OFF_B, COLS_B, functools.partial(store, base=0))
        _project(h, w_ref, OFF_D, COLS_D, store_d)
        _mix_c(pbc_s, COLS_B, lg_ref, lb_ref, ws_ref, bs_ref, yc_ref)
        _mix_b_prepare(pbc_s, cos_ref, sin_ref, kv_ref, q_s, hist)
        _mix_d_prepare(pd_s, cos_ref, sin_ref, qd_ref, kd_ref, vd_ref, gd_ref)
        _decode_sequences(dec, step, sink_ref, xq_ref, xt_ref, caches, dec_outs)
        cast_slabs()

    @pl.when(sub == 1)
    def _():
        def store_a(j, cw, acc):
            pa_s[:, j:j + cw] = acc

        _project(h_s[...], w_ref, OFF_A, COLS_A, store_a)

        prev2 = jnp.where(first_tile, 0.0, tail_s[0:1, :])
        prev1 = jnp.where(first_tile, 0.0, tail_s[1:2, :])
        prev2, prev1 = _mix_a(pa_s, cw_ref, ya_ref, prev2, prev1)
        last2 = jnp.concatenate([prev2, prev1], axis=0)
        tail_s[...] = last2
        nc_ref[...] = last2
        _mix_b_attend(sink_ref, pbc_s, first_tile, yb_ref, q_s, hist)
        _decode_sequences(dec, step, sink_ref, xq_ref, xt_ref, caches, dec_outs)
        cast_slabs()


def _inproj_mix(x, g, w_bf16, w_layer, layer, depth, cos, sin, conv_w, ln_g, ln_b, w_s, b_tile, tm, seq, dec,
                prev_state, casts):
    t = x.shape[0]
    nb = t // seq
    tiles_per_seq = seq // tm
    steps = (t // tm) * N_SUB
    assert dec.n_sub == N_SUB
    row = lambda i, j: (i, 0)
    pos = lambda i, j: (i % tiles_per_seq, 0)
    per_seq = lambda i, j: (layer, i // tiles_per_seq, 0, 0)
    in_specs = ([pl.BlockSpec((tm, D_MODEL), row), _const_spec((1, D_MODEL)), _layer_spec(w_bf16.shape, w_layer),
                 pl.BlockSpec((tm, BLOCK), pos), pl.BlockSpec((tm, BLOCK), pos),
                 _const_spec((3, MIX)), _const_spec((1, MIX)), _const_spec((1, MIX)),
                 _const_spec((MIX // HEAD_DIM, BLOCK, BLOCK)), _const_spec((BLOCK, MIX))]
                + [pl.BlockSpec(memory_space=pl.ANY) for _ in prev_state])
    grouped = pl.BlockSpec((D_GROUPS, tm, D_LANES), lambda i, j: (0, i, 0))
    out_specs = ([pl.BlockSpec((tm, MIX), row)] * 3 + [grouped] * 3 + [pl.BlockSpec((tm, MIX), row),
                 pl.BlockSpec((None, None, 2, MIX), per_seq), pl.BlockSpec((None, None, KV_CH, BLOCK), per_seq)])
    out_shape = ([jax.ShapeDtypeStruct((t, MIX), BF16)] * 3
                 + [jax.ShapeDtypeStruct((D_GROUPS, t, D_LANES), F32)] * 3
                 + [jax.ShapeDtypeStruct((t, MIX), BF16),
                    jax.ShapeDtypeStruct((depth, nb, 2, MIX), F32),
                    jax.ShapeDtypeStruct((depth, nb, KV_CH, BLOCK), F32)])
    assert len(out_specs) == N_MIX_OUT
    args = [x, g, w_bf16, cos, sin, conv_w, ln_g, ln_b, w_s, b_tile, *prev_state]
    state_aliases = {1 + 10 + k: N_MIX_OUT - 2 + k for k in range(len(prev_state))}
    cast_bytes = 0
    for arr, n_rows, first_row in casts:
        slab = n_rows // steps
        assert slab * steps == n_rows and slab % 16 == 0 and first_row % slab == 0
        first = first_row // slab
        in_specs.append(pl.BlockSpec((slab, arr.shape[1]), lambda i, j, first=first: (first + i * N_SUB + j, 0)))
        out_specs.append(pl.BlockSpec((slab, arr.shape[1]), lambda i, j: (i * N_SUB + j, 0)))
        out_shape.append(jax.ShapeDtypeStruct((n_rows, arr.shape[1]), BF16))
        args.append(arr)
        cast_bytes += 2 * slab * arr.shape[1] * 6
    scratch = [pltpu.VMEM((tm, D_MODEL), BF16), pltpu.VMEM((tm, COLS_A), F32), pltpu.VMEM((tm, COLS_BC), F32),
               pltpu.VMEM((tm, COLS_D), F32), pltpu.VMEM((2, MIX), F32), pltpu.VMEM((tm, MIX), BF16)
               ] + [pltpu.VMEM((BLOCK + tm, BLOCK), BF16)] * 4
    est = (2 * tm * D_MODEL * 4 + D_MODEL * IN_COLS * 2 + 2 * tm * (4 * MIX * 2 + 3 * MIX * 4)
           + tm * (D_MODEL * 2 + IN_COLS * 4 + MIX * 2) + 4 * (BLOCK + tm) * BLOCK * 2
           + 4 * tm * BLOCK * 4)
    in_specs, out_specs, out_shape, args, aliases, extra = dec.extend(in_specs, out_specs, out_shape, args)
    return pl.pallas_call(
        functools.partial(_inproj_mix_body, dec=dec.static, tiles_per_seq=tiles_per_seq,
                          n_prev=len(prev_state), n_cast=len(casts)),
        grid=(t // tm, N_SUB),
        in_specs=in_specs, out_specs=out_specs, out_shape=out_shape,
        scratch_shapes=scratch,
        input_output_aliases={**aliases, **state_aliases},
        compiler_params=pltpu.CompilerParams(dimension_semantics=("arbitrary", "arbitrary"),
                                             vmem_limit_bytes=_vmem_limit(est + extra + cast_bytes)),
        name="inproj_mix",
    )(*args)


def _mix_d_body(q_ref, kc_ref, kp_ref, vc_ref, vp_ref, gd_ref, yd_ref, c1_ref, c4_ref, c16_ref,
                o_ref, m_ref, d_ref):
    sb = pl.program_id(1)
    lw = D_LANES
    n_groups = D_GROUPS

    def window(cur, prev, g, base, dil, j):
        if j > 0:
            return cur[g, pl.ds(base - BLOCK * dil, 2 * BLOCK, stride=dil), :]
        return jnp.concatenate([prev[g, pl.ds(SUPER - BLOCK * dil + base, BLOCK, stride=dil), :],
                                cur[g, pl.ds(base, BLOCK, stride=dil), :]], axis=0)

    n_blocks = SUPER // BLOCK
    bias_rest = jnp.where(_band_mask(False), 0.0, NEG_INF)
    bias_first = jnp.where(_band_mask(sb == 0), 0.0, NEG_INF)
    for g, dil in enumerate(DILATIONS):
        for t in range(n_blocks):
            r = t % dil
            j = t // dil
            base = j * (BLOCK * dil) + r
            qs = q_ref[g, pl.ds(base, BLOCK, stride=dil), :]
            bias = bias_first if j == 0 else bias_rest
            ks_bf = window(kc_ref, kp_ref, g, base, dil, j).astype(BF16)
            vs_bf = window(vc_ref, vp_ref, g, base, dil, j).astype(BF16)
            head0 = lax.broadcasted_iota(jnp.int32, (BLOCK, lw), 1) < HEAD_DIM
            o_h, m_h, d_h = [], [], []
            for hh in range(2):
                q_h = jnp.where(head0 if hh == 0 else jnp.logical_not(head0), qs, 0.0).astype(BF16)
                s = _dot_nt(q_h, ks_bf) + bias
                m = jnp.max(s, axis=-1, keepdims=True)
                p = jnp.exp2(s - m)
                o_h.append(_dot(p.astype(BF16), vs_bf))
                m_h.append(m)
                d_h.append(jnp.sum(p, axis=-1, keepdims=True))
            rows = pl.ds(base, BLOCK, stride=dil)
            o_ref[g, rows, :] = jnp.where(head0, o_h[0], o_h[1])
            m_ref[g, rows, :] = jnp.where(head0, m_h[0], m_h[1])
            d_ref[g, rows, :] = jnp.where(head0, d_h[0], d_h[1])

    def merge(c, carry):
        rows = pl.ds(pl.multiple_of(c * ROW_CHUNK, ROW_CHUNK), ROW_CHUNK)
        ms = [m_ref[g, rows, :] for g in range(n_groups)]
        mmax = jnp.maximum(jnp.maximum(ms[0], ms[1]), ms[2])
        es = [jnp.exp2(m - mmax) for m in ms]
        inv = 1.0 / (d_ref[0, rows, :] * es[0] + d_ref[1, rows, :] * es[1] + d_ref[2, rows, :] * es[2])
        for g in range(n_groups):
            gate = gd_ref[rows, g * lw:(g + 1) * lw].astype(F32)
            y = o_ref[g, rows, :] * (es[g] * inv)
            yd_ref[rows, g * lw:(g + 1) * lw] = (y * _silu(gate)).astype(yd_ref.dtype)
        return carry

    lax.fori_loop(0, SUPER // ROW_CHUNK, merge, 0)

    @pl.when(sb == pl.num_programs(1) - 1)
    def _():
        for g, (c_ref, dil) in enumerate(zip((c1_ref, c4_ref, c16_ref), DILATIONS)):
            n = BLOCK * dil
            for j in range(0, n, ROW_CHUNK if n >= ROW_CHUNK else n):
                w = min(ROW_CHUNK, n)
                src = slice(SUPER - n + j, SUPER - n + j + w)
                c_ref[0:lw, j:j + w] = kc_ref[g, src, :].T
                c_ref[lw:2 * lw, j:j + w] = vc_ref[g, src, :].T


def _mix_d_entry(q_ref, kc_ref, kp_ref, vc_ref, vp_ref, gd_ref, *rest, n_prev):
    _mix_d_body(q_ref, kc_ref, kp_ref, vc_ref, vp_ref, gd_ref, *rest[n_prev:])


def _mix_d(qd, kd, vd, gd, layer, depth, prev_caches):
    _, b, s, _ = qd.shape
    slab = D_GROUPS * SUPER * D_LANES * 4
    est = (2 * 5 * slab + 3 * slab + 2 * 2 * SUPER * MIX * 2
           + 2 * KV_CH * (BLOCK + 4 * BLOCK + SUPER) * 4)
    cur = pl.BlockSpec((D_GROUPS, None, SUPER, D_LANES), lambda bi, si: (0, bi, si, 0))
    prev = pl.BlockSpec((D_GROUPS, None, SUPER, D_LANES), lambda bi, si: (0, bi, jnp.maximum(si - 1, 0), 0))
    tile = lambda bi, si: (bi, si, 0)
    cache = lambda bi, si: (layer, bi, 0, 0)
    return pl.pallas_call(
        functools.partial(_mix_d_entry, n_prev=len(prev_caches)),
        grid=(b, s // SUPER),
        in_specs=[cur, cur, prev, cur, prev, pl.BlockSpec((None, SUPER, MIX), tile)]
                 + [pl.BlockSpec(memory_space=pl.ANY) for _ in prev_caches],
        out_specs=[pl.BlockSpec((None, SUPER, MIX), tile)]
                  + [pl.BlockSpec((None, None, KV_CH, BLOCK * d), cache) for d in DILATIONS],
        out_shape=[jax.ShapeDtypeStruct((b, s, MIX), BF16)]
                  + [jax.ShapeDtypeStruct((depth, b, KV_CH, BLOCK * d), F32) for d in DILATIONS],
        input_output_aliases={6 + k: 1 + k for k in range(len(prev_caches))},
        scratch_shapes=[pltpu.VMEM((D_GROUPS, SUPER, D_LANES), F32)] * 3,
        compiler_params=pltpu.CompilerParams(dimension_semantics=("arbitrary", "arbitrary"),
                                             vmem_limit_bytes=_vmem_limit(est)),
        name="mix_d",
    )(qd, kd, kd, vd, vd, gd, *prev_caches)


def _shift_pass(c_ref, o_ref, i, ch0, new_b, per_tile):
    n_tiles = c_ref.shape[2] // BLOCK
    chans = pl.ds(ch0, 2 * HEAD_DIM)
    lane = lax.broadcasted_iota(jnp.int32, (2 * HEAD_DIM, BLOCK), 1)
    prev = None
    for t in range(n_tiles):
        x = c_ref[i, chans, t * BLOCK:(t + 1) * BLOCK]
        per_tile(t, x)
        r = pltpu.roll(x, BLOCK - 1, 1)
        if t > 0:
            o_ref[i, chans, (t - 1) * BLOCK:t * BLOCK] = jnp.where(lane < BLOCK - 1, prev, r)
        prev = r
    o_ref[i, chans, (n_tiles - 1) * BLOCK:n_tiles * BLOCK] = jnp.where(lane < BLOCK - 1, prev, new_b)


def _decode_cache(c_ref, o_ref, i, heads, k_new, v_new, dil):
    hd = HEAD_DIM
    n = c_ref.shape[2]
    n_tiles = n // BLOCK
    s_tiles = [[None] * (n_tiles + 1) for _ in heads]

    def scores(t, x):
        for qi, (g, q_b, _) in enumerate(heads):
            s_tiles[qi][t] = jnp.sum(x[g * hd:(g + 1) * hd, :] * q_b, axis=0, keepdims=True)

    _shift_pass(c_ref, o_ref, i, 0, k_new, scores)
    scores(n_tiles, k_new)

    lane = lax.broadcasted_iota(jnp.int32, (1, n + BLOCK), 1)
    valid = jnp.logical_or(jnp.logical_and(lane < n, (lane & (dil - 1)) == 0), lane == n + BLOCK - 1)
    probs, dens, lses = [], [], []
    for qi, (g, q_b, sink) in enumerate(heads):
        s = jnp.where(valid, jnp.concatenate(s_tiles[qi], axis=1), NEG_INF)
        m = jnp.max(s, axis=1, keepdims=True)
        if sink is not None:
            m = jnp.maximum(m, sink)
        p = jnp.exp(s - m)
        den = jnp.sum(p, axis=1, keepdims=True)
        if sink is not None:
            den = den + jnp.exp(sink - m)
        probs.append(p)
        dens.append(den)
        lses.append(m + jnp.log(den))

    accs = [jnp.zeros((hd, BLOCK), F32) for _ in heads]

    def weighted(t, x):
        for qi, (g, _, _) in enumerate(heads):
            accs[qi] = accs[qi] + x[g * hd:(g + 1) * hd, :] * probs[qi][:, t * BLOCK:(t + 1) * BLOCK]

    _shift_pass(c_ref, o_ref, i, 2 * hd, v_new, weighted)
    weighted(n_tiles, v_new)

    outs = [jnp.sum(accs[qi], axis=1, keepdims=True) / dens[qi] for qi in range(len(heads))]
    return outs, lses


XT_KB, XT_VB, XT_KD, XT_VD = 0, BLOCK, 2 * BLOCK, 2 * BLOCK + MIX
XT_ROWS = 2 * BLOCK + 2 * MIX


def _decode_prep_body(pa_ref, pb_ref, pc_ref, pd_ref, st_ref, cw_ref, lg_ref, lb_ref,
                      wd_ref, bd_ref, cos_ref, sin_ref,
                      ya_ref, yc_ref, nst_ref, vn_ref, xq_ref, xt_ref):
    a_b = pa_ref[:, 0:MIX]
    z = pa_ref[:, MIX:2 * MIX] * pa_ref[:, 2 * MIX:3 * MIX]
    prev0 = st_ref[:, 0:MIX]
    prev1 = st_ref[:, MIX:2 * MIX]
    conv = cw_ref[0:1, :] * prev0 + cw_ref[1:2, :] * prev1 + cw_ref[2:3, :] * z
    ya_ref[...] = a_b * conv * _silu(pa_ref[:, 3 * MIX:4 * MIX])
    nst_ref[:, 0:MIX] = prev1
    nst_ref[:, MIX:2 * MIX] = z

    c_v = pc_ref[:, MIX:2 * MIX]
    mu = jnp.mean(c_v, axis=-1, keepdims=True)
    cen = c_v - mu
    var = jnp.mean(cen * cen, axis=-1, keepdims=True)
    vn = cen * lax.rsqrt(var + EPS) * lg_ref[...] + lb_ref[...]
    vn_ref[...] = vn
    mixed = wd_ref[...] * vn + bd_ref[...]
    yc_ref[...] = pc_ref[:, 0:MIX] * mixed * _silu(pc_ref[:, 2 * MIX:3 * MIX])

    cos = cos_ref[...]
    sin = sin_ref[...]
    cos3, sin3 = _tile_lanes(cos, 3), _tile_lanes(sin, 3)
    xq_ref[0:MIX, :] = (_rope(pb_ref[:, 0:MIX], cos3, sin3) * SCALE).T.astype(BF16)
    xq_ref[MIX:2 * MIX, :] = (_rope(pd_ref[:, 0:MIX], cos3, sin3) * SCALE).T.astype(BF16)
    xt_ref[XT_KB:XT_VB, :] = _rope(pb_ref[:, MIX:MIX + BLOCK], cos, sin).T
    xt_ref[XT_VB:XT_KD, :] = pb_ref[:, MIX + BLOCK:MIX + 2 * BLOCK].T
    xt_ref[XT_KD:XT_VD, :] = _rope(pd_ref[:, MIX:2 * MIX], cos3, sin3).T
    xt_ref[XT_VD:XT_ROWS, :] = pd_ref[:, 2 * MIX:3 * MIX].T


def _inproj_sample_body(x_ref, g_ref, w_ref, st_ref, cw_ref, lg_ref, lb_ref, wd_ref, bd_ref, cos_ref, sin_ref,
                        ya_ref, yc_ref, nst_ref, vn_ref, xq_ref, xt_ref, gb_ref, gd_ref, *rest, cast):
    if cast:
        wbf_ref = rest[0]
        rest = rest[1:]
        for j in range(0, IN_COLS, MATMUL_COL_CHUNK):
            cw = min(MATMUL_COL_CHUNK, IN_COLS - j)
            wbf_ref[:, j:j + cw] = w_ref[:, j:j + cw].astype(BF16)
        w_ref = wbf_ref
    pa_s, pb_s, pc_s, pd_s = rest
    h = _rms(x_ref[...], g_ref[...]).astype(BF16)
    for p_s, off in zip((pa_s, pb_s, pc_s, pd_s), (OFF_A, OFF_B, OFF_C, OFF_D)):
        def store(j, cw, acc, p_s=p_s):
            p_s[:, j:j + cw] = acc
        _project(h, w_ref, off, p_s.shape[-1], store)
    _decode_prep_body(pa_s, pb_s, pc_s, pd_s, st_ref, cw_ref, lg_ref, lb_ref, wd_ref, bd_ref, cos_ref, sin_ref,
                      ya_ref, yc_ref, nst_ref, vn_ref, xq_ref, xt_ref)
    gb_ref[...] = pb_s[:, MIX + 2 * BLOCK:COLS_B]
    gd_ref[...] = pd_s[:, 3 * MIX:4 * MIX]


def _inproj_sample(x, g, w, layer, state, conv_w, ln_g, ln_b, w_diag, b_diag, cos, sin):
    ns = x.shape[0]
    assert ns == BLOCK
    cast = w.dtype != BF16
    shapes = [((ns, MIX), F32), ((ns, MIX), F32), ((ns, 2 * MIX), F32), ((ns, MIX), F32),
              ((2 * MIX, ns), BF16), ((XT_ROWS, ns), F32), ((ns, MIX), F32), ((ns, MIX), F32)]
    out_specs = [_const_spec(s) for s, _ in shapes]
    if cast:
        shapes.append(((1, D_MODEL, IN_COLS), BF16))
        out_specs.append(pl.BlockSpec((None, D_MODEL, IN_COLS), lambda i: (0, 0, 0)))
    small = (state, conv_w, ln_g, ln_b, w_diag, b_diag, cos, sin)
    est = D_MODEL * IN_COLS * (2 + 4 * cast) * 2 + 8 * ns * IN_COLS * 4
    return pl.pallas_call(
        functools.partial(_inproj_sample_body, cast=cast),
        grid=(1,),
        in_specs=[_const_spec(x.shape), _const_spec(g.shape), _layer_spec(w.shape, layer)]
                 + [_const_spec(a.shape) for a in small],
        out_specs=out_specs,
        out_shape=[jax.ShapeDtypeStruct(s, d) for s, d in shapes],
        scratch_shapes=[pltpu.VMEM((ns, wd), F32) for wd in (COLS_A, COLS_B, COLS_C, COLS_D)],
        compiler_params=pltpu.CompilerParams(dimension_semantics=("arbitrary",),
                                             vmem_limit_bytes=_vmem_limit(est)),
        name="inproj_sample",
    )(x, g, w, *small)


def _decode_sequences(dec, step, sink_ref, xq_ref, xt_ref, cache_refs, out_refs):
    cs_ref, c1_ref, c4_ref, c16_ref = cache_refs
    ns_ref, n1_ref, n4_ref, n16_ref, yt_ref = out_refs
    n_seq = xq_ref.shape[1]
    hd = HEAD_DIM

    for i in range(dec.seq_per_step):
        b = dec.seq_base + step * dec.seq_per_step + i
        onehot = (lax.broadcasted_iota(jnp.int32, (n_seq, BLOCK), 0) == b).astype(BF16)
        q_b = _dot(xq_ref[...], onehot)
        new_rot = pltpu.roll(xt_ref[...], BLOCK - 1 - b, 1)
        heads = [(h // SWA_REP, q_b[h * hd:(h + 1) * hd], jnp.full((1, 1), sink_ref[h], F32))
                 for h in range(SWA_Q_HEADS)]
        outs, _ = _decode_cache(cs_ref, ns_ref, i, heads, new_rot[XT_KB:XT_VB], new_rot[XT_VB:XT_KD], 1)
        yb_col = jnp.concatenate(outs, axis=0)

        q_col = q_b[MIX:2 * MIX]
        k_new = new_rot[XT_KD:XT_VD]
        v_new = new_rot[XT_VD:XT_ROWS]
        o_heads, l_heads = [], []
        for g, (c_ref, n_ref, dil) in enumerate(zip((c1_ref, c4_ref, c16_ref),
                                                    (n1_ref, n4_ref, n16_ref), DILATIONS)):
            gc = slice(2 * g * hd, 2 * (g + 1) * hd)
            heads = [(hh, q_col[(2 * g + hh) * hd:(2 * g + hh + 1) * hd], None) for hh in range(2)]
            o, lse = _decode_cache(c_ref, n_ref, i, heads, k_new[gc], v_new[gc], dil)
            o_heads += o
            l_heads += lse
        cols = []
        for hh in range(2):
            ls = [l_heads[2 * g + hh] for g in range(3)]
            lmax = jnp.maximum(jnp.maximum(ls[0], ls[1]), ls[2])
            es = [jnp.exp(l - lmax) for l in ls]
            esum = es[0] + es[1] + es[2]
            cols.append([o_heads[2 * g + hh] * (es[g] / esum) for g in range(3)])
        yd_col = jnp.concatenate([cols[hh][g] for g in range(3) for hh in range(2)], axis=0)
        y_col = jnp.concatenate([yb_col, yd_col], axis=0)
        lane_y = lax.broadcasted_iota(jnp.int32, (2 * MIX, n_seq), 1)
        yt_ref[...] = jnp.where(lane_y == b, y_col, yt_ref[...])


class _DecodeStatic(NamedTuple):
    seq_base: int
    seq_per_step: int
    n_alias: int


class _Decode:
    def __init__(self, layer, seq_base, seq_per_step, n_sub, sinks, xq, xt, caches, prev_outs):
        self.layer, self.seq_base, self.seq_per_step, self.n_sub = layer, seq_base, seq_per_step, n_sub
        self.sinks, self.xq, self.xt = sinks, xq, xt
        self.caches, self.prev_outs = tuple(caches), tuple(prev_outs)
        self.static = _DecodeStatic(seq_base, seq_per_step, len(self.prev_outs))

    def extend(self, in_specs, out_specs, out_shape, args):
        layer, block0, n_sub = self.layer, self.seq_base // self.seq_per_step, self.n_sub
        linear = (lambda i: i) if n_sub == 1 else (lambda i, j: i * n_sub + j)
        spec = lambda c: pl.BlockSpec((None, self.seq_per_step) + c.shape[2:],
                                      lambda *ids: (layer, block0 + linear(*ids), 0, 0))
        first_alias_in = 1 + len(args) + 2 + len(self.caches)
        aliases = {first_alias_in + i: len(out_specs) + i for i in range(len(self.prev_outs))}
        in_specs = ([pl.BlockSpec(memory_space=pltpu.SMEM)] + in_specs
                    + [_const_spec(self.xq.shape), _const_spec(self.xt.shape)]
                    + [spec(c) for c in self.caches]
                    + [pl.BlockSpec(memory_space=pl.ANY) for _ in self.prev_outs])
        args = [self.sinks] + args + [self.xq, self.xt, *self.caches, *self.prev_outs]
        yt_shape = (2 * MIX, self.xq.shape[1])
        out_specs = out_specs + [spec(c) for c in self.caches] + [_const_spec(yt_shape)]
        out_shape = (out_shape + [jax.ShapeDtypeStruct(c.shape, F32) for c in self.caches]
                     + [jax.ShapeDtypeStruct(yt_shape, F32)])
        rows = sum(c.shape[3] for c in self.caches)
        extra = 4 * self.seq_per_step * KV_CH * rows * 4 + 4 * (2 * MIX + XT_ROWS) * BLOCK * 4
        return in_specs, out_specs, out_shape, args, aliases, extra


def _rope_lanes():
    lane = jnp.arange(BLOCK)
    inv_freq = ROPE_THETA ** (-(lane % HALF_HEAD).astype(F32) / HALF_HEAD)
    sign = jnp.where((lane // HALF_HEAD) % 2 == 0, -1.0, 1.0).astype(F32)
    return inv_freq, sign


def _rope_tables_prefix(n):
    assert n % BLOCK == 0
    inv_freq, sign = _rope_lanes()
    ang_hi = (jnp.arange(n // BLOCK, dtype=F32) * BLOCK)[:, None] * inv_freq[None, :]
    ang_lo = jnp.arange(BLOCK, dtype=F32)[:, None] * inv_freq[None, :]
    ch, sh = jnp.cos(ang_hi)[:, None, :], jnp.sin(ang_hi)[:, None, :]
    cl, sl = jnp.cos(ang_lo)[None, :, :], jnp.sin(ang_lo)[None, :, :]
    cos = ch * cl - sh * sl
    sin = (sh * cl + ch * sl) * sign
    return cos.reshape(n, BLOCK), sin.reshape(n, BLOCK)


def _rope_tables_at(pos):
    inv_freq, sign = _rope_lanes()
    ang = pos.astype(F32)[:, None] * inv_freq[None, :]
    return jnp.cos(ang), jnp.sin(ang) * sign


def _cache_view(c):
    d, b, n = c.shape[:3]
    return jnp.transpose(c, (0, 1, 3, 4, 5, 2)).reshape(d, b, KV_CH, n)


def _cache_unview(c):
    lead, n = c.shape[:-2], c.shape[-1]
    nl = len(lead)
    perm = tuple(range(nl)) + (nl + 3, nl, nl + 1, nl + 2)
    return jnp.transpose(c.reshape(lead + (2, 2, HEAD_DIM, n)), perm)


def kernel(x_prompt, x_sample, state_conv, cache_swa_kv, cache_dil1_kv, cache_dil4_kv, cache_dil16_kv,
           norm_g, w_in, conv_w, attn_sinks, v_ln_g, v_ln_b, w_spatial, b_spatial,
           w_branch, w_merge, w_out, final_norm_g):
    depth = w_in.shape[0]
    nb, seq, _ = x_prompt.shape
    ns, dec_seq, _ = x_sample.shape
    assert dec_seq == 1 and seq % SUPER == 0
    assert cache_swa_kv.shape[2] == BLOCK and cache_dil16_kv.shape[2] == SUPER

    cos_p, sin_p = _rope_tables_prefix(seq)
    cos_s, sin_s = _rope_tables_at(PAST_LEN + jnp.arange(1, dtype=jnp.int32))
    assert depth == 2
    casts = ((w_in.reshape(depth * D_MODEL, IN_COLS), D_MODEL, D_MODEL),
             (w_merge.reshape(depth * D_MODEL, 4 * D_MODEL), depth * D_MODEL, 0),
             (w_branch.reshape(depth * 4 * MIX, D_MODEL), depth * 4 * MIX, 0),
             (w_out.reshape(depth * D_MODEL, D_MODEL), depth * D_MODEL, 0))
    w_in_l = w_in
    caches = [_cache_view(c) for c in (cache_swa_kv, cache_dil1_kv, cache_dil4_kv, cache_dil16_kv)]
    fg = final_norm_g.reshape(1, D_MODEL)

    xp = x_prompt.reshape(nb * seq, D_MODEL)
    xs = x_sample.reshape(ns, D_MODEL)
    conv_s, chunk_v = [], []
    state_p = ()
    dil_p = ()
    new_s = ()
    for l in range(depth):
        g = norm_g[l].reshape(1, D_MODEL)
        final = l == depth - 1
        ln_g = v_ln_g[l].reshape(1, MIX)
        ln_b = v_ln_b[l].reshape(1, MIX)

        w_diag = jnp.repeat(w_spatial[l][:, 0, 0], HEAD_DIM).reshape(1, MIX)
        b_diag = jnp.repeat(b_spatial[l][:, 0], HEAD_DIM).reshape(1, MIX)
        za, zc, nstate, vn, xq, xt, gate_b, gate_d, *w_new = _inproj_sample(
            xs, g, w_in_l, 0, state_conv[l].reshape(ns, 2 * MIX), conv_w[l], ln_g, ln_b, w_diag, b_diag,
            cos_s, sin_s)
        if w_new:
            w_in_l, = w_new

        tm = 512
        tiles = nb * seq // tm
        half = tiles * N_SUB
        assert ns - half == 2 * tiles
        b_tile = jnp.repeat(b_spatial[l].T, HEAD_DIM, axis=1)
        dec = _Decode(l, 0, 1, N_SUB, attn_sinks[l], xq, xt, caches, new_s)
        ya, yb, yc, qd, kd, vd, gd, conv_p, swa_p, *rest = _inproj_mix(
            xp, g, w_in_l, 0, l, depth, cos_p, sin_p, conv_w[l], ln_g, ln_b, w_spatial[l], b_tile, tm, seq,
            dec, state_p, casts)
        state_p = (conv_p, swa_p)
        if casts:
            w_next, w_merge_bf, w_branch_bf, w_out_bf = rest[:4]
            w_in_next = w_next.reshape(1, D_MODEL, IN_COLS)
            w_merge_bf = w_merge_bf.reshape(depth, D_MODEL, 4 * D_MODEL)
            w_branch_bf = w_branch_bf.reshape(depth, 4, MIX, D_MODEL)
            w_out_bf = w_out_bf.reshape(depth, D_MODEL, D_MODEL)
            rest = rest[4:]
            casts = ()
        ns0, ns1, ns4, ns16, yt0 = rest
        per_seq = lambda a: a.reshape(D_GROUPS, nb, seq, D_LANES)
        yd, *dil_p = _mix_d(per_seq(qd), per_seq(kd), per_seq(vd), gd.reshape(nb, seq, MIX), l, depth, dil_p)
        dec = _Decode(l, half, 2, 1, attn_sinks[l], xq, xt, caches, (ns0, ns1, ns4, ns16))
        xp, *new_s, yt1 = _outproj(xp, (ya, yb, yc, yd.reshape(nb * seq, MIX)), g, w_merge_bf,
                                   w_branch_bf, w_out_bf, fg, l, tm, final, dec)

        xs = _outproj_sample(xs, za, zc, yt0, yt1, gate_b, gate_d, g, w_merge_bf, w_branch_bf, w_out_bf, fg,
                             l, final, half)
        conv_s.append(nstate.reshape(ns, 2, MIX))
        chunk_v.append(vn.reshape(ns, 1, MIX))
        w_in_l = w_in_next

    st = jnp.stack
    conv_p, swa_p = state_p
    new_p = [_cache_unview(c) for c in (swa_p, *dil_p)]
    new_s = [_cache_unview(c) for c in new_s]
    return (xp.reshape(nb, seq, D_MODEL), xs.reshape(ns, 1, D_MODEL),
            conv_p, st(conv_s),
            new_p[0], new_s[0], new_p[1], new_s[1],
            new_p[2], new_s[2], new_p[3], new_s[3],
            st(chunk_v))
```
